```python
import jax, jax.numpy as jnp
from jax import lax
import numpy as np

D_MODEL = 1024
BATCH = 32
SEQ = 256
DEPTH = 4
DEC_BATCH = 8
DEC_SEQ = 4096
PAST_LEN = 512

GRID_W = 64
N_MIXERS = 3
N_CONV_LAYERS = (DEPTH + 2) // 3
N_GDN_LAYERS = (DEPTH + 1) // 3
N_NA_LAYERS = DEPTH // 3
CONV_WIDTH = 31
GDN_DK = 128
GDN_DV = 128
GDN_HEADS = D_MODEL // GDN_DK
GDN_CONV = 5
GDN_CHUNK = 64
NA_HEADS = 16
NA_HEAD_DIM = D_MODEL // NA_HEADS
NA_WIN_R = 8
NA_WIN_C = 16
NA_QCOLS = 16
NA_KCOLS = 32
CTX_QBLOCK = 128
MOE_GROUPS = 4
MOE_EXPERTS_PER_GROUP = 4
MOE_TOP_K = 2
MOE_D_FF = 256
ADA_CHUNKS = 6
RMS_EPS = 1e-6
LN_EPS = 1e-5

kernel_name = "hybrid_flow_prefix_trunk_step"


def _rmsnorm(x, gain):
    xf = x.astype(jnp.float32)
    y = xf * lax.rsqrt(jnp.mean(xf * xf, axis=-1, keepdims=True) + RMS_EPS)
    return (y * gain.astype(jnp.float32)).astype(x.dtype)


def _layernorm(x, gain, bias):
    xf = x.astype(jnp.float32)
    mu = jnp.mean(xf, axis=-1, keepdims=True)
    var = jnp.mean(jnp.square(xf - mu), axis=-1, keepdims=True)
    y = (xf - mu) * lax.rsqrt(var + LN_EPS)
    return (y * gain.astype(jnp.float32) + bias.astype(jnp.float32)).astype(x.dtype)


def _l2norm(x):
    xf = x.astype(jnp.float32)
    return xf * lax.rsqrt(jnp.sum(xf * xf, axis=-1, keepdims=True) + 1e-6)


def _depthwise_conv(x, w):
    width = w.shape[0]
    return lax.conv_general_dilated(
        x, w[:, None, :].astype(x.dtype), window_strides=(1,),
        padding=[(width // 2, width // 2)],
        dimension_numbers=("NWC", "WIO", "NWC"),
        feature_group_count=x.shape[-1])


def _adaln(cond, w, b):
    m = jax.nn.silu(cond) @ w + b
    return jnp.split(m[..., None, :], ADA_CHUNKS, axis=-1)


def _merge_heads(o):
    B, H, L, d = o.shape
    return o.transpose(0, 2, 1, 3).reshape(B, L, H * d)


def _conv_module(h, pw1_w, pw1_b, dw_w, dw_b, ln_g, ln_b, pw2_w, pw2_b):
    u = h @ pw1_w + pw1_b
    a, g = jnp.split(u, 2, axis=-1)
    u = a * jax.nn.sigmoid(g)
    u = _depthwise_conv(u, dw_w) + dw_b
    u = jax.nn.silu(_layernorm(u, ln_g, ln_b))
    return u @ pw2_w + pw2_b


def _chunk_gated_delta(q, k, v, g, beta, s0):
    B, H, L, DK = q.shape
    DV = v.shape[-1]
    C = GDN_CHUNK
    N = L // C
    q = q * (DK ** -0.5)

    def to_chunks(t):
        return t.reshape((B, H, N, C) + t.shape[3:])

    q, k, v, g, beta = (to_chunks(t) for t in (q, k, v, g, beta))
    g = jnp.cumsum(g, axis=-1)
    incl = jnp.tril(jnp.ones((C, C), bool))
    strict = jnp.tril(jnp.ones((C, C), bool), -1)
    diff = g[..., :, None] - g[..., None, :]
    decay = jnp.where(incl, jnp.exp(jnp.where(incl, diff, 0.0)), 0.0)
    kb = k * beta[..., None]
    a_mat = jnp.eye(C, dtype=jnp.float32) + jnp.where(
        strict, jnp.einsum("bhnik,bhnjk->bhnij", kb, k) * decay, 0.0)
    rhs = jnp.concatenate([v * beta[..., None], kb * jnp.exp(g)[..., None]], axis=-1)
    sol = lax.linalg.triangular_solve(a_mat, rhs, left_side=True, lower=True, unit_diagonal=True)
    u, w = sol[..., :DV], sol[..., DV:]
    qk = jnp.where(incl, jnp.einsum("bhnik,bhnjk->bhnij", q, k) * decay, 0.0)
    q_g = q * jnp.exp(g)[..., None]
    g_last = g[..., -1]
    k_tail = k * jnp.exp(g_last[..., None] - g)[..., None]
    xs = tuple(jnp.moveaxis(t, 2, 0) for t in (u, w, qk, q_g, k_tail, g_last))

    def step(s, inp):
        u_n, w_n, qk_n, qg_n, kt_n, gl_n = inp
        v_new = u_n - jnp.einsum("bhck,bhkv->bhcv", w_n, s)
        o_n = jnp.einsum("bhck,bhkv->bhcv", qg_n, s) + jnp.einsum("bhij,bhjv->bhiv", qk_n, v_new)
        s = s * jnp.exp(gl_n)[..., None, None] + jnp.einsum("bhck,bhcv->bhkv", kt_n, v_new)
        return s, o_n

    s_fin, o = lax.scan(step, s0, xs)
    o = jnp.moveaxis(o, 0, 2).reshape(B, H, L, DV)
    return o, s_fin


def _gdn_mixer(h, s0, w_qkvz, conv_w, w_ba, a_log, dt_bias, o_g, w_o):
    B, L, _ = h.shape
    H, DK, DV = GDN_HEADS, GDN_DK, GDN_DV
    n_qkv = 2 * H * DK + H * DV
    qkvz = h @ w_qkvz
    qkv, z = qkvz[..., :n_qkv], qkvz[..., n_qkv:]
    qkv = jax.nn.silu(_depthwise_conv(qkv, conv_w))
    q, k, v = jnp.split(qkv, [H * DK, 2 * H * DK], axis=-1)
    q = _l2norm(q.reshape(B, L, H, DK)).transpose(0, 2, 1, 3)
    k = _l2norm(k.reshape(B, L, H, DK)).transpose(0, 2, 1, 3)
    v = v.reshape(B, L, H, DV).transpose(0, 2, 1, 3).astype(jnp.float32)
    ba = (h @ w_ba).astype(jnp.float32).reshape(B, L, 2, 2, H)
    beta = jax.nn.sigmoid(ba[:, :, :, 0]).transpose(0, 2, 3, 1)
    g = (-jnp.exp(a_log.astype(jnp.float32))
         * jax.nn.softplus(ba[:, :, :, 1] + dt_bias.astype(jnp.float32))).transpose(0, 2, 3, 1)
    s0 = s0.astype(jnp.float32)
    o_f, s_f = _chunk_gated_delta(q, k, v, g[:, 0], beta[:, 0], s0[:, 0])
    o_b, s_b = _chunk_gated_delta(jnp.flip(q, 2), jnp.flip(k, 2), jnp.flip(v, 2),
                                  jnp.flip(g[:, 1], -1), jnp.flip(beta[:, 1], -1), s0[:, 1])
    o = o_f + jnp.flip(o_b, 2)
    o = _rmsnorm(o, o_g)
    o = _merge_heads(o).astype(h.dtype) * jax.nn.silu(z)
    return o @ w_o, jnp.stack([s_f, s_b], axis=1)


def _na_project(h, w_qkv):
    B, L, _ = h.shape
    qkv = (h @ w_qkv).reshape(B, L, 3, NA_HEADS, NA_HEAD_DIM)
    return tuple(qkv[:, :, i].transpose(0, 2, 1, 3) for i in range(3))


def _ctx_attention(q, k, v):
    B, H, L, dh = q.shape
    nb = L // CTX_QBLOCK
    qb = jnp.moveaxis(q.reshape(B, H, nb, CTX_QBLOCK, dh), 2, 0)
    scale = dh ** -0.5

    def blk(qi):
        s = jnp.einsum("bhqd,bhkd->bhqk", qi, k).astype(jnp.float32) * scale
        p = jax.nn.softmax(s, axis=-1)
        return jnp.einsum("bhqk,bhkd->bhqd", p.astype(v.dtype), v)

    o = lax.map(blk, qb)
    return jnp.moveaxis(o, 0, 2).reshape(B, H, L, dh)


def _na_latent_attention(q, k, v, k_ctx, v_ctx, rpb):
    B, H, L, dh = q.shape
    rows = L // GRID_W
    kr = min(NA_WIN_R, rows)
    ncb = GRID_W // NA_QCOLS
    scale = dh ** -0.5
    qcol = np.arange(GRID_W).reshape(ncb, NA_QCOLS)
    cstart = np.clip(qcol - NA_WIN_C // 2, 0, GRID_W - NA_WIN_C)
    kstart = np.clip(np.arange(ncb) * NA_QCOLS - NA_WIN_C // 2, 0, GRID_W - NA_KCOLS)
    kcol = kstart[:, None] + np.arange(NA_KCOLS)
    col_ok = ((kcol[:, None, :] >= cstart[:, :, None])
              & (kcol[:, None, :] < cstart[:, :, None] + NA_WIN_C))
    dc_idx = np.clip(kcol[:, None, :] - qcol[:, :, None], -(NA_WIN_C - 1), NA_WIN_C - 1) + NA_WIN_C - 1
    mask = np.broadcast_to(col_ok[:, :, None, :], (ncb, NA_QCOLS, kr, NA_KCOLS)).reshape(
        ncb, NA_QCOLS, kr * NA_KCOLS)
    n_loc = kr * NA_KCOLS
    qg = q.reshape(B, H, rows, GRID_W, dh)
    kg = k.reshape(B, H, rows, GRID_W, dh)
    vg = v.reshape(B, H, rows, GRID_W, dh)

    def row_fn(r):
        rs = jnp.clip(r - kr // 2, 0, rows - kr)
        q_r = lax.dynamic_index_in_dim(qg, r, axis=2, keepdims=False).reshape(B, H, ncb, NA_QCOLS, dh)
        k_s = lax.dynamic_slice_in_dim(kg, rs, kr, axis=2)[:, :, :, kcol]
        v_s = lax.dynamic_slice_in_dim(vg, rs, kr, axis=2)[:, :, :, kcol]
        k_s = k_s.transpose(0, 1, 3, 2, 4, 5).reshape(B, H, ncb, n_loc, dh)
        v_s = v_s.transpose(0, 1, 3, 2, 4, 5).reshape(B, H, ncb, n_loc, dh)
        dr_idx = rs + jnp.arange(kr) - r + NA_WIN_R - 1
        bias = rpb[:, dr_idx[:, None, None, None], dc_idx[None]]
        bias = bias.transpose(0, 2, 3, 1, 4).reshape(H, ncb, NA_QCOLS, n_loc)
        s_loc = jnp.einsum("bhnqd,bhnkd->bhnqk", q_r, k_s).astype(jnp.float32) * scale \
            + bias.astype(jnp.float32)
        s_loc = jnp.where(mask, s_loc, -jnp.inf)
        s_ctx = jnp.einsum("bhnqd,bhkd->bhnqk", q_r, k_ctx).astype(jnp.float32) * scale
        p = jax.nn.softmax(jnp.concatenate([s_loc, s_ctx], axis=-1), axis=-1).astype(v.dtype)
        o = jnp.einsum("bhnqk,bhnkd->bhnqd", p[..., :n_loc], v_s) \
            + jnp.einsum("bhnqk,bhkd->bhnqd", p[..., n_loc:], v_ctx)
        return o.reshape(B, H, GRID_W, dh)

    o = lax.map(row_fn, jnp.arange(rows))
    return jnp.moveaxis(o, 0, 2).reshape(B, H, L, dh)


def _na_context(h, w_qkv, w_o):
    q, k, v = _na_project(h, w_qkv)
    return _merge_heads(_ctx_attention(q, k, v)) @ w_o, k, v


def _na_latent(h, k_ctx, v_ctx, w_qkv, rpb, w_o):
    q, k, v = _na_project(h, w_qkv)
    o = _na_latent_attention(q, k, v, k_ctx.astype(q.dtype), v_ctx.astype(q.dtype), rpb)
    return _merge_heads(o) @ w_o


def _hier_moe(h, wg, bg, we, be, w1, w3, w2):
    B, L, D = h.shape
    T = B * L
    G, E = MOE_GROUPS, MOE_EXPERTS_PER_GROUP
    x = h.reshape(T, D)
    pg = jax.nn.softmax((x @ wg + bg).astype(jnp.float32), axis=-1)
    g_sel = jnp.argmax(pg, axis=-1)
    g_w = jnp.take_along_axis(pg, g_sel[:, None], axis=-1)
    le = (x @ we + be).astype(jnp.float32).reshape(T, G, E)
    le = jnp.take_along_axis(le, g_sel[:, None, None], axis=1)[:, 0]
    top_v, top_i = lax.top_k(jax.nn.softmax(le, axis=-1), MOE_TOP_K)
    top_v = top_v / jnp.sum(top_v, axis=-1, keepdims=True) * g_w
    w_exp = jnp.sum(jax.nn.one_hot(top_i, E, dtype=jnp.float32) * top_v[..., None], axis=1)
    combine = (jax.nn.one_hot(g_sel, G, dtype=jnp.float32)[:, :, None]
               * w_exp[:, None, :]).astype(x.dtype)
    out = jnp.zeros((T, D), x.dtype)
    for gi in range(G):
        a = jnp.einsum("td,edf->tef", x, w1[gi])
        b = jnp.einsum("td,edf->tef", x, w3[gi])
        hid = jax.nn.silu(a) * b * combine[:, gi, :, None]
        out = out + jnp.einsum("tef,efd->td", hid, w2[gi])
    return out.reshape(B, L, D)


def _trunk(x, cond, P, cache):
    latent = cache is not None
    gdn_states, na_ks, na_vs = [], [], []
    for i in range(DEPTH):
        sh1, sc1, gt1, sh2, sc2, gt2 = _adaln(cond, P["ada_w"][i], P["ada_b"][i])
        h = _rmsnorm(x, P["norm1_g"][i]) * (1 + sc1) + sh1
        kind, j = i % N_MIXERS, i // N_MIXERS
        if kind == 0:
            y = _conv_module(h, P["cv_pw1_w"][j], P["cv_pw1_b"][j], P["cv_dw_w"][j], P["cv_dw_b"][j],
                             P["cv_ln_g"][j], P["cv_ln_b"][j], P["cv_pw2_w"][j], P["cv_pw2_b"][j])
        elif kind == 1:
            if latent:
                s0 = cache[0][:, j]
            else:
                s0 = jnp.zeros((x.shape[0], 2, GDN_HEADS, GDN_DK, GDN_DV), jnp.float32)
            y, s_fin = _gdn_mixer(h, s0, P["gdn_w_qkvz"][j], P["gdn_conv_w"][j], P["gdn_w_ba"][j],
                                  P["gdn_a_log"][j], P["gdn_dt_bias"][j], P["gdn_o_g"][j], P["gdn_w_o"][j])
            gdn_states.append(s_fin)
        else:
            if latent:
                y = _na_latent(h, cache[1][:, j], cache[2][:, j],
                               P["na_w_qkv"][j], P["na_rpb"][j], P["na_w_o"][j])
            else:
                y, k_ctx, v_ctx = _na_context(h, P["na_w_qkv"][j], P["na_w_o"][j])
                na_ks.append(k_ctx)
                na_vs.append(v_ctx)
        x = x + gt1 * y
        h = _rmsnorm(x, P["norm2_g"][i]) * (1 + sc2) + sh2
        x = x + gt2 * _hier_moe(h, P["moe_wg"][i], P["moe_bg"][i], P["moe_we"][i], P["moe_be"][i],
                                P["moe_w1"][i], P["moe_w3"][i], P["moe_w2"][i])
    return _rmsnorm(x, P["final_norm_g"]), gdn_states, na_ks, na_vs


def setup_inputs(seed: int = 0) -> dict:
    key = jax.random.key(seed)
    ks = iter(jax.random.split(key, 64))
    D = D_MODEL
    G, E, F = MOE_GROUPS, MOE_EXPERTS_PER_GROUP, MOE_D_FF
    H, DK, DV = GDN_HEADS, GDN_DK, GDN_DV
    Nc, Ng, Nn = N_CONV_LAYERS, N_GDN_LAYERS, N_NA_LAYERS

    def nrm(shape, scale):
        return jax.random.normal(next(ks), shape, jnp.float32) * scale

    def gain(shape):
        return 1.0 + nrm(shape, 0.1)

    dt = jnp.exp(jax.random.uniform(next(ks), (Ng, 2, H), jnp.float32,
                                    minval=float(np.log(1e-3)), maxval=float(np.log(1e-1))))
    a_log = jnp.log(jax.random.uniform(next(ks), (Ng, 2, H), jnp.float32, minval=1.0, maxval=16.0))
    return {
        "x_prompt": nrm((BATCH, SEQ, D), 1.0),
        "x_sample": nrm((DEC_BATCH, DEC_SEQ, D), 1.0),
        "state_gdn": nrm((DEC_BATCH, Ng, 2, H, DK, DV), 0.1),
        "cache_na_k": nrm((DEC_BATCH, Nn, NA_HEADS, PAST_LEN, NA_HEAD_DIM), 1.0),
        "cache_na_v": nrm((DEC_BATCH, Nn, NA_HEADS, PAST_LEN, NA_HEAD_DIM), 1.0),
        "c": nrm((DEC_BATCH, D), 1.0),
        "c_ctx": nrm((D,), 1.0),
        "ada_w": nrm((DEPTH, D, ADA_CHUNKS * D), 0.5 * D ** -0.5),
        "ada_b": nrm((DEPTH, ADA_CHUNKS * D), 0.02),
        "norm1_g": gain((DEPTH, D)),
        "norm2_g": gain((DEPTH, D)),
        "cv_pw1_w": nrm((Nc, D, 2 * D), D ** -0.5),
        "cv_pw1_b": nrm((Nc, 2 * D), 0.02),
        "cv_dw_w": nrm((Nc, CONV_WIDTH, D), CONV_WIDTH ** -0.5),
        "cv_dw_b": nrm((Nc, D), 0.02),
        "cv_ln_g": gain((Nc, D)),
        "cv_ln_b": nrm((Nc, D), 0.02),
        "cv_pw2_w": nrm((Nc, D, D), D ** -0.5),
        "cv_pw2_b": nrm((Nc, D), 0.02),
        "gdn_w_qkvz": nrm((Ng, D, 2 * H * DK + 2 * H * DV), D ** -0.5),
        "gdn_conv_w": nrm((Ng, GDN_CONV, 2 * H * DK + H * DV), GDN_CONV ** -0.5),
        "gdn_w_ba": nrm((Ng, D, 4 * H), D ** -0.5),
        "gdn_a_log": a_log,
        "gdn_dt_bias": dt + jnp.log(-jnp.expm1(-dt)),
        "gdn_o_g": gain((Ng, DV)),
        "gdn_w_o": nrm((Ng, H * DV, D), (H * DV) ** -0.5),
        "na_w_qkv": nrm((Nn, D, 3 * D), D ** -0.5),
        "na_rpb": nrm((Nn, NA_HEADS, 2 * NA_WIN_R - 1, 2 * NA_WIN_C - 1), 0.2),
        "na_w_o": nrm((Nn, D, D), D ** -0.5),
        "moe_wg": nrm((DEPTH, D, G), D ** -0.5),
        "moe_bg": nrm((DEPTH, G), 0.01),
        "moe_we": nrm((DEPTH, D, G * E), D ** -0.5),
        "moe_be": nrm((DEPTH, G * E), 0.01),
        "moe_w1": nrm((DEPTH, G, E, D, F), D ** -0.5),
        "moe_w3": nrm((DEPTH, G, E, D, F), D ** -0.5),
        "moe_w2": nrm((DEPTH, G, E, F, D), F ** -0.5),
        "final_norm_g": gain((D,)),
    }


def reference(x_prompt, x_sample, state_gdn, cache_na_k, cache_na_v, c, c_ctx,
              ada_w, ada_b, norm1_g, norm2_g,
              cv_pw1_w, cv_pw1_b, cv_dw_w, cv_dw_b, cv_ln_g, cv_ln_b, cv_pw2_w, cv_pw2_b,
              gdn_w_qkvz, gdn_conv_w, gdn_w_ba, gdn_a_log, gdn_dt_bias, gdn_o_g, gdn_w_o,
              na_w_qkv, na_rpb, na_w_o,
              moe_wg, moe_bg, moe_we, moe_be, moe_w1, moe_w3, moe_w2,
              final_norm_g):
    P = dict(ada_w=ada_w, ada_b=ada_b, norm1_g=norm1_g, norm2_g=norm2_g,
             cv_pw1_w=cv_pw1_w, cv_pw1_b=cv_pw1_b, cv_dw_w=cv_dw_w, cv_dw_b=cv_dw_b,
             cv_ln_g=cv_ln_g, cv_ln_b=cv_ln_b, cv_pw2_w=cv_pw2_w, cv_pw2_b=cv_pw2_b,
             gdn_w_qkvz=gdn_w_qkvz, gdn_conv_w=gdn_conv_w, gdn_w_ba=gdn_w_ba, gdn_a_log=gdn_a_log,
             gdn_dt_bias=gdn_dt_bias, gdn_o_g=gdn_o_g, gdn_w_o=gdn_w_o,
             na_w_qkv=na_w_qkv, na_rpb=na_rpb, na_w_o=na_w_o,
             moe_wg=moe_wg, moe_bg=moe_bg, moe_we=moe_we, moe_be=moe_be,
             moe_w1=moe_w1, moe_w3=moe_w3, moe_w2=moe_w2, final_norm_g=final_norm_g)
    y_prompt, gdn_states, na_ks, na_vs = _trunk(x_prompt, c_ctx, P, None)
    y_sample, _, _, _ = _trunk(x_sample, c, P, (state_gdn, cache_na_k, cache_na_v))
    state_gdn_new = jnp.stack(gdn_states, axis=1)
    cache_na_k_new = jnp.stack(na_ks, axis=1)
    cache_na_v_new = jnp.stack(na_vs, axis=1)
    return (y_prompt, y_sample, state_gdn_new, cache_na_k_new, cache_na_v_new)
```

```python
import functools

import numpy as np
import jax
import jax.numpy as jnp
from jax import lax
from jax.experimental import pallas as pl
from jax.experimental.pallas import tpu as pltpu

F32 = jnp.float32
BF16 = jnp.bfloat16

D_MODEL = 1024
DEPTH = 4
N_MIXERS = 3
ADA_CHUNKS = 6
RMS_EPS = 1e-6
LN_EPS = 1e-5
CONV_WIDTH = 31
CONV_HALO = 16
GDN_H = 8
GDN_DK = 128
GDN_CONV = 5
GDN_HALO = 8
GDN_CHUNK = 64
GDN_SUB = 16
NA_HEADS = 16
NA_DH = 64
NA_WIN_R = 8
NA_WIN_C = 16
GRID_W = 64
MOE_G = 4
MOE_E = 4
MOE_F = 256
LANES = 128
NEG_BIG = -1e30

VMEM_LIMIT_BYTES = 48 * 1024 * 1024


def _cparams(*sem):
    return pltpu.CompilerParams(dimension_semantics=sem, vmem_limit_bytes=VMEM_LIMIT_BYTES)


def _sigmoid(x):
    return 1.0 / (1.0 + jnp.exp(-x))


def _silu(x):
    return x * _sigmoid(x)


def _norm_mod(x, g, sc, sh):
    ms = jnp.mean(x * x, axis=-1, keepdims=True)
    y = x * lax.rsqrt(ms + RMS_EPS)
    return (y * g) * (1.0 + sc) + sh


def _dot(a, b):
    return jnp.dot(a, b, preferred_element_type=F32)


def _dot_nt(a, b):
    return lax.dot_general(a, b, (((1,), (1,)), ((), ())), preferred_element_type=F32)


def _dot_f32(a, b):
    return jnp.dot(a, b, preferred_element_type=F32, precision=lax.Precision.HIGHEST)


def _split_bf16(w):
    hi = w.astype(BF16)
    lo = (w - hi.astype(F32)).astype(BF16)
    return hi, lo


def _mod_spec(mod):
    if mod.shape[0] == 1:
        return pl.BlockSpec((None, 1, D_MODEL), lambda b, *_: (0, 0, 0))
    return pl.BlockSpec((None, 1, D_MODEL), lambda b, *_: (b, 0, 0))


def _vec_spec(n):
    return pl.BlockSpec((1, n), lambda *_: (0, 0))


def _ada_kernel(c_ref, w_ref, b_ref, o_ref):
    s = _silu(c_ref[...]).astype(BF16)
    o_ref[...] = _dot(s, w_ref[...].astype(BF16)) + b_ref[...]


def _ada_all(cond, ada_w, ada_b, tn=1024):
    R = cond.shape[0]
    N = ada_w.shape[-1]
    return pl.pallas_call(
        _ada_kernel,
        out_shape=jax.ShapeDtypeStruct((DEPTH, R, N), F32),
        grid=(DEPTH, N // tn),
        in_specs=[pl.BlockSpec((R, D_MODEL), lambda l, j: (0, 0)),
                  pl.BlockSpec((None, D_MODEL, tn), lambda l, j: (l, 0, j)),
                  pl.BlockSpec((None, 1, tn), lambda l, j: (l, 0, j))],
        out_specs=pl.BlockSpec((None, R, tn), lambda l, j: (l, 0, j)),
        compiler_params=_cparams("parallel", "parallel"),
        name="ada",
    )(cond, ada_w, ada_b.reshape(DEPTH, 1, N))


def _nm_linear_kernel(x_ref, g_ref, sc_ref, sh_ref, w_ref, o_ref, h_ref):
    @pl.when(pl.program_id(2) == 0)
    def _():
        h_ref[...] = _norm_mod(x_ref[...], g_ref[...], sc_ref[...], sh_ref[...]).astype(BF16)

    o_ref[...] = _dot(h_ref[...], w_ref[...]).astype(o_ref.dtype)


def _nm_linear(x, g, sc, sh, w, tm, tn):
    B, L, _ = x.shape
    N = w.shape[1]
    return pl.pallas_call(
        _nm_linear_kernel,
        out_shape=jax.ShapeDtypeStruct((B, L, N), F32),
        grid=(B, L // tm, N // tn),
        in_specs=[pl.BlockSpec((None, tm, D_MODEL), lambda b, i, j: (b, i, 0)),
                  _vec_spec(D_MODEL), _mod_spec(sc), _mod_spec(sh),
                  pl.BlockSpec((D_MODEL, tn), lambda b, i, j: (0, j))],
        out_specs=pl.BlockSpec((None, tm, tn), lambda b, i, j: (b, i, j)),
        scratch_shapes=[pltpu.VMEM((tm, D_MODEL), BF16)],
        compiler_params=_cparams("parallel", "parallel", "arbitrary"),
        name="nm_linear",
    )(x, g, sc, sh, w)


def _nm_glu_kernel(x_ref, g_ref, sc_ref, sh_ref, wa_ref, wg_ref, ba_ref, bg_ref, o_ref, h_ref):
    @pl.when(pl.program_id(2) == 0)
    def _():
        h_ref[...] = _norm_mod(x_ref[...], g_ref[...], sc_ref[...], sh_ref[...]).astype(BF16)

    h = h_ref[...]
    a = _dot(h, wa_ref[...]) + ba_ref[...]
    gate = _dot(h, wg_ref[...]) + bg_ref[...]
    o_ref[...] = a * _sigmoid(gate)


def _nm_glu(x, g, sc, sh, w, b, tm, tn):
    B, L, _ = x.shape
    N = w.shape[1] // 2
    nj = N // tn
    return pl.pallas_call(
        _nm_glu_kernel,
        out_shape=jax.ShapeDtypeStruct((B, L, N), F32),
        grid=(B, L // tm, nj),
        in_specs=[pl.BlockSpec((None, tm, D_MODEL), lambda b_, i, j: (b_, i, 0)),
                  _vec_spec(D_MODEL), _mod_spec(sc), _mod_spec(sh),
                  pl.BlockSpec((D_MODEL, tn), lambda b_, i, j: (0, j)),
                  pl.BlockSpec((D_MODEL, tn), lambda b_, i, j: (0, j + nj)),
                  pl.BlockSpec((1, tn), lambda b_, i, j: (0, j)),
                  pl.BlockSpec((1, tn), lambda b_, i, j: (0, j + nj))],
        out_specs=pl.BlockSpec((None, tm, tn), lambda b_, i, j: (b_, i, j)),
        scratch_shapes=[pltpu.VMEM((tm, D_MODEL), BF16)],
        compiler_params=_cparams("parallel", "parallel", "arbitrary"),
        name="nm_glu",
    )(x, g, sc, sh, w, w, b, b)


def _nm_small_kernel(x_ref, g_ref, sc_ref, sh_ref, whi_ref, wlo_ref, o_ref):
    h = _norm_mod(x_ref[...], g_ref[...], sc_ref[...], sh_ref[...])
    h_hi, h_lo = _split_bf16(h)
    o_ref[...] = _dot(h_hi, whi_ref[...]) + _dot(h_hi, wlo_ref[...]) + _dot(h_lo, whi_ref[...])


def _nm_small(x, g, sc, sh, w, tm):
    B, L, _ = x.shape
    w_hi, w_lo = _split_bf16(w)
    return pl.pallas_call(
        _nm_small_kernel,
        out_shape=jax.ShapeDtypeStruct((B, L, LANES), F32),
        grid=(B, L // tm),
        in_specs=[pl.BlockSpec((None, tm, D_MODEL), lambda b, i: (b, i, 0)),
                  _vec_spec(D_MODEL), _mod_spec(sc), _mod_spec(sh),
                  pl.BlockSpec((D_MODEL, LANES), lambda b, i: (0, 0)),
                  pl.BlockSpec((D_MODEL, LANES), lambda b, i: (0, 0))],
        out_specs=pl.BlockSpec((None, tm, LANES), lambda b, i: (b, i, 0)),
        compiler_params=_cparams("parallel", "parallel"),
        name="nm_small",
    )(x, g, sc, sh, w_hi, w_lo)


def _linear_res_kernel(a_ref, w_ref, x_ref, gt_ref, o_ref):
    y = _dot(a_ref[...].astype(BF16), w_ref[...])
    o_ref[...] = x_ref[...] + gt_ref[...] * y


def _linear_res(a, w, x, gt, tm):
    B, L, K = a.shape
    return pl.pallas_call(
        _linear_res_kernel,
        out_shape=jax.ShapeDtypeStruct((B, L, D_MODEL), F32),
        grid=(B, L // tm),
        in_specs=[pl.BlockSpec((None, tm, K), lambda b, i: (b, i, 0)),
                  pl.BlockSpec((K, D_MODEL), lambda b, i: (0, 0)),
                  pl.BlockSpec((None, tm, D_MODEL), lambda b, i: (b, i, 0)),
                  _mod_spec(gt)],
        out_specs=pl.BlockSpec((None, tm, D_MODEL), lambda b, i: (b, i, 0)),
        compiler_params=_cparams("parallel", "parallel"),
        name="linear_res",
    )(a, w, x, gt)


def _conv_tail_kernel(u_ref, up_ref, un_ref, dww_ref, dwb_ref, lng_ref, lnb_ref, w2_ref, b2_ref,
                      x_ref, gt_ref, o_ref, buf_ref, cv_ref, *, tm, rc):
    i = pl.program_id(1)
    last = pl.num_programs(1) - 1
    buf_ref[0:CONV_HALO, :] = jnp.where(i > 0, up_ref[...], 0.0)
    buf_ref[CONV_HALO:CONV_HALO + tm, :] = u_ref[...]
    buf_ref[CONV_HALO + tm:, :] = jnp.where(i < last, un_ref[...], 0.0)
    off = CONV_HALO - CONV_WIDTH // 2
    for c in range(D_MODEL // LANES):
        cs = slice(c * LANES, (c + 1) * LANES)
        for r0 in range(0, tm, rc):
            acc = jnp.zeros((rc, LANES), F32)
            for k in range(CONV_WIDTH):
                acc = acc + dww_ref[k:k + 1, cs] * buf_ref[r0 + off + k:r0 + off + k + rc, cs]
            cv_ref[r0:r0 + rc, cs] = acc + dwb_ref[:, cs]
    v = cv_ref[...]
    mu = jnp.mean(v, axis=-1, keepdims=True)
    vc = v - mu
    var = jnp.mean(vc * vc, axis=-1, keepdims=True)
    y = _silu(vc * lax.rsqrt(var + LN_EPS) * lng_ref[...] + lnb_ref[...])
    z = _dot(y.astype(BF16), w2_ref[...]) + b2_ref[...]
    o_ref[...] = x_ref[...] + gt_ref[...] * z


def _conv_tail(u, dw_w, dw_b, ln_g, ln_b, w2, b2, x, gt, tm):
    B, L, _ = u.shape
    hb = tm // CONV_HALO
    nh = L // CONV_HALO
    dww = jnp.pad(dw_w, ((0, 32 - CONV_WIDTH), (0, 0)))
    kern = functools.partial(_conv_tail_kernel, tm=tm, rc=64)
    return pl.pallas_call(
        kern,
        out_shape=jax.ShapeDtypeStruct((B, L, D_MODEL), F32),
        grid=(B, L // tm),
        in_specs=[pl.BlockSpec((None, tm, D_MODEL), lambda b, i: (b, i, 0)),
                  pl.BlockSpec((None, CONV_HALO, D_MODEL),
                               lambda b, i: (b, jnp.maximum(i * hb - 1, 0), 0)),
                  pl.BlockSpec((None, CONV_HALO, D_MODEL),
                               lambda b, i: (b, jnp.minimum((i + 1) * hb, nh - 1), 0)),
                  pl.BlockSpec((32, D_MODEL), lambda b, i: (0, 0)),
                  _vec_spec(D_MODEL), _vec_spec(D_MODEL), _vec_spec(D_MODEL),
                  pl.BlockSpec((D_MODEL, D_MODEL), lambda b, i: (0, 0)),
                  _vec_spec(D_MODEL),
                  pl.BlockSpec((None, tm, D_MODEL), lambda b, i: (b, i, 0)),
                  _mod_spec(gt)],
        out_specs=pl.BlockSpec((None, tm, D_MODEL), lambda b, i: (b, i, 0)),
        scratch_shapes=[pltpu.VMEM((tm + 2 * CONV_HALO, D_MODEL), F32),
                        pltpu.VMEM((tm, D_MODEL), F32)],
        compiler_params=_cparams("parallel", "parallel"),
        name="conv_tail",
    )(u, u, u, dww, dw_b, ln_g, ln_b, w2, b2, x, gt)


def _gdn_prep_kernel(u_ref, up_ref, un_ref, w_ref, o_ref, buf_ref, *, tm):
    i = pl.program_id(1)
    j = pl.program_id(2)
    last = pl.num_programs(1) - 1
    buf_ref[0:GDN_HALO, :] = jnp.where(i > 0, up_ref[...], 0.0)
    buf_ref[GDN_HALO:GDN_HALO + tm, :] = u_ref[...]
    buf_ref[GDN_HALO + tm:, :] = jnp.where(i < last, un_ref[...], 0.0)
    off = GDN_HALO - GDN_CONV // 2
    for h in range(GDN_H):
        cs = slice(h * GDN_DK, (h + 1) * GDN_DK)
        acc = jnp.zeros((tm, GDN_DK), F32)
        for k in range(GDN_CONV):
            acc = acc + w_ref[k:k + 1, cs] * buf_ref[off + k:off + k + tm, cs]
        y = _silu(acc)
        nrm = y * lax.rsqrt(jnp.sum(y * y, axis=-1, keepdims=True) + 1e-6)
        o_ref[:, cs] = jnp.where(j == 0, nrm * (GDN_DK ** -0.5), jnp.where(j == 1, nrm, y))


def _gdn_prep(qkvz, conv_w, tm):
    B, L, _ = qkvz.shape
    W = GDN_H * GDN_DK
    hb = tm // GDN_HALO
    nh = L // GDN_HALO
    cw = jnp.pad(conv_w, ((0, 8 - GDN_CONV), (0, 0)))
    kern = functools.partial(_gdn_prep_kernel, tm=tm)
    return pl.pallas_call(
        kern,
        out_shape=jax.ShapeDtypeStruct((B, L, 3 * W), F32),
        grid=(B, L // tm, 3),
        in_specs=[pl.BlockSpec((None, tm, W), lambda b, i, j: (b, i, j)),
                  pl.BlockSpec((None, GDN_HALO, W), lambda b, i, j: (b, jnp.maximum(i * hb - 1, 0), j)),
                  pl.BlockSpec((None, GDN_HALO, W),
                               lambda b, i, j: (b, jnp.minimum((i + 1) * hb, nh - 1), j)),
                  pl.BlockSpec((8, W), lambda b, i, j: (0, j))],
        out_specs=pl.BlockSpec((None, tm, W), lambda b, i, j: (b, i, j)),
        scratch_shapes=[pltpu.VMEM((tm + 2 * GDN_HALO, W), F32)],
        compiler_params=_cparams("parallel", "parallel", "arbitrary"),
        name="gdn_prep",
    )(qkvz, qkvz, qkvz, cw)


def _gdn_chunk_kernel(q_ref, k_ref, v_ref, ba_ref, a_ref, dtb_ref, s0_ref, o_ref, sfin_ref, s_ref,
                      *, rev, col0):
    n = pl.program_id(1)
    C = GDN_CHUNK

    @pl.when(n == 0)
    def _():
        s_ref[...] = s0_ref[...]

    ri = lax.broadcasted_iota(jnp.int32, (C, C), 0)
    ci = lax.broadcasted_iota(jnp.int32, (C, C), 1)
    if rev:
        incl, strict = ri <= ci, ri < ci
    else:
        incl, strict = ri >= ci, ri > ci
    incl_t = ri <= ci if not rev else ri >= ci
    eye = ri == ci
    blk = (ri // GDN_SUB) == (ci // GDN_SUB)
    eye_f = jnp.where(eye, 1.0, 0.0)

    ba = ba_ref[...]
    beta_all = _sigmoid(ba)
    xa = ba + dtb_ref[...]
    softplus = jnp.maximum(xa, 0.0) + jnp.log(1.0 + jnp.exp(-jnp.abs(xa)))
    g_all = -a_ref[...] * softplus

    for h in range(GDN_H):
        cs = slice(h * GDN_DK, (h + 1) * GDN_DK)
        q = q_ref[:, cs]
        k = k_ref[:, cs]
        v = v_ref[:, cs]
        beta = beta_all[:, col0 + h:col0 + h + 1]
        gl = g_all[:, col0 + GDN_H + h:col0 + GDN_H + h + 1]
        gcb = jnp.broadcast_to(gl, (C, C))
        g_row = jnp.sum(jnp.where(eye, gcb, 0.0), axis=0, keepdims=True)
        cum_row = jnp.sum(jnp.where(incl_t, gcb, 0.0), axis=0, keepdims=True)
        cum_col = jnp.sum(jnp.where(incl, jnp.broadcast_to(g_row, (C, C)), 0.0),
                          axis=1, keepdims=True)
        g_tot = jnp.sum(gl, axis=0, keepdims=True)
        decay = jnp.where(incl, jnp.exp(jnp.where(incl, cum_col - cum_row, 0.0)), 0.0)
        e_cum = jnp.exp(cum_col)
        kb = k * beta
        kbf = k.astype(BF16)
        kk = _dot_nt(kb.astype(BF16), kbf)
        nm = jnp.where(strict, kk * decay, 0.0)
        nd = jnp.where(blk, nm, 0.0)
        no = nm - nd
        nd2 = _dot_f32(nd, nd)
        nd4 = _dot_f32(nd2, nd2)
        nd8 = _dot_f32(nd4, nd4)
        t = eye_f - nd
        t = t + _dot_f32(t, nd2)
        t = t + _dot_f32(t, nd4)
        t = t + _dot_f32(t, nd8)
        m = _dot_f32(t, no)
        m2 = _dot_f32(m, m)
        rhs = jnp.concatenate([v * beta, kb * e_cum], axis=-1)
        r = _dot_f32(t, rhs)
        r = r + _dot_f32(m2, r)
        sol = r - _dot_f32(m, r)
        u = sol[:, :GDN_DK]
        w = sol[:, GDN_DK:]
        qk = jnp.where(incl, _dot_nt(q.astype(BF16), kbf) * decay, 0.0)
        q_g = q * e_cum
        k_tail = k * jnp.exp(g_tot - cum_col)
        s = s_ref[h]
        sb = s.astype(BF16)
        v_new = u - _dot(w.astype(BF16), sb)
        vb = v_new.astype(BF16)
        o_ref[:, cs] = _dot(q_g.astype(BF16), sb) + _dot(qk.astype(BF16), vb)
        s_ref[h] = s * jnp.exp(g_tot) + _dot(k_tail.T.astype(BF16), vb)

    @pl.when(n == pl.num_programs(1) - 1)
    def _():
        sfin_ref[...] = s_ref[...]


def _gdn_chunk(qkv, ba, a_row, dtb_row, s0, rev):
    B, L, _ = qkv.shape
    W = GDN_H * GDN_DK
    nc = L // GDN_CHUNK
    cidx = (lambda n: nc - 1 - n) if rev else (lambda n: n)
    kern = functools.partial(_gdn_chunk_kernel, rev=rev, col0=2 * GDN_H if rev else 0)
    return pl.pallas_call(
        kern,
        out_shape=(jax.ShapeDtypeStruct((B, L, W), F32),
                   jax.ShapeDtypeStruct((B, GDN_H, GDN_DK, GDN_DK), F32)),
        grid=(B, nc),
        in_specs=[pl.BlockSpec((None, GDN_CHUNK, W), lambda b, n: (b, cidx(n), 0)),
                  pl.BlockSpec((None, GDN_CHUNK, W), lambda b, n: (b, cidx(n), 1)),
                  pl.BlockSpec((None, GDN_CHUNK, W), lambda b, n: (b, cidx(n), 2)),
                  pl.BlockSpec((None, GDN_CHUNK, LANES), lambda b, n: (b, cidx(n), 0)),
                  _vec_spec(LANES), _vec_spec(LANES),
                  pl.BlockSpec((None, GDN_H, GDN_DK, GDN_DK), lambda b, n: (b, 0, 0, 0))],
        out_specs=(pl.BlockSpec((None, GDN_CHUNK, W), lambda b, n: (b, cidx(n), 0)),
                   pl.BlockSpec((None, GDN_H, GDN_DK, GDN_DK), lambda b, n: (b, 0, 0, 0))),
        scratch_shapes=[pltpu.VMEM((GDN_H, GDN_DK, GDN_DK), F32)],
        compiler_params=_cparams("parallel", "arbitrary"),
        name="gdn_chunk_rev" if rev else "gdn_chunk_fwd",
    )(qkv, qkv, qkv, ba, a_row, dtb_row, s0)


def _gdn_out_kernel(of_ref, ob_ref, z_ref, og_ref, w_ref, x_ref, gt_ref, o_ref, y_ref):
    o = of_ref[...] + ob_ref[...]
    z = z_ref[...]
    for h in range(GDN_H):
        cs = slice(h * GDN_DK, (h + 1) * GDN_DK)
        oh = o[:, cs]
        ms = jnp.mean(oh * oh, axis=-1, keepdims=True)
        y = (oh * lax.rsqrt(ms + RMS_EPS)) * og_ref[...]
        y_ref[:, cs] = (y * _silu(z[:, cs])).astype(BF16)
    o_ref[...] = x_ref[...] + gt_ref[...] * _dot(y_ref[...], w_ref[...])


def _gdn_out(o_f, o_b, qkvz, o_g, w_o, x, gt, tm):
    B, L, W = o_f.shape
    return pl.pallas_call(
        _gdn_out_kernel,
        out_shape=jax.ShapeDtypeStruct((B, L, D_MODEL), F32),
        grid=(B, L // tm),
        in_specs=[pl.BlockSpec((None, tm, W), lambda b, i: (b, i, 0)),
                  pl.BlockSpec((None, tm, W), lambda b, i: (b, i, 0)),
                  pl.BlockSpec((None, tm, W), lambda b, i: (b, i, 3)),
                  _vec_spec(GDN_DK),
                  pl.BlockSpec((W, D_MODEL), lambda b, i: (0, 0)),
                  pl.BlockSpec((None, tm, D_MODEL), lambda b, i: (b, i, 0)),
                  _mod_spec(gt)],
        out_specs=pl.BlockSpec((None, tm, D_MODEL), lambda b, i: (b, i, 0)),
        scratch_shapes=[pltpu.VMEM((tm, W), BF16)],
        compiler_params=_cparams("parallel", "parallel"),
        name="gdn_out",
    )(o_f, o_b, qkvz, o_g, w_o, x, gt)


def _ctx_attn_kernel(q_ref, k_ref, v_ref, o_ref, ko_ref, vo_ref):
    scale = NA_DH ** -0.5
    for hh in range(2):
        cs = slice(hh * NA_DH, (hh + 1) * NA_DH)
        q = q_ref[:, cs]
        k = k_ref[:, cs]
        v = v_ref[:, cs]
        s = _dot_nt(q.astype(BF16), k.astype(BF16)) * scale
        p = jnp.exp(s - jnp.max(s, axis=-1, keepdims=True))
        p = p / jnp.sum(p, axis=-1, keepdims=True)
        o_ref[:, cs] = _dot(p.astype(BF16), v.astype(BF16))
        ko_ref[hh] = k
        vo_ref[hh] = v


def _ctx_attn(qkv):
    B, L, _ = qkv.shape
    hp = NA_HEADS // 2
    kv_shape = jax.ShapeDtypeStruct((B, NA_HEADS, L, NA_DH), F32)
    return pl.pallas_call(
        _ctx_attn_kernel,
        out_shape=(jax.ShapeDtypeStruct((B, L, D_MODEL), F32), kv_shape, kv_shape),
        grid=(B, hp),
        in_specs=[pl.BlockSpec((None, L, 2 * NA_DH), lambda b, p: (b, 0, p)),
                  pl.BlockSpec((None, L, 2 * NA_DH), lambda b, p: (b, 0, hp + p)),
                  pl.BlockSpec((None, L, 2 * NA_DH), lambda b, p: (b, 0, 2 * hp + p))],
        out_specs=(pl.BlockSpec((None, L, 2 * NA_DH), lambda b, p: (b, 0, p)),
                   pl.BlockSpec((None, 2, L, NA_DH), lambda b, p: (b, p, 0, 0)),
                   pl.BlockSpec((None, 2, L, NA_DH), lambda b, p: (b, p, 0, 0))),
        compiler_params=_cparams("parallel", "parallel"),
        name="ctx_attn",
    )(qkv, qkv, qkv)


def _na_attn_kernel(q_ref, k_ref, v_ref, kc_ref, vc_ref, bias_ref, o_ref, *, rows):
    scale = NA_DH ** -0.5
    nloc = NA_WIN_R * GRID_W

    def body(r, carry):
        rs = jnp.clip(r - NA_WIN_R // 2, 0, rows - NA_WIN_R)
        dr0 = rs - r + NA_WIN_R - 1
        q0 = pl.multiple_of(r * GRID_W, GRID_W)
        k0 = pl.multiple_of(rs * GRID_W, GRID_W)
        for hh in range(2):
            cs = slice(hh * NA_DH, (hh + 1) * NA_DH)
            q = q_ref[pl.ds(q0, GRID_W), cs].astype(BF16)
            kw = k_ref[pl.ds(k0, nloc), cs].astype(BF16)
            vw = v_ref[pl.ds(k0, nloc), cs].astype(BF16)
            s_loc = _dot_nt(q, kw) * scale + bias_ref[hh, dr0]
            s_ctx = _dot_nt(q, kc_ref[hh].astype(BF16)) * scale
            m = jnp.maximum(jnp.max(s_loc, axis=-1, keepdims=True),
                            jnp.max(s_ctx, axis=-1, keepdims=True))
            p_loc = jnp.exp(s_loc - m)
            p_ctx = jnp.exp(s_ctx - m)
            den = jnp.sum(p_loc, axis=-1, keepdims=True) + jnp.sum(p_ctx, axis=-1, keepdims=True)
            o = _dot(p_loc.astype(BF16), vw) + _dot(p_ctx.astype(BF16), vc_ref[hh].astype(BF16))
            o_ref[pl.ds(q0, GRID_W), cs] = o / den
        return carry

    lax.fori_loop(0, rows, body, 0)


def _na_attn(qkv, k_ctx, v_ctx, bias):
    B, L, _ = qkv.shape
    P = k_ctx.shape[2]
    rows = L // GRID_W
    hp = NA_HEADS // 2
    nloc = NA_WIN_R * GRID_W
    kern = functools.partial(_na_attn_kernel, rows=rows)
    return pl.pallas_call(
        kern,
        out_shape=jax.ShapeDtypeStruct((B, L, D_MODEL), F32),
        grid=(B, hp),
        in_specs=[pl.BlockSpec((None, L, 2 * NA_DH), lambda b, p: (b, 0, p)),
                  pl.BlockSpec((None, L, 2 * NA_DH), lambda b, p: (b, 0, hp + p)),
                  pl.BlockSpec((None, L, 2 * NA_DH), lambda b, p: (b, 0, 2 * hp + p)),
                  pl.BlockSpec((None, 2, P, NA_DH), lambda b, p: (b, p, 0, 0)),
                  pl.BlockSpec((None, 2, P, NA_DH), lambda b, p: (b, p, 0, 0)),
                  pl.BlockSpec((2, NA_WIN_R, GRID_W, nloc), lambda b, p: (p, 0, 0, 0))],
        out_specs=pl.BlockSpec((None, L, 2 * NA_DH), lambda b, p: (b, 0, p)),
        compiler_params=_cparams("parallel", "parallel"),
        name="na_attn",
    )(qkv, qkv, qkv, k_ctx, v_ctx, bias)


def _na_bias_table(rpb):
    qcol = np.arange(GRID_W)
    kcol = np.arange(GRID_W)
    cstart = np.clip(qcol - NA_WIN_C // 2, 0, GRID_W - NA_WIN_C)
    valid = (kcol[None, :] >= cstart[:, None]) & (kcol[None, :] < cstart[:, None] + NA_WIN_C)
    dc = np.clip(kcol[None, :] - qcol[:, None], -(NA_WIN_C - 1), NA_WIN_C - 1) + NA_WIN_C - 1
    t = jnp.where(valid[None, None], rpb[:, :, dc], NEG_BIG)
    var = jnp.stack([t[:, d:d + NA_WIN_R] for d in range(NA_WIN_R)], axis=1)
    var = var.transpose(0, 1, 3, 2, 4)
    return var.reshape(rpb.shape[0], NA_WIN_R, GRID_W, NA_WIN_R * GRID_W)


def _moe_router_kernel(x_ref, g_ref, sc_ref, sh_ref, whi_ref, wlo_ref, br_ref, o_ref):
    h = _norm_mod(x_ref[...], g_ref[...], sc_ref[...], sh_ref[...])
    h_hi, h_lo = _split_bf16(h)
    lg = (_dot(h_hi, whi_ref[...]) + _dot(h_hi, wlo_ref[...]) + _dot(h_lo, whi_ref[...])
          + br_ref[...])
    tm = lg.shape[0]
    gl = [lg[:, g:g + 1] for g in range(MOE_G)]
    gmax = functools.reduce(jnp.maximum, gl)
    gsum = functools.reduce(lambda a, b: a + b, [jnp.exp(v - gmax) for v in gl])
    g_w = 1.0 / gsum
    taken = jnp.zeros((tm, 1), jnp.bool_)
    g_hot = []
    for g in range(MOE_G):
        hit = jnp.logical_and(gl[g] == gmax, jnp.logical_not(taken))
        taken = jnp.logical_or(taken, hit)
        g_hot.append(jnp.where(hit, 1.0, 0.0))
    el = []
    for e in range(MOE_E):
        acc = jnp.zeros((tm, 1), F32)
        for g in range(MOE_G):
            c = MOE_G + g * MOE_E + e
            acc = acc + g_hot[g] * lg[:, c:c + 1]
        el.append(acc)
    emax = functools.reduce(jnp.maximum, el)
    ex = [jnp.exp(v - emax) for v in el]
    esum = functools.reduce(lambda a, b: a + b, ex)
    pe = [v / esum for v in ex]

    def first_max(vals):
        mx = functools.reduce(jnp.maximum, vals)
        tk = jnp.zeros((tm, 1), jnp.bool_)
        hot = []
        for v in vals:
            hit = jnp.logical_and(v == mx, jnp.logical_not(tk))
            tk = jnp.logical_or(tk, hit)
            hot.append(hit)
        return mx, hot

    m1, hot1 = first_max(pe)
    m2, hot2 = first_max([jnp.where(hot1[e], -1.0, pe[e]) for e in range(MOE_E)])
    den = m1 + m2
    w_e = [(jnp.where(hot1[e], m1, 0.0) + jnp.where(hot2[e], m2, 0.0)) / den * g_w
           for e in range(MOE_E)]
    lane = lax.broadcasted_iota(jnp.int32, (tm, LANES), 1)
    out = jnp.zeros((tm, LANES), F32)
    for g in range(MOE_G):
        for e in range(MOE_E):
            out = jnp.where(lane == g * MOE_E + e, g_hot[g] * w_e[e], out)
    o_ref[...] = out


def _moe_router(x, g, sc, sh, w_r, b_r, tm):
    B, L, _ = x.shape
    w_hi, w_lo = _split_bf16(w_r)
    return pl.pallas_call(
        _moe_router_kernel,
        out_shape=jax.ShapeDtypeStruct((B, L, LANES), F32),
        grid=(B, L // tm),
        in_specs=[pl.BlockSpec((None, tm, D_MODEL), lambda b, i: (b, i, 0)),
                  _vec_spec(D_MODEL), _mod_spec(sc), _mod_spec(sh),
                  pl.BlockSpec((D_MODEL, LANES), lambda b, i: (0, 0)),
                  pl.BlockSpec((D_MODEL, LANES), lambda b, i: (0, 0)),
                  _vec_spec(LANES)],
        out_specs=pl.BlockSpec((None, tm, LANES), lambda b, i: (b, i, 0)),
        compiler_params=_cparams("parallel", "parallel"),
        name="moe_router",
    )(x, g, sc, sh, w_hi, w_lo, b_r)


def _moe_experts_kernel(x_ref, g_ref, sc_ref, sh_ref, cw_ref, w1_ref, w3_ref, w2_ref, gt_ref,
                        o_ref, h_ref, acc_ref):
    e = pl.program_id(2)

    @pl.when(e == 0)
    def _():
        h_ref[...] = _norm_mod(x_ref[...], g_ref[...], sc_ref[...], sh_ref[...]).astype(BF16)
        acc_ref[...] = jnp.zeros_like(acc_ref)

    h = h_ref[...]
    a = _dot(h, w1_ref[...])
    b = _dot(h, w3_ref[...])
    cw = cw_ref[...]
    lane = lax.broadcasted_iota(jnp.int32, cw.shape, 1)
    c = jnp.sum(jnp.where(lane == e, cw, 0.0), axis=-1, keepdims=True)
    hid = _silu(a) * b * c
    acc_ref[...] += _dot(hid.astype(BF16), w2_ref[...])

    @pl.when(e == pl.num_programs(2) - 1)
    def _():
        o_ref[...] = x_ref[...] + gt_ref[...] * acc_ref[...]


def _moe_experts(x, g, sc, sh, cw, w1, w3, w2, gt, tm):
    B, L, _ = x.shape
    ne = w1.shape[0]
    return pl.pallas_call(
        _moe_experts_kernel,
        out_shape=jax.ShapeDtypeStruct((B, L, D_MODEL), F32),
        grid=(B, L // tm, ne),
        in_specs=[pl.BlockSpec((None, tm, D_MODEL), lambda b, i, e: (b, i, 0)),
                  _vec_spec(D_MODEL), _mod_spec(sc), _mod_spec(sh),
                  pl.BlockSpec((None, tm, LANES), lambda b, i, e: (b, i, 0)),
                  pl.BlockSpec((None, D_MODEL, MOE_F), lambda b, i, e: (e, 0, 0)),
                  pl.BlockSpec((None, D_MODEL, MOE_F), lambda b, i, e: (e, 0, 0)),
                  pl.BlockSpec((None, MOE_F, D_MODEL), lambda b, i, e: (e, 0, 0)),
                  _mod_spec(gt)],
        out_specs=pl.BlockSpec((None, tm, D_MODEL), lambda b, i, e: (b, i, 0)),
        scratch_shapes=[pltpu.VMEM((tm, D_MODEL), BF16), pltpu.VMEM((tm, D_MODEL), F32)],
        compiler_params=_cparams("parallel", "parallel", "arbitrary"),
        name="moe_experts",
    )(x, g, sc, sh, cw, w1, w3, w2, gt)


def _final_norm_kernel(x_ref, g_ref, o_ref):
    x = x_ref[...]
    ms = jnp.mean(x * x, axis=-1, keepdims=True)
    o_ref[...] = (x * lax.rsqrt(ms + RMS_EPS)) * g_ref[...]


def _final_norm(x, g, tm):
    B, L, _ = x.shape
    return pl.pallas_call(
        _final_norm_kernel,
        out_shape=jax.ShapeDtypeStruct((B, L, D_MODEL), F32),
        grid=(B, L // tm),
        in_specs=[pl.BlockSpec((None, tm, D_MODEL), lambda b, i: (b, i, 0)), _vec_spec(D_MODEL)],
        out_specs=pl.BlockSpec((None, tm, D_MODEL), lambda b, i: (b, i, 0)),
        compiler_params=_cparams("parallel", "parallel"),
        name="final_norm",
    )(x, g)


def _row(v):
    return v.reshape(1, -1)


def _prepare_weights(P):
    W = {}
    W["cv_pw1_w"] = P["cv_pw1_w"].astype(BF16)
    W["cv_pw2_w"] = P["cv_pw2_w"].astype(BF16)
    W["gdn_w_qkvz"] = P["gdn_w_qkvz"].astype(BF16)
    W["gdn_w_ba"] = jnp.pad(P["gdn_w_ba"], ((0, 0), (0, 0), (0, LANES - 4 * GDN_H)))
    a_neg = jnp.exp(P["gdn_a_log"].astype(F32))
    zeros = jnp.zeros_like(a_neg)
    a_cols = jnp.stack([zeros, a_neg], axis=2).reshape(a_neg.shape[0], 4 * GDN_H)
    dt_cols = jnp.stack([zeros, P["gdn_dt_bias"].astype(F32)], axis=2).reshape(a_neg.shape[0], 4 * GDN_H)
    W["gdn_a_row"] = jnp.pad(a_cols, ((0, 0), (0, LANES - 4 * GDN_H)))
    W["gdn_dt_row"] = jnp.pad(dt_cols, ((0, 0), (0, LANES - 4 * GDN_H)))
    W["gdn_w_o"] = P["gdn_w_o"].astype(BF16)
    W["na_w_qkv"] = P["na_w_qkv"].astype(BF16)
    W["na_w_o"] = P["na_w_o"].astype(BF16)
    W["na_bias"] = [_na_bias_table(P["na_rpb"][j]) for j in range(P["na_rpb"].shape[0])]
    ne = MOE_G * MOE_E
    W["moe_w1"] = P["moe_w1"].reshape(DEPTH, ne, D_MODEL, MOE_F).astype(BF16)
    W["moe_w3"] = P["moe_w3"].reshape(DEPTH, ne, D_MODEL, MOE_F).astype(BF16)
    W["moe_w2"] = P["moe_w2"].reshape(DEPTH, ne, MOE_F, D_MODEL).astype(BF16)
    w_r = jnp.concatenate([P["moe_wg"], P["moe_we"]], axis=-1)
    W["moe_w_r"] = jnp.pad(w_r, ((0, 0), (0, 0), (0, LANES - w_r.shape[-1])))
    b_r = jnp.concatenate([P["moe_bg"], P["moe_be"]], axis=-1)
    W["moe_b_r"] = jnp.pad(b_r, ((0, 0), (0, LANES - b_r.shape[-1])))
    return W


def _trunk(x, mods, P, W, cache):
    B, L, _ = x.shape
    latent = cache is not None
    Bm = mods.shape[1]
    flat = Bm == 1
    tm_seq = min(L, 256)
    tm_tok = 512
    gdn_states, na_ks, na_vs = [], [], []

    def tok(a):
        return a.reshape(1, B * L, a.shape[-1]) if flat else a

    def seq(a):
        return a.reshape(B, L, a.shape[-1])

    for i in range(DEPTH):
        m = mods[i].reshape(Bm, 1, ADA_CHUNKS, D_MODEL)
        sh1, sc1, gt1, sh2, sc2, gt2 = (m[:, :, c] for c in range(ADA_CHUNKS))
        g1 = _row(P["norm1_g"][i])
        kind, j = i % N_MIXERS, i // N_MIXERS
        if kind == 0:
            u = _nm_glu(tok(x), g1, sc1, sh1, W["cv_pw1_w"][j], _row(P["cv_pw1_b"][j]), tm_tok, 512)
            x = _conv_tail(seq(u), P["cv_dw_w"][j], _row(P["cv_dw_b"][j]), _row(P["cv_ln_g"][j]),
                           _row(P["cv_ln_b"][j]), W["cv_pw2_w"][j], _row(P["cv_pw2_b"][j]),
                           seq(x), gt1, tm_seq)
        elif kind == 1:
            qkvz = _nm_linear(tok(x), g1, sc1, sh1, W["gdn_w_qkvz"][j], tm_tok, 512)
            ba = _nm_small(tok(x), g1, sc1, sh1, W["gdn_w_ba"][j], tm_tok)
            qkv = _gdn_prep(seq(qkvz), P["gdn_conv_w"][j], tm_seq)
            if latent:
                s0 = cache[0][:, j].astype(F32)
            else:
                s0 = jnp.zeros((B, 2, GDN_H, GDN_DK, GDN_DK), F32)
            a_row, dt_row = W["gdn_a_row"][j:j + 1], W["gdn_dt_row"][j:j + 1]
            o_f, s_f = _gdn_chunk(qkv, seq(ba), a_row, dt_row, s0[:, 0], rev=False)
            o_b, s_b = _gdn_chunk(qkv, seq(ba), a_row, dt_row, s0[:, 1], rev=True)
            gdn_states.append(jnp.stack([s_f, s_b], axis=1))
            x = _gdn_out(tok(o_f), tok(o_b), qkvz, _row(P["gdn_o_g"][j]), W["gdn_w_o"][j],
                         tok(x), gt1, tm_tok)
        else:
            qkv = _nm_linear(tok(x), g1, sc1, sh1, W["na_w_qkv"][j], tm_tok, 512)
            if latent:
                o = _na_attn(seq(qkv), cache[1][:, j], cache[2][:, j], W["na_bias"][j])
            else:
                o, k_ctx, v_ctx = _ctx_attn(seq(qkv))
                na_ks.append(k_ctx)
                na_vs.append(v_ctx)
            x = _linear_res(tok(o), W["na_w_o"][j], tok(x), gt1, tm_tok)
        cw = _moe_router(tok(x), _row(P["norm2_g"][i]), sc2, sh2, W["moe_w_r"][i],
                         W["moe_b_r"][i:i + 1], tm_tok)
        x = _moe_experts(tok(x), _row(P["norm2_g"][i]), sc2, sh2, cw, W["moe_w1"][i], W["moe_w3"][i],
                         W["moe_w2"][i], gt2, tm_tok)
        x = seq(x)
    y = seq(_final_norm(tok(x), _row(P["final_norm_g"]), tm_tok))
    return y, gdn_states, na_ks, na_vs


def kernel(x_prompt, x_sample, state_gdn, cache_na_k, cache_na_v, c, c_ctx,
           ada_w, ada_b, norm1_g, norm2_g,
           cv_pw1_w, cv_pw1_b, cv_dw_w, cv_dw_b, cv_ln_g, cv_ln_b, cv_pw2_w, cv_pw2_b,
           gdn_w_qkvz, gdn_conv_w, gdn_w_ba, gdn_a_log, gdn_dt_bias, gdn_o_g, gdn_w_o,
           na_w_qkv, na_rpb, na_w_o,
           moe_wg, moe_bg, moe_we, moe_be, moe_w1, moe_w3, moe_w2,
           final_norm_g):
    P = dict(norm1_g=norm1_g, norm2_g=norm2_g,
             cv_pw1_w=cv_pw1_w, cv_pw1_b=cv_pw1_b, cv_dw_w=cv_dw_w, cv_dw_b=cv_dw_b,
             cv_ln_g=cv_ln_g, cv_ln_b=cv_ln_b, cv_pw2_w=cv_pw2_w, cv_pw2_b=cv_pw2_b,
             gdn_w_qkvz=gdn_w_qkvz, gdn_conv_w=gdn_conv_w, gdn_w_ba=gdn_w_ba, gdn_a_log=gdn_a_log,
             gdn_dt_bias=gdn_dt_bias, gdn_o_g=gdn_o_g, gdn_w_o=gdn_w_o,
             na_w_qkv=na_w_qkv, na_rpb=na_rpb, na_w_o=na_w_o,
             moe_wg=moe_wg, moe_bg=moe_bg, moe_we=moe_we, moe_be=moe_be,
             moe_w1=moe_w1, moe_w3=moe_w3, moe_w2=moe_w2, final_norm_g=final_norm_g)
    W = _prepare_weights(P)
    nb = c.shape[0]
    rows = 8 * ((nb + 1 + 7) // 8)
    cond = jnp.concatenate([c, c_ctx[None, :], jnp.zeros((rows - nb - 1, D_MODEL), F32)], axis=0)
    mods = _ada_all(cond, ada_w, ada_b)
    y_prompt, gdn_states, na_ks, na_vs = _trunk(x_prompt, mods[:, nb:nb + 1], P, W, None)
    y_sample, _, _, _ = _trunk(x_sample, mods[:, :nb], P, W, (state_gdn, cache_na_k, cache_na_v))
    return (y_prompt, y_sample, jnp.stack(gdn_states, axis=1),
            jnp.stack(na_ks, axis=1), jnp.stack(na_vs, axis=1))
```

```python
import functools

import numpy as np
import jax
import jax.numpy as jnp
from jax import lax
from jax.experimental import pallas as pl
from jax.experimental.pallas import tpu as pltpu

F32 = jnp.float32
BF16 = jnp.bfloat16

D_MODEL = 1024
DEPTH = 4
N_MIXERS = 3
ADA_CHUNKS = 6
RMS_EPS = 1e-6
LN_EPS = 1e-5
CONV_WIDTH = 31
CONV_HALO = 16
GDN_H = 8
GDN_DK = 128
GDN_CONV = 5
GDN_HALO = 8
GDN_BLOCK = 128
GDN_SUB = 16
NA_HEADS = 16
NA_DH = 64
NA_WIN_R = 8
NA_WIN_C = 16
NA_QROWS = 4
NA_KROWS = 12
GRID_W = 64
MOE_G = 4
MOE_E = 4
MOE_F = 256
LANES = 128
NEG_BIG = -1e30

VMEM_LIMIT_BYTES = 48 * 1024 * 1024


def _cparams(*sem):
    return pltpu.CompilerParams(dimension_semantics=sem, vmem_limit_bytes=VMEM_LIMIT_BYTES)


def _sigmoid(x):
    return 1.0 / (1.0 + jnp.exp(-x))


def _silu(x):
    return x * _sigmoid(x)


def _norm_mod(x, g, sc, sh):
    ms = jnp.mean(x * x, axis=-1, keepdims=True)
    y = x * lax.rsqrt(ms + RMS_EPS)
    return (y * g) * (1.0 + sc) + sh


def _dot(a, b):
    return jnp.dot(a, b, preferred_element_type=F32)


def _dot_nt(a, b):
    return lax.dot_general(a, b, (((1,), (1,)), ((), ())), preferred_element_type=F32)


def _dot_f32(a, b):
    return jnp.dot(a, b, preferred_element_type=F32, precision=lax.Precision.HIGHEST)


def _split_bf16(w):
    hi = w.astype(BF16)
    lo = (w - hi.astype(F32)).astype(BF16)
    return hi, lo


def _mod_spec(mod):
    if mod.shape[0] == 1:
        return pl.BlockSpec((None, 1, D_MODEL), lambda b, *_: (0, 0, 0))
    return pl.BlockSpec((None, 1, D_MODEL), lambda b, *_: (b, 0, 0))


def _vec_spec(n):
    return pl.BlockSpec((1, n), lambda *_: (0, 0))


def _ada_kernel(c_ref, w_ref, b_ref, o_ref):
    s = _silu(c_ref[...]).astype(BF16)
    o_ref[...] = _dot(s, w_ref[...].astype(BF16)) + b_ref[...]


def _ada_all(cond, ada_w, ada_b, tn=1024):
    R = cond.shape[0]
    N = ada_w.shape[-1]
    return pl.pallas_call(
        _ada_kernel,
        out_shape=jax.ShapeDtypeStruct((DEPTH, R, N), F32),
        grid=(DEPTH, N // tn),
        in_specs=[pl.BlockSpec((R, D_MODEL), lambda l, j: (0, 0)),
                  pl.BlockSpec((None, D_MODEL, tn), lambda l, j: (l, 0, j)),
                  pl.BlockSpec((None, 1, tn), lambda l, j: (l, 0, j))],
        out_specs=pl.BlockSpec((None, R, tn), lambda l, j: (l, 0, j)),
        compiler_params=_cparams("parallel", "parallel"),
        name="ada",
    )(cond, ada_w, ada_b.reshape(DEPTH, 1, N))


def _nm_linear_kernel(x_ref, g_ref, sc_ref, sh_ref, w_ref, o_ref, h_ref):
    @pl.when(pl.program_id(2) == 0)
    def _():
        h_ref[...] = _norm_mod(x_ref[...], g_ref[...], sc_ref[...], sh_ref[...]).astype(BF16)

    o_ref[...] = _dot(h_ref[...], w_ref[...]).astype(o_ref.dtype)


def _nm_linear(x, g, sc, sh, w, tm, tn):
    B, L, _ = x.shape
    N = w.shape[1]
    return pl.pallas_call(
        _nm_linear_kernel,
        out_shape=jax.ShapeDtypeStruct((B, L, N), F32),
        grid=(B, L // tm, N // tn),
        in_specs=[pl.BlockSpec((None, tm, D_MODEL), lambda b, i, j: (b, i, 0)),
                  _vec_spec(D_MODEL), _mod_spec(sc), _mod_spec(sh),
                  pl.BlockSpec((D_MODEL, tn), lambda b, i, j: (0, j))],
        out_specs=pl.BlockSpec((None, tm, tn), lambda b, i, j: (b, i, j)),
        scratch_shapes=[pltpu.VMEM((tm, D_MODEL), BF16)],
        compiler_params=_cparams("parallel", "parallel", "arbitrary"),
        name="nm_linear",
    )(x, g, sc, sh, w)


def _nm_glu_kernel(x_ref, g_ref, sc_ref, sh_ref, wa_ref, wg_ref, ba_ref, bg_ref, o_ref, h_ref):
    @pl.when(pl.program_id(2) == 0)
    def _():
        h_ref[...] = _norm_mod(x_ref[...], g_ref[...], sc_ref[...], sh_ref[...]).astype(BF16)

    h = h_ref[...]
    a = _dot(h, wa_ref[...]) + ba_ref[...]
    gate = _dot(h, wg_ref[...]) + bg_ref[...]
    o_ref[...] = a * _sigmoid(gate)


def _nm_glu(x, g, sc, sh, w, b, tm, tn):
    B, L, _ = x.shape
    N = w.shape[1] // 2
    nj = N // tn
    return pl.pallas_call(
        _nm_glu_kernel,
        out_shape=jax.ShapeDtypeStruct((B, L, N), F32),
        grid=(B, L // tm, nj),
        in_specs=[pl.BlockSpec((None, tm, D_MODEL), lambda b_, i, j: (b_, i, 0)),
                  _vec_spec(D_MODEL), _mod_spec(sc), _mod_spec(sh),
                  pl.BlockSpec((D_MODEL, tn), lambda b_, i, j: (0, j)),
                  pl.BlockSpec((D_MODEL, tn), lambda b_, i, j: (0, j + nj)),
                  pl.BlockSpec((1, tn), lambda b_, i, j: (0, j)),
                  pl.BlockSpec((1, tn), lambda b_, i, j: (0, j + nj))],
        out_specs=pl.BlockSpec((None, tm, tn), lambda b_, i, j: (b_, i, j)),
        scratch_shapes=[pltpu.VMEM((tm, D_MODEL), BF16)],
        compiler_params=_cparams("parallel", "parallel", "arbitrary"),
        name="nm_glu",
    )(x, g, sc, sh, w, w, b, b)


def _nm_small_kernel(x_ref, g_ref, sc_ref, sh_ref, whi_ref, wlo_ref, o_ref):
    h = _norm_mod(x_ref[...], g_ref[...], sc_ref[...], sh_ref[...])
    h_hi, h_lo = _split_bf16(h)
    o_ref[...] = _dot(h_hi, whi_ref[...]) + _dot(h_hi, wlo_ref[...]) + _dot(h_lo, whi_ref[...])


def _nm_small(x, g, sc, sh, w, tm):
    B, L, _ = x.shape
    w_hi, w_lo = _split_bf16(w)
    return pl.pallas_call(
        _nm_small_kernel,
        out_shape=jax.ShapeDtypeStruct((B, L, LANES), F32),
        grid=(B, L // tm),
        in_specs=[pl.BlockSpec((None, tm, D_MODEL), lambda b, i: (b, i, 0)),
                  _vec_spec(D_MODEL), _mod_spec(sc), _mod_spec(sh),
                  pl.BlockSpec((D_MODEL, LANES), lambda b, i: (0, 0)),
                  pl.BlockSpec((D_MODEL, LANES), lambda b, i: (0, 0))],
        out_specs=pl.BlockSpec((None, tm, LANES), lambda b, i: (b, i, 0)),
        compiler_params=_cparams("parallel", "parallel"),
        name="nm_small",
    )(x, g, sc, sh, w_hi, w_lo)


def _linear_res_kernel(a_ref, w_ref, x_ref, gt_ref, o_ref):
    y = _dot(a_ref[...].astype(BF16), w_ref[...])
    o_ref[...] = x_ref[...] + gt_ref[...] * y


def _linear_res(a, w, x, gt, tm):
    B, L, K = a.shape
    return pl.pallas_call(
        _linear_res_kernel,
        out_shape=jax.ShapeDtypeStruct((B, L, D_MODEL), F32),
        grid=(B, L // tm),
        in_specs=[pl.BlockSpec((None, tm, K), lambda b, i: (b, i, 0)),
                  pl.BlockSpec((K, D_MODEL), lambda b, i: (0, 0)),
                  pl.BlockSpec((None, tm, D_MODEL), lambda b, i: (b, i, 0)),
                  _mod_spec(gt)],
        out_specs=pl.BlockSpec((None, tm, D_MODEL), lambda b, i: (b, i, 0)),
        compiler_params=_cparams("parallel", "parallel"),
        name="linear_res",
    )(a, w, x, gt)


def _conv_tail_kernel(u_ref, up_ref, un_ref, dww_ref, dwb_ref, lng_ref, lnb_ref, w2_ref, b2_ref,
                      x_ref, gt_ref, o_ref, buf_ref, cv_ref, *, tm, rc):
    i = pl.program_id(1)
    last = pl.num_programs(1) - 1
    buf_ref[0:CONV_HALO, :] = jnp.where(i > 0, up_ref[...], 0.0)
    buf_ref[CONV_HALO:CONV_HALO + tm, :] = u_ref[...]
    buf_ref[CONV_HALO + tm:, :] = jnp.where(i < last, un_ref[...], 0.0)
    off = CONV_HALO - CONV_WIDTH // 2
    for c in range(D_MODEL // LANES):
        cs = slice(c * LANES, (c + 1) * LANES)
        for r0 in range(0, tm, rc):
            acc = jnp.zeros((rc, LANES), F32)
            for k in range(CONV_WIDTH):
                acc = acc + dww_ref[k:k + 1, cs] * buf_ref[r0 + off + k:r0 + off + k + rc, cs]
            cv_ref[r0:r0 + rc, cs] = acc + dwb_ref[:, cs]
    v = cv_ref[...]
    mu = jnp.mean(v, axis=-1, keepdims=True)
    vc = v - mu
    var = jnp.mean(vc * vc, axis=-1, keepdims=True)
    y = _silu(vc * lax.rsqrt(var + LN_EPS) * lng_ref[...] + lnb_ref[...])
    z = _dot(y.astype(BF16), w2_ref[...]) + b2_ref[...]
    o_ref[...] = x_ref[...] + gt_ref[...] * z


def _conv_tail(u, dw_w, dw_b, ln_g, ln_b, w2, b2, x, gt, tm):
    B, L, _ = u.shape
    hb = tm // CONV_HALO
    nh = L // CONV_HALO
    dww = jnp.pad(dw_w, ((0, 32 - CONV_WIDTH), (0, 0)))
    kern = functools.partial(_conv_tail_kernel, tm=tm, rc=64)
    return pl.pallas_call(
        kern,
        out_shape=jax.ShapeDtypeStruct((B, L, D_MODEL), F32),
        grid=(B, L // tm),
        in_specs=[pl.BlockSpec((None, tm, D_MODEL), lambda b, i: (b, i, 0)),
                  pl.BlockSpec((None, CONV_HALO, D_MODEL),
                               lambda b, i: (b, jnp.maximum(i * hb - 1, 0), 0)),
                  pl.BlockSpec((None, CONV_HALO, D_MODEL),
                               lambda b, i: (b, jnp.minimum((i + 1) * hb, nh - 1), 0)),
                  pl.BlockSpec((32, D_MODEL), lambda b, i: (0, 0)),
                  _vec_spec(D_MODEL), _vec_spec(D_MODEL), _vec_spec(D_MODEL),
                  pl.BlockSpec((D_MODEL, D_MODEL), lambda b, i: (0, 0)),
                  _vec_spec(D_MODEL),
                  pl.BlockSpec((None, tm, D_MODEL), lambda b, i: (b, i, 0)),
                  _mod_spec(gt)],
        out_specs=pl.BlockSpec((None, tm, D_MODEL), lambda b, i: (b, i, 0)),
        scratch_shapes=[pltpu.VMEM((tm + 2 * CONV_HALO, D_MODEL), F32),
                        pltpu.VMEM((tm, D_MODEL), F32)],
        compiler_params=_cparams("parallel", "parallel"),
        name="conv_tail",
    )(u, u, u, dww, dw_b, ln_g, ln_b, w2, b2, x, gt)


def _gdn_prep_kernel(u_ref, up_ref, un_ref, w_ref, o_ref, buf_ref, *, tm):
    i = pl.program_id(1)
    j = pl.program_id(2)
    last = pl.num_programs(1) - 1
    buf_ref[0:GDN_HALO, :] = jnp.where(i > 0, up_ref[...], 0.0)
    buf_ref[GDN_HALO:GDN_HALO + tm, :] = u_ref[...]
    buf_ref[GDN_HALO + tm:, :] = jnp.where(i < last, un_ref[...], 0.0)
    off = GDN_HALO - GDN_CONV // 2
    for h in range(GDN_H):
        cs = slice(h * GDN_DK, (h + 1) * GDN_DK)
        acc = jnp.zeros((tm, GDN_DK), F32)
        for k in range(GDN_CONV):
            acc = acc + w_ref[k:k + 1, cs] * buf_ref[off + k:off + k + tm, cs]
        y = _silu(acc)
        nrm = y * lax.rsqrt(jnp.sum(y * y, axis=-1, keepdims=True) + 1e-6)
        o_ref[:, cs] = jnp.where(j == 0, nrm * (GDN_DK ** -0.5), jnp.where(j == 1, nrm, y))


def _gdn_prep(qkvz, conv_w, tm):
    B, L, _ = qkvz.shape
    W = GDN_H * GDN_DK
    hb = tm // GDN_HALO
    nh = L // GDN_HALO
    cw = jnp.pad(conv_w, ((0, 8 - GDN_CONV), (0, 0)))
    kern = functools.partial(_gdn_prep_kernel, tm=tm)
    return pl.pallas_call(
        kern,
        out_shape=jax.ShapeDtypeStruct((B, L, 3 * W), F32),
        grid=(B, L // tm, 3),
        in_specs=[pl.BlockSpec((None, tm, W), lambda b, i, j: (b, i, j)),
                  pl.BlockSpec((None, GDN_HALO, W), lambda b, i, j: (b, jnp.maximum(i * hb - 1, 0), j)),
                  pl.BlockSpec((None, GDN_HALO, W),
                               lambda b, i, j: (b, jnp.minimum((i + 1) * hb, nh - 1), j)),
                  pl.BlockSpec((8, W), lambda b, i, j: (0, j))],
        out_specs=pl.BlockSpec((None, tm, W), lambda b, i, j: (b, i, j)),
        scratch_shapes=[pltpu.VMEM((tm + 2 * GDN_HALO, W), F32)],
        compiler_params=_cparams("parallel", "parallel", "arbitrary"),
        name="gdn_prep",
    )(qkvz, qkvz, qkvz, cw)


def _split3(x):
    hi = x.astype(BF16)
    r1 = x - hi.astype(F32)
    mid = r1.astype(BF16)
    lo = (r1 - mid.astype(F32)).astype(BF16)
    return hi, mid, lo


def _mm3(a, b):
    return _dot(a[0], b[0]) + _dot(a[0], b[1]) + _dot(a[1], b[0])


def _gdn_chunk_kernel(q_ref, k_ref, v_ref, ba_ref, a_ref, dtb_ref, s0_ref, o_ref, sfin_ref, s_ref,
                      *, rev, col0):
    n = pl.program_id(1)
    C = GDN_BLOCK

    @pl.when(n == 0)
    def _():
        s_ref[...] = s0_ref[...]

    ri = lax.broadcasted_iota(jnp.int32, (C, C), 0)
    ci = lax.broadcasted_iota(jnp.int32, (C, C), 1)
    if rev:
        incl, strict = ri <= ci, ri < ci
    else:
        incl, strict = ri >= ci, ri > ci
    blk = (ri // GDN_SUB) == (ci // GDN_SUB)
    eye_f = jnp.where(ri == ci, 1.0, 0.0)

    ba = ba_ref[...]
    beta_all = _sigmoid(ba)
    xa = ba + dtb_ref[...]
    softplus = jnp.maximum(xa, 0.0) + jnp.log(1.0 + jnp.exp(-jnp.abs(xa)))
    g_all = -a_ref[...] * softplus
    tri = jnp.where(incl, 1.0, 0.0).astype(BF16)
    g_hi, g_mid, g_lo = _split3(g_all)
    cum_all = _dot(tri, g_hi) + _dot(tri, g_mid) + _dot(tri, g_lo)
    cum_t = cum_all.T
    tot_row = cum_all[0:1, :] if rev else cum_all[C - 1:C, :]
    ecum_all = jnp.exp(cum_all)
    etail_all = jnp.exp(tot_row - cum_all)
    etot_row = jnp.exp(tot_row)

    HS = range(GDN_H)
    cs = [slice(h * GDN_DK, (h + 1) * GDN_DK) for h in HS]
    cg = [col0 + GDN_H + h for h in HS]
    k = [k_ref[:, cs[h]] for h in HS]
    kbf = [k[h].astype(BF16) for h in HS]
    beta = [beta_all[:, col0 + h:col0 + h + 1] for h in HS]
    kb = [k[h] * beta[h] for h in HS]
    decay = [jnp.where(incl, jnp.exp(jnp.where(
        incl, cum_all[:, cg[h]:cg[h] + 1] - cum_t[cg[h]:cg[h] + 1, :], 0.0)), 0.0) for h in HS]
    nm = [jnp.where(strict, _dot_nt(kb[h].astype(BF16), kbf[h]) * decay[h], 0.0) for h in HS]
    nd = [jnp.where(blk, nm[h], 0.0) for h in HS]
    nd_s = [_split_bf16(nd[h]) for h in HS]
    nd2_s = [_split_bf16(_mm3(nd_s[h], nd_s[h])) for h in HS]
    nd4_s = [_split_bf16(_mm3(nd2_s[h], nd2_s[h])) for h in HS]
    nd8_s = [_split_bf16(_mm3(nd4_s[h], nd4_s[h])) for h in HS]
    t = [eye_f - nd[h] for h in HS]
    t = [t[h] + _mm3(_split_bf16(t[h]), nd2_s[h]) for h in HS]
    t = [t[h] + _mm3(_split_bf16(t[h]), nd4_s[h]) for h in HS]
    t = [t[h] + _mm3(_split_bf16(t[h]), nd8_s[h]) for h in HS]
    t_s = [_split_bf16(t[h]) for h in HS]
    m_s = [_split_bf16(_mm3(t_s[h], _split_bf16(nm[h] - nd[h]))) for h in HS]
    m2_s = [_split_bf16(_mm3(m_s[h], m_s[h])) for h in HS]
    m4_s = [_split_bf16(_mm3(m2_s[h], m2_s[h])) for h in HS]
    a_inv = [t[h] + _mm3(m4_s[h], t_s[h]) for h in HS]
    a_inv = [a_inv[h] + _mm3(m2_s[h], _split_bf16(a_inv[h])) for h in HS]
    a_inv = [a_inv[h] - _mm3(m_s[h], _split_bf16(a_inv[h])) for h in HS]
    e_cum = [ecum_all[:, cg[h]:cg[h] + 1] for h in HS]
    rhs = [jnp.concatenate([v_ref[:, cs[h]] * beta[h], kb[h] * e_cum[h]], axis=-1) for h in HS]
    sol = [_mm3(_split_bf16(a_inv[h]), _split_bf16(rhs[h])) for h in HS]
    q = [q_ref[:, cs[h]] for h in HS]
    qk = [jnp.where(incl, _dot_nt(q[h].astype(BF16), kbf[h]) * decay[h], 0.0).astype(BF16)
          for h in HS]
    q_g = [(q[h] * e_cum[h]).astype(BF16) for h in HS]
    k_tail_t = [(k[h] * etail_all[:, cg[h]:cg[h] + 1]).T.astype(BF16) for h in HS]
    s = [s_ref[h] for h in HS]
    sb = [s[h].astype(BF16) for h in HS]
    vb = [(sol[h][:, :GDN_DK] - _dot(sol[h][:, GDN_DK:].astype(BF16), sb[h])).astype(BF16)
          for h in HS]
    for h in HS:
        o_ref[:, cs[h]] = _dot(q_g[h], sb[h]) + _dot(qk[h], vb[h])
    for h in HS:
        s_ref[h] = s[h] * etot_row[:, cg[h]:cg[h] + 1] + _dot(k_tail_t[h], vb[h])

    @pl.when(n == pl.num_programs(1) - 1)
    def _():
        sfin_ref[...] = s_ref[...]


def _gdn_chunk(qkv, ba, a_row, dtb_row, s0, rev):
    B, L, _ = qkv.shape
    W = GDN_H * GDN_DK
    nc = L // GDN_BLOCK
    cidx = (lambda n: nc - 1 - n) if rev else (lambda n: n)
    kern = functools.partial(_gdn_chunk_kernel, rev=rev, col0=2 * GDN_H if rev else 0)
    return pl.pallas_call(
        kern,
        out_shape=(jax.ShapeDtypeStruct((B, L, W), F32),
                   jax.ShapeDtypeStruct((B, GDN_H, GDN_DK, GDN_DK), F32)),
        grid=(B, nc),
        in_specs=[pl.BlockSpec((None, GDN_BLOCK, W), lambda b, n: (b, cidx(n), 0)),
                  pl.BlockSpec((None, GDN_BLOCK, W), lambda b, n: (b, cidx(n), 1)),
                  pl.BlockSpec((None, GDN_BLOCK, W), lambda b, n: (b, cidx(n), 2)),
                  pl.BlockSpec((None, GDN_BLOCK, LANES), lambda b, n: (b, cidx(n), 0)),
                  _vec_spec(LANES), _vec_spec(LANES),
                  pl.BlockSpec((None, GDN_H, GDN_DK, GDN_DK), lambda b, n: (b, 0, 0, 0))],
        out_specs=(pl.BlockSpec((None, GDN_BLOCK, W), lambda b, n: (b, cidx(n), 0)),
                   pl.BlockSpec((None, GDN_H, GDN_DK, GDN_DK), lambda b, n: (b, 0, 0, 0))),
        scratch_shapes=[pltpu.VMEM((GDN_H, GDN_DK, GDN_DK), F32)],
        compiler_params=_cparams("parallel", "arbitrary"),
        name="gdn_chunk_rev" if rev else "gdn_chunk_fwd",
    )(qkv, qkv, qkv, ba, a_row, dtb_row, s0)


def _gdn_out_kernel(of_ref, ob_ref, z_ref, og_ref, w_ref, x_ref, gt_ref, o_ref, y_ref):
    o = of_ref[...] + ob_ref[...]
    z = z_ref[...]
    for h in range(GDN_H):
        cs = slice(h * GDN_DK, (h + 1) * GDN_DK)
        oh = o[:, cs]
        ms = jnp.mean(oh * oh, axis=-1, keepdims=True)
        y = (oh * lax.rsqrt(ms + RMS_EPS)) * og_ref[...]
        y_ref[:, cs] = (y * _silu(z[:, cs])).astype(BF16)
    o_ref[...] = x_ref[...] + gt_ref[...] * _dot(y_ref[...], w_ref[...])


def _gdn_out(o_f, o_b, qkvz, o_g, w_o, x, gt, tm):
    B, L, W = o_f.shape
    return pl.pallas_call(
        _gdn_out_kernel,
        out_shape=jax.ShapeDtypeStruct((B, L, D_MODEL), F32),
        grid=(B, L // tm),
        in_specs=[pl.BlockSpec((None, tm, W), lambda b, i: (b, i, 0)),
                  pl.BlockSpec((None, tm, W), lambda b, i: (b, i, 0)),
                  pl.BlockSpec((None, tm, W), lambda b, i: (b, i, 3)),
                  _vec_spec(GDN_DK),
                  pl.BlockSpec((W, D_MODEL), lambda b, i: (0, 0)),
                  pl.BlockSpec((None, tm, D_MODEL), lambda b, i: (b, i, 0)),
                  _mod_spec(gt)],
        out_specs=pl.BlockSpec((None, tm, D_MODEL), lambda b, i: (b, i, 0)),
        scratch_shapes=[pltpu.VMEM((tm, W), BF16)],
        compiler_params=_cparams("parallel", "parallel"),
        name="gdn_out",
    )(o_f, o_b, qkvz, o_g, w_o, x, gt)


def _ctx_attn_kernel(q_ref, k_ref, v_ref, o_ref, ko_ref, vo_ref):
    scale = NA_DH ** -0.5
    for hh in range(2):
        cs = slice(hh * NA_DH, (hh + 1) * NA_DH)
        q = q_ref[:, cs]
        k = k_ref[:, cs]
        v = v_ref[:, cs]
        s = _dot_nt(q.astype(BF16), k.astype(BF16)) * scale
        p = jnp.exp(s - jnp.max(s, axis=-1, keepdims=True))
        p = p / jnp.sum(p, axis=-1, keepdims=True)
        o_ref[:, cs] = _dot(p.astype(BF16), v.astype(BF16))
        ko_ref[hh] = k
        vo_ref[hh] = v


def _ctx_attn(qkv):
    B, L, _ = qkv.shape
    hp = NA_HEADS // 2
    kv_shape = jax.ShapeDtypeStruct((B, NA_HEADS, L, NA_DH), F32)
    return pl.pallas_call(
        _ctx_attn_kernel,
        out_shape=(jax.ShapeDtypeStruct((B, L, D_MODEL), F32), kv_shape, kv_shape),
        grid=(B, hp),
        in_specs=[pl.BlockSpec((None, L, 2 * NA_DH), lambda b, p: (b, 0, p)),
                  pl.BlockSpec((None, L, 2 * NA_DH), lambda b, p: (b, 0, hp + p)),
                  pl.BlockSpec((None, L, 2 * NA_DH), lambda b, p: (b, 0, 2 * hp + p))],
        out_specs=(pl.BlockSpec((None, L, 2 * NA_DH), lambda b, p: (b, 0, p)),
                   pl.BlockSpec((None, 2, L, NA_DH), lambda b, p: (b, p, 0, 0)),
                   pl.BlockSpec((None, 2, L, NA_DH), lambda b, p: (b, p, 0, 0))),
        compiler_params=_cparams("parallel", "parallel"),
        name="ctx_attn",
    )(qkv, qkv, qkv)


def _na_geometry(rows):
    nblk = rows // NA_QROWS
    kr = min(NA_WIN_R, rows)
    variants, var_of_block, kstart = [], [], []
    for b in range(nblk):
        r0 = b * NA_QROWS
        ks = int(np.clip(r0 - NA_WIN_R // 2, 0, rows - NA_KROWS))
        qr = r0 + np.arange(NA_QROWS)
        rs = np.clip(qr - kr // 2, 0, rows - kr)
        key_row = ks + np.arange(NA_KROWS)
        ok = (key_row[None, :] >= rs[:, None]) & (key_row[None, :] < rs[:, None] + kr)
        dr = np.where(ok, key_row[None, :] - qr[:, None] + NA_WIN_R - 1, 0)
        geo = (ok.tobytes(), dr.tobytes())
        if geo not in [g for g, _, _ in variants]:
            variants.append((geo, ok, dr))
        var_of_block.append([g for g, _, _ in variants].index(geo))
        kstart.append(ks)
    return [(ok, dr) for _, ok, dr in variants], var_of_block, kstart


def _na_attn_kernel(q_ref, k_ref, v_ref, kc_ref, vc_ref, bias_ref, o_ref, *, rows):
    scale = NA_DH ** -0.5
    nblk = rows // NA_QROWS
    nq = NA_QROWS * GRID_W
    nk = NA_KROWS * GRID_W
    kc = [kc_ref[hh].astype(BF16) for hh in range(2)]
    vc = [vc_ref[hh].astype(BF16) for hh in range(2)]

    def body(blk, carry):
        ks = jnp.clip(blk * NA_QROWS - NA_WIN_R // 2, 0, rows - NA_KROWS)
        var = jnp.where(blk == 0, 0, jnp.where(blk == nblk - 1, 2, 1))
        q0 = pl.multiple_of(blk * nq, nq)
        k0 = pl.multiple_of(ks * GRID_W, GRID_W)
        for hh in range(2):
            cs = slice(hh * NA_DH, (hh + 1) * NA_DH)
            q = (q_ref[pl.ds(q0, nq), cs] * scale).astype(BF16)
            kw = k_ref[pl.ds(k0, nk), cs].astype(BF16)
            vw = v_ref[pl.ds(k0, nk), cs].astype(BF16)
            s_loc = _dot_nt(q, kw) + bias_ref[hh, var]
            s_ctx = _dot_nt(q, kc[hh])
            m = jnp.maximum(jnp.max(s_loc, axis=-1, keepdims=True),
                            jnp.max(s_ctx, axis=-1, keepdims=True))
            p_loc = jnp.exp(s_loc - m)
            p_ctx = jnp.exp(s_ctx - m)
            den = jnp.sum(p_loc, axis=-1, keepdims=True) + jnp.sum(p_ctx, axis=-1, keepdims=True)
            o = _dot(p_loc.astype(BF16), vw) + _dot(p_ctx.astype(BF16), vc[hh])
            o_ref[pl.ds(q0, nq), cs] = o / den
        return carry

    lax.fori_loop(0, nblk, body, 0)


def _na_attn(qkv, k_ctx, v_ctx, bias):
    B, L, _ = qkv.shape
    P = k_ctx.shape[2]
    rows = L // GRID_W
    hp = NA_HEADS // 2
    kern = functools.partial(_na_attn_kernel, rows=rows)
    return pl.pallas_call(
        kern,
        out_shape=jax.ShapeDtypeStruct((B, L, D_MODEL), F32),
        grid=(B, hp),
        in_specs=[pl.BlockSpec((None, L, 2 * NA_DH), lambda b, p: (b, 0, p)),
                  pl.BlockSpec((None, L, 2 * NA_DH), lambda b, p: (b, 0, hp + p)),
                  pl.BlockSpec((None, L, 2 * NA_DH), lambda b, p: (b, 0, 2 * hp + p)),
                  pl.BlockSpec((None, 2, P, NA_DH), lambda b, p: (b, p, 0, 0)),
                  pl.BlockSpec((None, 2, P, NA_DH), lambda b, p: (b, p, 0, 0)),
                  pl.BlockSpec((2,) + bias.shape[1:], lambda b, p: (p, 0, 0, 0))],
        out_specs=pl.BlockSpec((None, L, 2 * NA_DH), lambda b, p: (b, 0, p)),
        compiler_params=_cparams("parallel", "parallel"),
        name="na_attn",
    )(qkv, qkv, qkv, k_ctx, v_ctx, bias)


def _na_bias_table(rpb, rows):
    variants, var_of_block, kstart = _na_geometry(rows)
    nblk = rows // NA_QROWS
    expect = [0] + [1] * (nblk - 2) + [2]
    assert rows % NA_QROWS == 0 and rows >= NA_KROWS and var_of_block == expect, (rows, var_of_block)
    assert all(kstart[b] == int(np.clip(b * NA_QROWS - NA_WIN_R // 2, 0, rows - NA_KROWS))
               for b in range(nblk))
    qcol = np.arange(GRID_W)
    kcol = np.arange(GRID_W)
    cstart = np.clip(qcol - NA_WIN_C // 2, 0, GRID_W - NA_WIN_C)
    col_ok = (kcol[None, :] >= cstart[:, None]) & (kcol[None, :] < cstart[:, None] + NA_WIN_C)
    dc = np.clip(kcol[None, :] - qcol[:, None], -(NA_WIN_C - 1), NA_WIN_C - 1) + NA_WIN_C - 1
    tabs = []
    for row_ok, dr in variants:
        val = rpb[:, dr[:, :, None, None], dc[None, None, :, :]]
        ok = row_ok[:, :, None, None] & col_ok[None, None, :, :]
        val = jnp.where(ok[None], val, NEG_BIG)
        val = val.transpose(0, 1, 3, 2, 4)
        tabs.append(val.reshape(rpb.shape[0], NA_QROWS * GRID_W, NA_KROWS * GRID_W))
    return jnp.stack(tabs, axis=1)


def _moe_router_kernel(x_ref, g_ref, sc_ref, sh_ref, whi_ref, wlo_ref, br_ref, o_ref):
    h = _norm_mod(x_ref[...], g_ref[...], sc_ref[...], sh_ref[...])
    h_hi, h_lo = _split_bf16(h)
    lg = (_dot(h_hi, whi_ref[...]) + _dot(h_hi, wlo_ref[...]) + _dot(h_lo, whi_ref[...])
          + br_ref[...])
    tm = lg.shape[0]
    gl = [lg[:, g:g + 1] for g in range(MOE_G)]
    gmax = functools.reduce(jnp.maximum, gl)
    gsum = functools.reduce(lambda a, b: a + b, [jnp.exp(v - gmax) for v in gl])
    g_w = 1.0 / gsum
    taken = jnp.zeros((tm, 1), jnp.bool_)
    g_hot = []
    for g in range(MOE_G):
        hit = jnp.logical_and(gl[g] == gmax, jnp.logical_not(taken))
        taken = jnp.logical_or(taken, hit)
        g_hot.append(jnp.where(hit, 1.0, 0.0))
    el = []
    for e in range(MOE_E):
        acc = jnp.zeros((tm, 1), F32)
        for g in range(MOE_G):
            c = MOE_G + g * MOE_E + e
            acc = acc + g_hot[g] * lg[:, c:c + 1]
        el.append(acc)
    emax = functools.reduce(jnp.maximum, el)
    ex = [jnp.exp(v - emax) for v in el]
    esum = functools.reduce(lambda a, b: a + b, ex)
    pe = [v / esum for v in ex]

    def first_max(vals):
        mx = functools.reduce(jnp.maximum, vals)
        tk = jnp.zeros((tm, 1), jnp.bool_)
        hot = []
        for v in vals:
            hit = jnp.logical_and(v == mx, jnp.logical_not(tk))
            tk = jnp.logical_or(tk, hit)
            hot.append(hit)
        return mx, hot

    m1, hot1 = first_max(pe)
    m2, hot2 = first_max([jnp.where(hot1[e], -1.0, pe[e]) for e in range(MOE_E)])
    den = m1 + m2
    w_e = [(jnp.where(hot1[e], m1, 0.0) + jnp.where(hot2[e], m2, 0.0)) / den * g_w
           for e in range(MOE_E)]
    lane = lax.broadcasted_iota(jnp.int32, (tm, LANES), 1)
    out = jnp.zeros((tm, LANES), F32)
    for g in range(MOE_G):
        for e in range(MOE_E):
            out = jnp.where(lane == g * MOE_E + e, g_hot[g] * w_e[e], out)
    o_ref[...] = out


def _moe_router(x, g, sc, sh, w_r, b_r, tm):
    B, L, _ = x.shape
    w_hi, w_lo = _split_bf16(w_r)
    return pl.pallas_call(
        _moe_router_kernel,
        out_shape=jax.ShapeDtypeStruct((B, L, LANES), F32),
        grid=(B, L // tm),
        in_specs=[pl.BlockSpec((None, tm, D_MODEL), lambda b, i: (b, i, 0)),
                  _vec_spec(D_MODEL), _mod_spec(sc), _mod_spec(sh),
                  pl.BlockSpec((D_MODEL, LANES), lambda b, i: (0, 0)),
                  pl.BlockSpec((D_MODEL, LANES), lambda b, i: (0, 0)),
                  _vec_spec(LANES)],
        out_specs=pl.BlockSpec((None, tm, LANES), lambda b, i: (b, i, 0)),
        compiler_params=_cparams("parallel", "parallel"),
        name="moe_router",
    )(x, g, sc, sh, w_hi, w_lo, b_r)


def _moe_experts_kernel(x_ref, g_ref, sc_ref, sh_ref, cw_ref, w1_ref, w3_ref, w2_ref, gt_ref,
                        o_ref, h_ref, acc_ref):
    e = pl.program_id(2)

    @pl.when(e == 0)
    def _():
        h_ref[...] = _norm_mod(x_ref[...], g_ref[...], sc_ref[...], sh_ref[...]).astype(BF16)
        acc_ref[...] = jnp.zeros_like(acc_ref)

    h = h_ref[...]
    a = _dot(h, w1_ref[...])
    b = _dot(h, w3_ref[...])
    cw = cw_ref[...]
    lane = lax.broadcasted_iota(jnp.int32, cw.shape, 1)
    c = jnp.sum(jnp.where(lane == e, cw, 0.0), axis=-1, keepdims=True)
    hid = _silu(a) * b * c
    acc_ref[...] += _dot(hid.astype(BF16), w2_ref[...])

    @pl.when(e == pl.num_programs(2) - 1)
    def _():
        o_ref[...] = x_ref[...] + gt_ref[...] * acc_ref[...]


def _moe_experts(x, g, sc, sh, cw, w1, w3, w2, gt, tm):
    B, L, _ = x.shape
    ne = w1.shape[0]
    return pl.pallas_call(
        _moe_experts_kernel,
        out_shape=jax.ShapeDtypeStruct((B, L, D_MODEL), F32),
        grid=(B, L // tm, ne),
        in_specs=[pl.BlockSpec((None, tm, D_MODEL), lambda b, i, e: (b, i, 0)),
                  _vec_spec(D_MODEL), _mod_spec(sc), _mod_spec(sh),
                  pl.BlockSpec((None, tm, LANES), lambda b, i, e: (b, i, 0)),
                  pl.BlockSpec((None, D_MODEL, MOE_F), lambda b, i, e: (e, 0, 0)),
                  pl.BlockSpec((None, D_MODEL, MOE_F), lambda b, i, e: (e, 0, 0)),
                  pl.BlockSpec((None, MOE_F, D_MODEL), lambda b, i, e: (e, 0, 0)),
                  _mod_spec(gt)],
        out_specs=pl.BlockSpec((None, tm, D_MODEL), lambda b, i, e: (b, i, 0)),
        scratch_shapes=[pltpu.VMEM((tm, D_MODEL), BF16), pltpu.VMEM((tm, D_MODEL), F32)],
        compiler_params=_cparams("parallel", "parallel", "arbitrary"),
        name="moe_experts",
    )(x, g, sc, sh, cw, w1, w3, w2, gt)


def _final_norm_kernel(x_ref, g_ref, o_ref):
    x = x_ref[...]
    ms = jnp.mean(x * x, axis=-1, keepdims=True)
    o_ref[...] = (x * lax.rsqrt(ms + RMS_EPS)) * g_ref[...]


def _final_norm(x, g, tm):
    B, L, _ = x.shape
    return pl.pallas_call(
        _final_norm_kernel,
        out_shape=jax.ShapeDtypeStruct((B, L, D_MODEL), F32),
        grid=(B, L // tm),
        in_specs=[pl.BlockSpec((None, tm, D_MODEL), lambda b, i: (b, i, 0)), _vec_spec(D_MODEL)],
        out_specs=pl.BlockSpec((None, tm, D_MODEL), lambda b, i: (b, i, 0)),
        compiler_params=_cparams("parallel", "parallel"),
        name="final_norm",
    )(x, g)


def _row(v):
    return v.reshape(1, -1)


def _prepare_weights(P):
    W = {}
    W["cv_pw1_w"] = P["cv_pw1_w"].astype(BF16)
    W["cv_pw2_w"] = P["cv_pw2_w"].astype(BF16)
    W["gdn_w_qkvz"] = P["gdn_w_qkvz"].astype(BF16)
    W["gdn_w_ba"] = jnp.pad(P["gdn_w_ba"], ((0, 0), (0, 0), (0, LANES - 4 * GDN_H)))
    a_neg = jnp.exp(P["gdn_a_log"].astype(F32))
    zeros = jnp.zeros_like(a_neg)
    a_cols = jnp.stack([zeros, a_neg], axis=2).reshape(a_neg.shape[0], 4 * GDN_H)
    dt_cols = jnp.stack([zeros, P["gdn_dt_bias"].astype(F32)], axis=2).reshape(a_neg.shape[0], 4 * GDN_H)
    W["gdn_a_row"] = jnp.pad(a_cols, ((0, 0), (0, LANES - 4 * GDN_H)))
    W["gdn_dt_row"] = jnp.pad(dt_cols, ((0, 0), (0, LANES - 4 * GDN_H)))
    W["gdn_w_o"] = P["gdn_w_o"].astype(BF16)
    W["na_w_qkv"] = P["na_w_qkv"].astype(BF16)
    W["na_w_o"] = P["na_w_o"].astype(BF16)
    W["na_rpb"] = P["na_rpb"]
    ne = MOE_G * MOE_E
    W["moe_w1"] = P["moe_w1"].reshape(DEPTH, ne, D_MODEL, MOE_F).astype(BF16)
    W["moe_w3"] = P["moe_w3"].reshape(DEPTH, ne, D_MODEL, MOE_F).astype(BF16)
    W["moe_w2"] = P["moe_w2"].reshape(DEPTH, ne, MOE_F, D_MODEL).astype(BF16)
    w_r = jnp.concatenate([P["moe_wg"], P["moe_we"]], axis=-1)
    W["moe_w_r"] = jnp.pad(w_r, ((0, 0), (0, 0), (0, LANES - w_r.shape[-1])))
    b_r = jnp.concatenate([P["moe_bg"], P["moe_be"]], axis=-1)
    W["moe_b_r"] = jnp.pad(b_r, ((0, 0), (0, LANES - b_r.shape[-1])))
    return W


def _trunk(x, mods, P, W, cache):
    B, L, _ = x.shape
    latent = cache is not None
    Bm = mods.shape[1]
    flat = Bm == 1
    tm_seq = min(L, 256)
    tm_tok = 512
    gdn_states, na_ks, na_vs = [], [], []

    def tok(a):
        return a.reshape(1, B * L, a.shape[-1]) if flat else a

    def seq(a):
        return a.reshape(B, L, a.shape[-1])

    for i in range(DEPTH):
        m = mods[i].reshape(Bm, 1, ADA_CHUNKS, D_MODEL)
        sh1, sc1, gt1, sh2, sc2, gt2 = (m[:, :, c] for c in range(ADA_CHUNKS))
        g1 = _row(P["norm1_g"][i])
        kind, j = i % N_MIXERS, i // N_MIXERS
        if kind == 0:
            u = _nm_glu(tok(x), g1, sc1, sh1, W["cv_pw1_w"][j], _row(P["cv_pw1_b"][j]), tm_tok, 512)
            x = _conv_tail(seq(u), P["cv_dw_w"][j], _row(P["cv_dw_b"][j]), _row(P["cv_ln_g"][j]),
                           _row(P["cv_ln_b"][j]), W["cv_pw2_w"][j], _row(P["cv_pw2_b"][j]),
                           seq(x), gt1, tm_seq)
        elif kind == 1:
            qkvz = _nm_linear(tok(x), g1, sc1, sh1, W["gdn_w_qkvz"][j], tm_tok, 512)
            ba = _nm_small(tok(x), g1, sc1, sh1, W["gdn_w_ba"][j], tm_tok)
            qkv = _gdn_prep(seq(qkvz), P["gdn_conv_w"][j], tm_seq)
            if latent:
                s0 = cache[0][:, j].astype(F32)
            else:
                s0 = jnp.zeros((B, 2, GDN_H, GDN_DK, GDN_DK), F32)
            a_row, dt_row = W["gdn_a_row"][j:j + 1], W["gdn_dt_row"][j:j + 1]
            o_f, s_f = _gdn_chunk(qkv, seq(ba), a_row, dt_row, s0[:, 0], rev=False)
            o_b, s_b = _gdn_chunk(qkv, seq(ba), a_row, dt_row, s0[:, 1], rev=True)
            gdn_states.append(jnp.stack([s_f, s_b], axis=1))
            x = _gdn_out(tok(o_f), tok(o_b), qkvz, _row(P["gdn_o_g"][j]), W["gdn_w_o"][j],
                         tok(x), gt1, tm_tok)
        else:
            qkv = _nm_linear(tok(x), g1, sc1, sh1, W["na_w_qkv"][j], tm_tok, 512)
            if latent:
                bias = _na_bias_table(W["na_rpb"][j], L // GRID_W)
                o = _na_attn(seq(qkv), cache[1][:, j], cache[2][:, j], bias)
            else:
                o, k_ctx, v_ctx = _ctx_attn(seq(qkv))
                na_ks.append(k_ctx)
                na_vs.append(v_ctx)
            x = _linear_res(tok(o), W["na_w_o"][j], tok(x), gt1, tm_tok)
        cw = _moe_router(tok(x), _row(P["norm2_g"][i]), sc2, sh2, W["moe_w_r"][i],
                         W["moe_b_r"][i:i + 1], tm_tok)
        x = _moe_experts(tok(x), _row(P["norm2_g"][i]), sc2, sh2, cw, W["moe_w1"][i], W["moe_w3"][i],
                         W["moe_w2"][i], gt2, tm_tok)
        x = seq(x)
    y = seq(_final_norm(tok(x), _row(P["final_norm_g"]), tm_tok))
    return y, gdn_states, na_ks, na_vs


def kernel(x_prompt, x_sample, state_gdn, cache_na_k, cache_na_v, c, c_ctx,
           ada_w, ada_b, norm1_g, norm2_g,
           cv_pw1_w, cv_pw1_b, cv_dw_w, cv_dw_b, cv_ln_g, cv_ln_b, cv_pw2_w, cv_pw2_b,
           gdn_w_qkvz, gdn_conv_w, gdn_w_ba, gdn_a_log, gdn_dt_bias, gdn_o_g, gdn_w_o,
           na_w_qkv, na_rpb, na_w_o,
           moe_wg, moe_bg, moe_we, moe_be, moe_w1, moe_w3, moe_w2,
           final_norm_g):
    P = dict(norm1_g=norm1_g, norm2_g=norm2_g,
             cv_pw1_w=cv_pw1_w, cv_pw1_b=cv_pw1_b, cv_dw_w=cv_dw_w, cv_dw_b=cv_dw_b,
             cv_ln_g=cv_ln_g, cv_ln_b=cv_ln_b, cv_pw2_w=cv_pw2_w, cv_pw2_b=cv_pw2_b,
             gdn_w_qkvz=gdn_w_qkvz, gdn_conv_w=gdn_conv_w, gdn_w_ba=gdn_w_ba, gdn_a_log=gdn_a_log,
             gdn_dt_bias=gdn_dt_bias, gdn_o_g=gdn_o_g, gdn_w_o=gdn_w_o,
             na_w_qkv=na_w_qkv, na_rpb=na_rpb, na_w_o=na_w_o,
             moe_wg=moe_wg, moe_bg=moe_bg, moe_we=moe_we, moe_be=moe_be,
             moe_w1=moe_w1, moe_w3=moe_w3, moe_w2=moe_w2, final_norm_g=final_norm_g)
    W = _prepare_weights(P)
    nb = c.shape[0]
    rows = 8 * ((nb + 1 + 7) // 8)
    cond = jnp.concatenate([c, c_ctx[None, :], jnp.zeros((rows - nb - 1, D_MODEL), F32)], axis=0)
    mods = _ada_all(cond, ada_w, ada_b)
    y_prompt, gdn_states, na_ks, na_vs = _trunk(x_prompt, mods[:, nb:nb + 1], P, W, None)
    y_sample, _, _, _ = _trunk(x_sample, mods[:, :nb], P, W, (state_gdn, cache_na_k, cache_na_v))
    return (y_prompt, y_sample, jnp.stack(gdn_states, axis=1),
            jnp.stack(na_ks, axis=1), jnp.stack(na_vs, axis=1))
```

```python
import functools

import numpy as np
import jax
import jax.numpy as jnp
from jax import lax
from jax.experimental import pallas as pl
from jax.experimental.pallas import tpu as pltpu

F32 = jnp.float32
BF16 = jnp.bfloat16

D_MODEL = 1024
DEPTH = 4
N_MIXERS = 3
ADA_CHUNKS = 6
RMS_EPS = 1e-6
LN_EPS = 1e-5
CONV_WIDTH = 31
CONV_HALO = 16
GDN_H = 8
GDN_DK = 128
GDN_CONV = 5
GDN_HALO = 8
GDN_BLOCK = 128
GDN_SUB = 16
NA_HEADS = 16
NA_DH = 64
NA_WIN_R = 8
NA_WIN_C = 16
NA_QROWS = 4
NA_KROWS = 12
GRID_W = 64
MOE_G = 4
MOE_E = 4
MOE_F = 256
ROUTER_ROWS = 32
LANES = 128
SUBLANES = 8
NEG_BIG = -1e30

VMEM_LIMIT_BYTES = 48 * 1024 * 1024


def _cparams(*sem):
    return pltpu.CompilerParams(dimension_semantics=sem, vmem_limit_bytes=VMEM_LIMIT_BYTES)


def _sigmoid(x):
    return 1.0 / (1.0 + jnp.exp(-x))


def _silu(x):
    return x * _sigmoid(x)


def _norm_mod(x, g, sc, sh):
    ms = jnp.mean(x * x, axis=-1, keepdims=True)
    y = x * lax.rsqrt(ms + RMS_EPS)
    return (y * g) * (1.0 + sc) + sh


def _dot(a, b):
    return jnp.dot(a, b, preferred_element_type=F32)


def _dot_nt(a, b):
    return lax.dot_general(a, b, (((1,), (1,)), ((), ())), preferred_element_type=F32)


def _dot_f32(a, b):
    return jnp.dot(a, b, preferred_element_type=F32, precision=lax.Precision.HIGHEST)


def _split_bf16(w):
    hi = w.astype(BF16)
    lo = (w - hi.astype(F32)).astype(BF16)
    return hi, lo


def _mod_spec(mod):
    if mod.shape[0] == 1:
        return pl.BlockSpec((None, 1, D_MODEL), lambda b, *_: (0, 0, 0))
    return pl.BlockSpec((None, 1, D_MODEL), lambda b, *_: (b, 0, 0))


def _vec_spec(n):
    return pl.BlockSpec((1, n), lambda *_: (0, 0))


def _ada_kernel(c_ref, w_ref, b_ref, o_ref):
    s = _silu(c_ref[...]).astype(BF16)
    o_ref[...] = _dot(s, w_ref[...].astype(BF16)) + b_ref[...]


def _ada_all(cond, ada_w, ada_b, tn=1024):
    R = cond.shape[0]
    N = ada_w.shape[-1]
    return pl.pallas_call(
        _ada_kernel,
        out_shape=jax.ShapeDtypeStruct((DEPTH, R, N), F32),
        grid=(DEPTH, N // tn),
        in_specs=[pl.BlockSpec((R, D_MODEL), lambda l, j: (0, 0)),
                  pl.BlockSpec((None, D_MODEL, tn), lambda l, j: (l, 0, j)),
                  pl.BlockSpec((None, 1, tn), lambda l, j: (l, 0, j))],
        out_specs=pl.BlockSpec((None, R, tn), lambda l, j: (l, 0, j)),
        compiler_params=_cparams("parallel", "parallel"),
        name="ada",
    )(cond, ada_w, ada_b.reshape(DEPTH, 1, N))


def _nm_linear_kernel(x_ref, g_ref, sc_ref, sh_ref, w_ref, o_ref, h_ref):
    @pl.when(pl.program_id(2) == 0)
    def _():
        h_ref[...] = _norm_mod(x_ref[...], g_ref[...], sc_ref[...], sh_ref[...]).astype(BF16)

    o_ref[...] = _dot(h_ref[...], w_ref[...]).astype(o_ref.dtype)


def _nm_linear(x, g, sc, sh, w, tm, tn):
    B, L, _ = x.shape
    N = w.shape[1]
    return pl.pallas_call(
        _nm_linear_kernel,
        out_shape=jax.ShapeDtypeStruct((B, L, N), F32),
        grid=(B, L // tm, N // tn),
        in_specs=[pl.BlockSpec((None, tm, D_MODEL), lambda b, i, j: (b, i, 0)),
                  _vec_spec(D_MODEL), _mod_spec(sc), _mod_spec(sh),
                  pl.BlockSpec((D_MODEL, tn), lambda b, i, j: (0, j))],
        out_specs=pl.BlockSpec((None, tm, tn), lambda b, i, j: (b, i, j)),
        scratch_shapes=[pltpu.VMEM((tm, D_MODEL), BF16)],
        compiler_params=_cparams("parallel", "parallel", "arbitrary"),
        name="nm_linear",
    )(x, g, sc, sh, w)


def _nm_glu_kernel(x_ref, g_ref, sc_ref, sh_ref, wa_ref, wg_ref, ba_ref, bg_ref, o_ref, h_ref):
    @pl.when(pl.program_id(2) == 0)
    def _():
        h_ref[...] = _norm_mod(x_ref[...], g_ref[...], sc_ref[...], sh_ref[...]).astype(BF16)

    h = h_ref[...]
    a = _dot(h, wa_ref[...]) + ba_ref[...]
    gate = _dot(h, wg_ref[...]) + bg_ref[...]
    o_ref[...] = a * _sigmoid(gate)


def _nm_glu(x, g, sc, sh, w, b, tm, tn):
    B, L, _ = x.shape
    N = w.shape[1] // 2
    nj = N // tn
    return pl.pallas_call(
        _nm_glu_kernel,
        out_shape=jax.ShapeDtypeStruct((B, L, N), F32),
        grid=(B, L // tm, nj),
        in_specs=[pl.BlockSpec((None, tm, D_MODEL), lambda b_, i, j: (b_, i, 0)),
                  _vec_spec(D_MODEL), _mod_spec(sc), _mod_spec(sh),
                  pl.BlockSpec((D_MODEL, tn), lambda b_, i, j: (0, j)),
                  pl.BlockSpec((D_MODEL, tn), lambda b_, i, j: (0, j + nj)),
                  pl.BlockSpec((1, tn), lambda b_, i, j: (0, j)),
                  pl.BlockSpec((1, tn), lambda b_, i, j: (0, j + nj))],
        out_specs=pl.BlockSpec((None, tm, tn), lambda b_, i, j: (b_, i, j)),
        scratch_shapes=[pltpu.VMEM((tm, D_MODEL), BF16)],
        compiler_params=_cparams("parallel", "parallel", "arbitrary"),
        name="nm_glu",
    )(x, g, sc, sh, w, w, b, b)


def _nm_small_kernel(x_ref, g_ref, sc_ref, sh_ref, whi_ref, wlo_ref, o_ref):
    h = _norm_mod(x_ref[...], g_ref[...], sc_ref[...], sh_ref[...])
    h_hi, h_lo = _split_bf16(h)
    o_ref[...] = _dot(h_hi, whi_ref[...]) + _dot(h_hi, wlo_ref[...]) + _dot(h_lo, whi_ref[...])


def _nm_small(x, g, sc, sh, w, tm):
    B, L, _ = x.shape
    w_hi, w_lo = _split_bf16(w)
    return pl.pallas_call(
        _nm_small_kernel,
        out_shape=jax.ShapeDtypeStruct((B, L, LANES), F32),
        grid=(B, L // tm),
        in_specs=[pl.BlockSpec((None, tm, D_MODEL), lambda b, i: (b, i, 0)),
                  _vec_spec(D_MODEL), _mod_spec(sc), _mod_spec(sh),
                  pl.BlockSpec((D_MODEL, LANES), lambda b, i: (0, 0)),
                  pl.BlockSpec((D_MODEL, LANES), lambda b, i: (0, 0))],
        out_specs=pl.BlockSpec((None, tm, LANES), lambda b, i: (b, i, 0)),
        compiler_params=_cparams("parallel", "parallel"),
        name="nm_small",
    )(x, g, sc, sh, w_hi, w_lo)


def _linear_res_kernel(a_ref, w_ref, x_ref, gt_ref, o_ref):
    y = _dot(a_ref[...].astype(BF16), w_ref[...])
    o_ref[...] = x_ref[...] + gt_ref[...] * y


def _linear_res(a, w, x, gt, tm):
    B, L, K = a.shape
    return pl.pallas_call(
        _linear_res_kernel,
        out_shape=jax.ShapeDtypeStruct((B, L, D_MODEL), F32),
        grid=(B, L // tm),
        in_specs=[pl.BlockSpec((None, tm, K), lambda b, i: (b, i, 0)),
                  pl.BlockSpec((K, D_MODEL), lambda b, i: (0, 0)),
                  pl.BlockSpec((None, tm, D_MODEL), lambda b, i: (b, i, 0)),
                  _mod_spec(gt)],
        out_specs=pl.BlockSpec((None, tm, D_MODEL), lambda b, i: (b, i, 0)),
        compiler_params=_cparams("parallel", "parallel"),
        name="linear_res",
    )(a, w, x, gt)


def _conv_tail_kernel(u_ref, up_ref, un_ref, dww_ref, dwb_ref, lng_ref, lnb_ref, w2_ref, b2_ref,
                      x_ref, gt_ref, o_ref, buf_ref, sh_ref, cv_ref, *, tm, rc):
    i = pl.program_id(1)
    last = pl.num_programs(1) - 1
    buf_ref[0:CONV_HALO, :] = jnp.where(i > 0, up_ref[...], 0.0)
    buf_ref[CONV_HALO:CONV_HALO + tm, :] = u_ref[...]
    buf_ref[CONV_HALO + tm:, :] = jnp.where(i < last, un_ref[...], 0.0)
    off = CONV_HALO - CONV_WIDTH // 2
    nsh = sh_ref.shape[1]
    for b in range(SUBLANES):
        sh_ref[b] = buf_ref[b:b + nsh, :]
    for c in range(D_MODEL // LANES):
        cs = slice(c * LANES, (c + 1) * LANES)
        for r0 in range(0, tm, rc):
            acc = jnp.zeros((rc, LANES), F32)
            for k in range(CONV_WIDTH):
                a, b = divmod(off + k, SUBLANES)
                acc = acc + dww_ref[k:k + 1, cs] * sh_ref[b, r0 + a * SUBLANES:r0 + a * SUBLANES + rc, cs]
            cv_ref[r0:r0 + rc, cs] = acc + dwb_ref[:, cs]
    v = cv_ref[...]
    mu = jnp.mean(v, axis=-1, keepdims=True)
    vc = v - mu
    var = jnp.mean(vc * vc, axis=-1, keepdims=True)
    y = _silu(vc * lax.rsqrt(var + LN_EPS) * lng_ref[...] + lnb_ref[...])
    z = _dot(y.astype(BF16), w2_ref[...]) + b2_ref[...]
    o_ref[...] = x_ref[...] + gt_ref[...] * z


def _conv_tail(u, dw_w, dw_b, ln_g, ln_b, w2, b2, x, gt, tm):
    B, L, _ = u.shape
    hb = tm // CONV_HALO
    nh = L // CONV_HALO
    dww = jnp.pad(dw_w, ((0, 32 - CONV_WIDTH), (0, 0)))
    last_tap = CONV_HALO - CONV_WIDTH // 2 + CONV_WIDTH - 1
    kern = functools.partial(_conv_tail_kernel, tm=tm, rc=64)
    return pl.pallas_call(
        kern,
        out_shape=jax.ShapeDtypeStruct((B, L, D_MODEL), F32),
        grid=(B, L // tm),
        in_specs=[pl.BlockSpec((None, tm, D_MODEL), lambda b, i: (b, i, 0)),
                  pl.BlockSpec((None, CONV_HALO, D_MODEL),
                               lambda b, i: (b, jnp.maximum(i * hb - 1, 0), 0)),
                  pl.BlockSpec((None, CONV_HALO, D_MODEL),
                               lambda b, i: (b, jnp.minimum((i + 1) * hb, nh - 1), 0)),
                  pl.BlockSpec((32, D_MODEL), lambda b, i: (0, 0)),
                  _vec_spec(D_MODEL), _vec_spec(D_MODEL), _vec_spec(D_MODEL),
                  pl.BlockSpec((D_MODEL, D_MODEL), lambda b, i: (0, 0)),
                  _vec_spec(D_MODEL),
                  pl.BlockSpec((None, tm, D_MODEL), lambda b, i: (b, i, 0)),
                  _mod_spec(gt)],
        out_specs=pl.BlockSpec((None, tm, D_MODEL), lambda b, i: (b, i, 0)),
        scratch_shapes=[pltpu.VMEM((tm + 2 * CONV_HALO, D_MODEL), F32),
                        pltpu.VMEM((SUBLANES, tm + SUBLANES * (last_tap // SUBLANES), D_MODEL), F32),
                        pltpu.VMEM((tm, D_MODEL), F32)],
        compiler_params=_cparams("parallel", "parallel"),
        name="conv_tail",
    )(u, u, u, dww, dw_b, ln_g, ln_b, w2, b2, x, gt)


def _gdn_prep_kernel(u_ref, up_ref, un_ref, w_ref, o_ref, buf_ref, *, tm):
    i = pl.program_id(1)
    j = pl.program_id(2)
    last = pl.num_programs(1) - 1
    buf_ref[0:GDN_HALO, :] = jnp.where(i > 0, up_ref[...], 0.0)
    buf_ref[GDN_HALO:GDN_HALO + tm, :] = u_ref[...]
    buf_ref[GDN_HALO + tm:, :] = jnp.where(i < last, un_ref[...], 0.0)
    off = GDN_HALO - GDN_CONV // 2
    for h in range(GDN_H):
        cs = slice(h * GDN_DK, (h + 1) * GDN_DK)
        acc = jnp.zeros((tm, GDN_DK), F32)
        for k in range(GDN_CONV):
            acc = acc + w_ref[k:k + 1, cs] * buf_ref[off + k:off + k + tm, cs]
        y = _silu(acc)
        nrm = y * lax.rsqrt(jnp.sum(y * y, axis=-1, keepdims=True) + 1e-6)
        o_ref[:, cs] = jnp.where(j == 0, nrm * (GDN_DK ** -0.5), jnp.where(j == 1, nrm, y))


def _gdn_prep(qkvz, conv_w, tm):
    B, L, _ = qkvz.shape
    W = GDN_H * GDN_DK
    hb = tm // GDN_HALO
    nh = L // GDN_HALO
    cw = jnp.pad(conv_w, ((0, 8 - GDN_CONV), (0, 0)))
    kern = functools.partial(_gdn_prep_kernel, tm=tm)
    return pl.pallas_call(
        kern,
        out_shape=jax.ShapeDtypeStruct((B, L, 3 * W), F32),
        grid=(B, L // tm, 3),
        in_specs=[pl.BlockSpec((None, tm, W), lambda b, i, j: (b, i, j)),
                  pl.BlockSpec((None, GDN_HALO, W), lambda b, i, j: (b, jnp.maximum(i * hb - 1, 0), j)),
                  pl.BlockSpec((None, GDN_HALO, W),
                               lambda b, i, j: (b, jnp.minimum((i + 1) * hb, nh - 1), j)),
                  pl.BlockSpec((8, W), lambda b, i, j: (0, j))],
        out_specs=pl.BlockSpec((None, tm, W), lambda b, i, j: (b, i, j)),
        scratch_shapes=[pltpu.VMEM((tm + 2 * GDN_HALO, W), F32)],
        compiler_params=_cparams("parallel", "parallel", "arbitrary"),
        name="gdn_prep",
    )(qkvz, qkvz, qkvz, cw)


def _split3(x):
    hi = x.astype(BF16)
    r1 = x - hi.astype(F32)
    mid = r1.astype(BF16)
    lo = (r1 - mid.astype(F32)).astype(BF16)
    return hi, mid, lo


def _mm3(a, b):
    lhs = jnp.concatenate([a[0], a[1]], axis=1)
    rhs = jnp.concatenate([b[0], b[0]], axis=0)
    return _dot(lhs, rhs) + _dot(a[0], b[1])


def _gdn_chunk_kernel(q_ref, k_ref, v_ref, ba_ref, a_ref, dtb_ref, s0_ref, o_ref, sfin_ref, s_ref,
                      *, rev, col0):
    n = pl.program_id(1)
    C = GDN_BLOCK

    @pl.when(n == 0)
    def _():
        s_ref[...] = s0_ref[...]

    ri = lax.broadcasted_iota(jnp.int32, (C, C), 0)
    ci = lax.broadcasted_iota(jnp.int32, (C, C), 1)
    if rev:
        incl, strict = ri <= ci, ri < ci
    else:
        incl, strict = ri >= ci, ri > ci
    blk = (ri // GDN_SUB) == (ci // GDN_SUB)
    eye_f = jnp.where(ri == ci, 1.0, 0.0)

    ba = ba_ref[...]
    beta_all = _sigmoid(ba)
    xa = ba + dtb_ref[...]
    softplus = jnp.maximum(xa, 0.0) + jnp.log(1.0 + jnp.exp(-jnp.abs(xa)))
    g_all = -a_ref[...] * softplus
    tri = jnp.where(incl, 1.0, 0.0).astype(BF16)
    g_hi, g_mid, g_lo = _split3(g_all)
    cum_all = _dot(tri, g_hi) + _dot(tri, g_mid) + _dot(tri, g_lo)
    cum_t = cum_all.T
    tot_row = cum_all[0:1, :] if rev else cum_all[C - 1:C, :]
    ecum_all = jnp.exp(cum_all)
    etail_all = jnp.exp(tot_row - cum_all)
    etot_row = jnp.exp(tot_row)

    HS = range(GDN_H)
    cs = [slice(h * GDN_DK, (h + 1) * GDN_DK) for h in HS]
    cg = [col0 + GDN_H + h for h in HS]
    k = [k_ref[:, cs[h]] for h in HS]
    kbf = [k[h].astype(BF16) for h in HS]
    beta = [beta_all[:, col0 + h:col0 + h + 1] for h in HS]
    kb = [k[h] * beta[h] for h in HS]
    decay = [jnp.where(incl, jnp.exp(jnp.where(
        incl, cum_all[:, cg[h]:cg[h] + 1] - cum_t[cg[h]:cg[h] + 1, :], 0.0)), 0.0) for h in HS]
    nm = [jnp.where(strict, _dot_nt(kb[h].astype(BF16), kbf[h]) * decay[h], 0.0) for h in HS]
    nd = [jnp.where(blk, nm[h], 0.0) for h in HS]
    nd_s = [_split_bf16(nd[h]) for h in HS]
    nd2_s = [_split_bf16(_mm3(nd_s[h], nd_s[h])) for h in HS]
    nd4_s = [_split_bf16(_mm3(nd2_s[h], nd2_s[h])) for h in HS]
    nd8_s = [_split_bf16(_mm3(nd4_s[h], nd4_s[h])) for h in HS]
    t = [eye_f - nd[h] for h in HS]
    t = [t[h] + _mm3(_split_bf16(t[h]), nd2_s[h]) for h in HS]
    t = [t[h] + _mm3(_split_bf16(t[h]), nd4_s[h]) for h in HS]
    t = [t[h] + _mm3(_split_bf16(t[h]), nd8_s[h]) for h in HS]
    t_s = [_split_bf16(t[h]) for h in HS]
    m_s = [_split_bf16(_mm3(t_s[h], _split_bf16(nm[h] - nd[h]))) for h in HS]
    m2_s = [_split_bf16(_mm3(m_s[h], m_s[h])) for h in HS]
    m4_s = [_split_bf16(_mm3(m2_s[h], m2_s[h])) for h in HS]
    a_inv = [t[h] + _mm3(m4_s[h], t_s[h]) for h in HS]
    a_inv = [a_inv[h] + _mm3(m2_s[h], _split_bf16(a_inv[h])) for h in HS]
    a_inv = [a_inv[h] - _mm3(m_s[h], _split_bf16(a_inv[h])) for h in HS]
    e_cum = [ecum_all[:, cg[h]:cg[h] + 1] for h in HS]
    rhs = [jnp.concatenate([v_ref[:, cs[h]] * beta[h], kb[h] * e_cum[h]], axis=-1) for h in HS]
    sol = [_mm3(_split_bf16(a_inv[h]), _split_bf16(rhs[h])) for h in HS]
    q = [q_ref[:, cs[h]] for h in HS]
    qk = [jnp.where(incl, _dot_nt(q[h].astype(BF16), kbf[h]) * decay[h], 0.0).astype(BF16)
          for h in HS]
    q_g = [(q[h] * e_cum[h]).astype(BF16) for h in HS]
    k_tail_t = [(k[h] * etail_all[:, cg[h]:cg[h] + 1]).T.astype(BF16) for h in HS]
    s = [s_ref[h] for h in HS]
    sb = [s[h].astype(BF16) for h in HS]
    vb = [(sol[h][:, :GDN_DK] - _dot(sol[h][:, GDN_DK:].astype(BF16), sb[h])).astype(BF16)
          for h in HS]
    for h in HS:
        o_ref[:, cs[h]] = _dot(q_g[h], sb[h]) + _dot(qk[h], vb[h])
    for h in HS:
        s_ref[h] = s[h] * etot_row[:, cg[h]:cg[h] + 1] + _dot(k_tail_t[h], vb[h])

    @pl.when(n == pl.num_programs(1) - 1)
    def _():
        sfin_ref[...] = s_ref[...]


def _gdn_chunk(qkv, ba, a_row, dtb_row, s0, rev):
    B, L, _ = qkv.shape
    W = GDN_H * GDN_DK
    nc = L // GDN_BLOCK
    cidx = (lambda n: nc - 1 - n) if rev else (lambda n: n)
    kern = functools.partial(_gdn_chunk_kernel, rev=rev, col0=2 * GDN_H if rev else 0)
    return pl.pallas_call(
        kern,
        out_shape=(jax.ShapeDtypeStruct((B, L, W), F32),
                   jax.ShapeDtypeStruct((B, GDN_H, GDN_DK, GDN_DK), F32)),
        grid=(B, nc),
        in_specs=[pl.BlockSpec((None, GDN_BLOCK, W), lambda b, n: (b, cidx(n), 0)),
                  pl.BlockSpec((None, GDN_BLOCK, W), lambda b, n: (b, cidx(n), 1)),
                  pl.BlockSpec((None, GDN_BLOCK, W), lambda b, n: (b, cidx(n), 2)),
                  pl.BlockSpec((None, GDN_BLOCK, LANES), lambda b, n: (b, cidx(n), 0)),
                  _vec_spec(LANES), _vec_spec(LANES),
                  pl.BlockSpec((None, GDN_H, GDN_DK, GDN_DK), lambda b, n: (b, 0, 0, 0))],
        out_specs=(pl.BlockSpec((None, GDN_BLOCK, W), lambda b, n: (b, cidx(n), 0)),
                   pl.BlockSpec((None, GDN_H, GDN_DK, GDN_DK), lambda b, n: (b, 0, 0, 0))),
        scratch_shapes=[pltpu.VMEM((GDN_H, GDN_DK, GDN_DK), F32)],
        compiler_params=_cparams("parallel", "arbitrary"),
        name="gdn_chunk_rev" if rev else "gdn_chunk_fwd",
    )(qkv, qkv, qkv, ba, a_row, dtb_row, s0)


def _gdn_out_kernel(of_ref, ob_ref, z_ref, og_ref, w_ref, x_ref, gt_ref, o_ref, y_ref):
    o = of_ref[...] + ob_ref[...]
    z = z_ref[...]
    for h in range(GDN_H):
        cs = slice(h * GDN_DK, (h + 1) * GDN_DK)
        oh = o[:, cs]
        ms = jnp.mean(oh * oh, axis=-1, keepdims=True)
        y = (oh * lax.rsqrt(ms + RMS_EPS)) * og_ref[...]
        y_ref[:, cs] = (y * _silu(z[:, cs])).astype(BF16)
    o_ref[...] = x_ref[...] + gt_ref[...] * _dot(y_ref[...], w_ref[...])


def _gdn_out(o_f, o_b, qkvz, o_g, w_o, x, gt, tm):
    B, L, W = o_f.shape
    return pl.pallas_call(
        _gdn_out_kernel,
        out_shape=jax.ShapeDtypeStruct((B, L, D_MODEL), F32),
        grid=(B, L // tm),
        in_specs=[pl.BlockSpec((None, tm, W), lambda b, i: (b, i, 0)),
                  pl.BlockSpec((None, tm, W), lambda b, i: (b, i, 0)),
                  pl.BlockSpec((None, tm, W), lambda b, i: (b, i, 3)),
                  _vec_spec(GDN_DK),
                  pl.BlockSpec((W, D_MODEL), lambda b, i: (0, 0)),
                  pl.BlockSpec((None, tm, D_MODEL), lambda b, i: (b, i, 0)),
                  _mod_spec(gt)],
        out_specs=pl.BlockSpec((None, tm, D_MODEL), lambda b, i: (b, i, 0)),
        scratch_shapes=[pltpu.VMEM((tm, W), BF16)],
        compiler_params=_cparams("parallel", "parallel"),
        name="gdn_out",
    )(o_f, o_b, qkvz, o_g, w_o, x, gt)


def _ctx_attn_kernel(q_ref, k_ref, v_ref, o_ref, ko_ref, vo_ref):
    scale = NA_DH ** -0.5
    for hh in range(2):
        cs = slice(hh * NA_DH, (hh + 1) * NA_DH)
        q = q_ref[:, cs]
        k = k_ref[:, cs]
        v = v_ref[:, cs]
        s = _dot_nt(q.astype(BF16), k.astype(BF16)) * scale
        p = jnp.exp(s - jnp.max(s, axis=-1, keepdims=True))
        p = p / jnp.sum(p, axis=-1, keepdims=True)
        o_ref[:, cs] = _dot(p.astype(BF16), v.astype(BF16))
        ko_ref[hh] = k
        vo_ref[hh] = v


def _ctx_attn(qkv):
    B, L, _ = qkv.shape
    hp = NA_HEADS // 2
    kv_shape = jax.ShapeDtypeStruct((B, NA_HEADS, L, NA_DH), F32)
    return pl.pallas_call(
        _ctx_attn_kernel,
        out_shape=(jax.ShapeDtypeStruct((B, L, D_MODEL), F32), kv_shape, kv_shape),
        grid=(B, hp),
        in_specs=[pl.BlockSpec((None, L, 2 * NA_DH), lambda b, p: (b, 0, p)),
                  pl.BlockSpec((None, L, 2 * NA_DH), lambda b, p: (b, 0, hp + p)),
                  pl.BlockSpec((None, L, 2 * NA_DH), lambda b, p: (b, 0, 2 * hp + p))],
        out_specs=(pl.BlockSpec((None, L, 2 * NA_DH), lambda b, p: (b, 0, p)),
                   pl.BlockSpec((None, 2, L, NA_DH), lambda b, p: (b, p, 0, 0)),
                   pl.BlockSpec((None, 2, L, NA_DH), lambda b, p: (b, p, 0, 0))),
        compiler_params=_cparams("parallel", "parallel"),
        name="ctx_attn",
    )(qkv, qkv, qkv)


def _na_geometry(rows):
    nblk = rows // NA_QROWS
    kr = min(NA_WIN_R, rows)
    variants, var_of_block, kstart = [], [], []
    for b in range(nblk):
        r0 = b * NA_QROWS
        ks = int(np.clip(r0 - NA_WIN_R // 2, 0, rows - NA_KROWS))
        qr = r0 + np.arange(NA_QROWS)
        rs = np.clip(qr - kr // 2, 0, rows - kr)
        key_row = ks + np.arange(NA_KROWS)
        ok = (key_row[None, :] >= rs[:, None]) & (key_row[None, :] < rs[:, None] + kr)
        dr = np.where(ok, key_row[None, :] - qr[:, None] + NA_WIN_R - 1, 0)
        geo = (ok.tobytes(), dr.tobytes())
        if geo not in [g for g, _, _ in variants]:
            variants.append((geo, ok, dr))
        var_of_block.append([g for g, _, _ in variants].index(geo))
        kstart.append(ks)
    return [(ok, dr) for _, ok, dr in variants], var_of_block, kstart


def _na_attn_kernel(q_ref, k_ref, v_ref, kc_ref, vc_ref, bias_ref, o_ref, *, rows):
    scale = NA_DH ** -0.5
    nblk = rows // NA_QROWS
    nq = NA_QROWS * GRID_W
    nk = NA_KROWS * GRID_W
    kc = [kc_ref[hh].astype(BF16) for hh in range(2)]
    vc = [vc_ref[hh].astype(BF16) for hh in range(2)]

    def body(blk, carry):
        ks = jnp.clip(blk * NA_QROWS - NA_WIN_R // 2, 0, rows - NA_KROWS)
        var = jnp.where(blk == 0, 0, jnp.where(blk == nblk - 1, 2, 1))
        q0 = pl.multiple_of(blk * nq, nq)
        k0 = pl.multiple_of(ks * GRID_W, GRID_W)
        for hh in range(2):
            cs = slice(hh * NA_DH, (hh + 1) * NA_DH)
            q = (q_ref[pl.ds(q0, nq), cs] * scale).astype(BF16)
            kw = k_ref[pl.ds(k0, nk), cs].astype(BF16)
            vw = v_ref[pl.ds(k0, nk), cs].astype(BF16)
            s_loc = _dot_nt(q, kw) + bias_ref[hh, var]
            s_ctx = _dot_nt(q, kc[hh])
            m = jnp.maximum(jnp.max(s_loc, axis=-1, keepdims=True),
                            jnp.max(s_ctx, axis=-1, keepdims=True))
            p_loc = jnp.exp(s_loc - m)
            p_ctx = jnp.exp(s_ctx - m)
            den = jnp.sum(p_loc, axis=-1, keepdims=True) + jnp.sum(p_ctx, axis=-1, keepdims=True)
            o = _dot(p_loc.astype(BF16), vw) + _dot(p_ctx.astype(BF16), vc[hh])
            o_ref[pl.ds(q0, nq), cs] = o / den
        return carry

    lax.fori_loop(0, nblk, body, 0)


def _na_attn(qkv, k_ctx, v_ctx, bias):
    B, L, _ = qkv.shape
    P = k_ctx.shape[2]
    rows = L // GRID_W
    hp = NA_HEADS // 2
    kern = functools.partial(_na_attn_kernel, rows=rows)
    return pl.pallas_call(
        kern,
        out_shape=jax.ShapeDtypeStruct((B, L, D_MODEL), F32),
        grid=(B, hp),
        in_specs=[pl.BlockSpec((None, L, 2 * NA_DH), lambda b, p: (b, 0, p)),
                  pl.BlockSpec((None, L, 2 * NA_DH), lambda b, p: (b, 0, hp + p)),
                  pl.BlockSpec((None, L, 2 * NA_DH), lambda b, p: (b, 0, 2 * hp + p)),
                  pl.BlockSpec((None, 2, P, NA_DH), lambda b, p: (b, p, 0, 0)),
                  pl.BlockSpec((None, 2, P, NA_DH), lambda b, p: (b, p, 0, 0)),
                  pl.BlockSpec((2,) + bias.shape[1:], lambda b, p: (p, 0, 0, 0))],
        out_specs=pl.BlockSpec((None, L, 2 * NA_DH), lambda b, p: (b, 0, p)),
        compiler_params=_cparams("parallel", "parallel"),
        name="na_attn",
    )(qkv, qkv, qkv, k_ctx, v_ctx, bias)


def _na_bias_table(rpb, rows):
    variants, var_of_block, kstart = _na_geometry(rows)
    nblk = rows // NA_QROWS
    expect = [0] + [1] * (nblk - 2) + [2]
    assert rows % NA_QROWS == 0 and rows >= NA_KROWS and var_of_block == expect, (rows, var_of_block)
    assert all(kstart[b] == int(np.clip(b * NA_QROWS - NA_WIN_R // 2, 0, rows - NA_KROWS))
               for b in range(nblk))
    qcol = np.arange(GRID_W)
    kcol = np.arange(GRID_W)
    cstart = np.clip(qcol - NA_WIN_C // 2, 0, GRID_W - NA_WIN_C)
    col_ok = (kcol[None, :] >= cstart[:, None]) & (kcol[None, :] < cstart[:, None] + NA_WIN_C)
    dc = np.clip(kcol[None, :] - qcol[:, None], -(NA_WIN_C - 1), NA_WIN_C - 1) + NA_WIN_C - 1
    plane = jnp.where(col_ok[None, None], rpb[:, :, dc], NEG_BIG)
    masked = jnp.full((rpb.shape[0], GRID_W, GRID_W), NEG_BIG, rpb.dtype)
    tabs = []
    for row_ok, dr in variants:
        q_rows = [jnp.concatenate([plane[:, int(dr[a, c])] if row_ok[a, c] else masked
                                   for c in range(NA_KROWS)], axis=-1) for a in range(NA_QROWS)]
        tabs.append(jnp.concatenate(q_rows, axis=1))
    return jnp.stack(tabs, axis=1)


def _moe_router_kernel(x_ref, g_ref, sc_ref, sh_ref, whi_ref, wlo_ref, br_ref, o_ref):
    h = _norm_mod(x_ref[...], g_ref[...], sc_ref[...], sh_ref[...])
    h_hi, h_lo = _split_bf16(h)
    lg = (_dot_nt(whi_ref[...], h_hi) + _dot_nt(wlo_ref[...], h_hi) + _dot_nt(whi_ref[...], h_lo)
          + br_ref[...])
    tm = lg.shape[1]

    def first_max(vals):
        mx = functools.reduce(jnp.maximum, vals)
        taken = jnp.zeros((1, tm), jnp.bool_)
        hot = []
        for v in vals:
            hit = jnp.logical_and(v == mx, jnp.logical_not(taken))
            taken = jnp.logical_or(taken, hit)
            hot.append(hit)
        return mx, hot

    gl = [lg[g:g + 1, :] for g in range(MOE_G)]
    gmax, g_hit = first_max(gl)
    gsum = functools.reduce(lambda a, b: a + b, [jnp.exp(v - gmax) for v in gl])
    g_w = 1.0 / gsum
    g_hot = [jnp.where(hit, 1.0, 0.0) for hit in g_hit]
    el = []
    for e in range(MOE_E):
        acc = jnp.zeros((1, tm), F32)
        for g in range(MOE_G):
            r = MOE_G + g * MOE_E + e
            acc = acc + g_hot[g] * lg[r:r + 1, :]
        el.append(acc)
    emax = functools.reduce(jnp.maximum, el)
    ex = [jnp.exp(v - emax) for v in el]
    esum = functools.reduce(lambda a, b: a + b, ex)
    pe = [v / esum for v in ex]
    m1, hot1 = first_max(pe)
    m2, hot2 = first_max([jnp.where(hot1[e], -1.0, pe[e]) for e in range(MOE_E)])
    den = m1 + m2
    w_e = [(jnp.where(hot1[e], m1, 0.0) + jnp.where(hot2[e], m2, 0.0)) / den * g_w
           for e in range(MOE_E)]
    rows = [g_hot[g] * w_e[e] for g in range(MOE_G) for e in range(MOE_E)]
    rows.append(jnp.zeros((LANES - len(rows), tm), F32))
    o_ref[...] = jnp.concatenate(rows, axis=0).T


def _moe_router(x, g, sc, sh, w_rt, b_r, tm):
    B, L, _ = x.shape
    w_hi, w_lo = _split_bf16(w_rt)
    return pl.pallas_call(
        _moe_router_kernel,
        out_shape=jax.ShapeDtypeStruct((B, L, LANES), F32),
        grid=(B, L // tm),
        in_specs=[pl.BlockSpec((None, tm, D_MODEL), lambda b, i: (b, i, 0)),
                  _vec_spec(D_MODEL), _mod_spec(sc), _mod_spec(sh),
                  pl.BlockSpec((ROUTER_ROWS, D_MODEL), lambda b, i: (0, 0)),
                  pl.BlockSpec((ROUTER_ROWS, D_MODEL), lambda b, i: (0, 0)),
                  pl.BlockSpec((ROUTER_ROWS, 1), lambda b, i: (0, 0))],
        out_specs=pl.BlockSpec((None, tm, LANES), lambda b, i: (b, i, 0)),
        compiler_params=_cparams("parallel", "parallel"),
        name="moe_router",
    )(x, g, sc, sh, w_hi, w_lo, b_r)


def _moe_experts_kernel(x_ref, g_ref, sc_ref, sh_ref, cw_ref, w1_ref, w3_ref, w2_ref, gt_ref,
                        o_ref, h_ref, acc_ref):
    e = pl.program_id(2)

    @pl.when(e == 0)
    def _():
        h_ref[...] = _norm_mod(x_ref[...], g_ref[...], sc_ref[...], sh_ref[...]).astype(BF16)
        acc_ref[...] = jnp.zeros_like(acc_ref)

    h = h_ref[...]
    a = _dot(h, w1_ref[...])
    b = _dot(h, w3_ref[...])
    cw = cw_ref[...]
    lane = lax.broadcasted_iota(jnp.int32, cw.shape, 1)
    c = jnp.sum(jnp.where(lane == e, cw, 0.0), axis=-1, keepdims=True)
    hid = _silu(a) * b * c
    acc_ref[...] += _dot(hid.astype(BF16), w2_ref[...])

    @pl.when(e == pl.num_programs(2) - 1)
    def _():
        o_ref[...] = x_ref[...] + gt_ref[...] * acc_ref[...]


def _moe_experts(x, g, sc, sh, cw, w1, w3, w2, gt, tm):
    B, L, _ = x.shape
    ne = w1.shape[0]
    return pl.pallas_call(
        _moe_experts_kernel,
        out_shape=jax.ShapeDtypeStruct((B, L, D_MODEL), F32),
        grid=(B, L // tm, ne),
        in_specs=[pl.BlockSpec((None, tm, D_MODEL), lambda b, i, e: (b, i, 0)),
                  _vec_spec(D_MODEL), _mod_spec(sc), _mod_spec(sh),
                  pl.BlockSpec((None, tm, LANES), lambda b, i, e: (b, i, 0)),
                  pl.BlockSpec((None, D_MODEL, MOE_F), lambda b, i, e: (e, 0, 0)),
                  pl.BlockSpec((None, D_MODEL, MOE_F), lambda b, i, e: (e, 0, 0)),
                  pl.BlockSpec((None, MOE_F, D_MODEL), lambda b, i, e: (e, 0, 0)),
                  _mod_spec(gt)],
        out_specs=pl.BlockSpec((None, tm, D_MODEL), lambda b, i, e: (b, i, 0)),
        scratch_shapes=[pltpu.VMEM((tm, D_MODEL), BF16), pltpu.VMEM((tm, D_MODEL), F32)],
        compiler_params=_cparams("parallel", "parallel", "arbitrary"),
        name="moe_experts",
    )(x, g, sc, sh, cw, w1, w3, w2, gt)


def _final_norm_kernel(x_ref, g_ref, o_ref):
    x = x_ref[...]
    ms = jnp.mean(x * x, axis=-1, keepdims=True)
    o_ref[...] = (x * lax.rsqrt(ms + RMS_EPS)) * g_ref[...]


def _final_norm(x, g, tm):
    B, L, _ = x.shape
    return pl.pallas_call(
        _final_norm_kernel,
        out_shape=jax.ShapeDtypeStruct((B, L, D_MODEL), F32),
        grid=(B, L // tm),
        in_specs=[pl.BlockSpec((None, tm, D_MODEL), lambda b, i: (b, i, 0)), _vec_spec(D_MODEL)],
        out_specs=pl.BlockSpec((None, tm, D_MODEL), lambda b, i: (b, i, 0)),
        compiler_params=_cparams("parallel", "parallel"),
        name="final_norm",
    )(x, g)


def _row(v):
    return v.reshape(1, -1)


def _prepare_weights(P):
    W = {}
    W["cv_pw1_w"] = P["cv_pw1_w"].astype(BF16)
    W["cv_pw2_w"] = P["cv_pw2_w"].astype(BF16)
    W["gdn_w_qkvz"] = P["gdn_w_qkvz"].astype(BF16)
    W["gdn_w_ba"] = jnp.pad(P["gdn_w_ba"], ((0, 0), (0, 0), (0, LANES - 4 * GDN_H)))
    a_neg = jnp.exp(P["gdn_a_log"].astype(F32))
    zeros = jnp.zeros_like(a_neg)
    a_cols = jnp.stack([zeros, a_neg], axis=2).reshape(a_neg.shape[0], 4 * GDN_H)
    dt_cols = jnp.stack([zeros, P["gdn_dt_bias"].astype(F32)], axis=2).reshape(a_neg.shape[0], 4 * GDN_H)
    W["gdn_a_row"] = jnp.pad(a_cols, ((0, 0), (0, LANES - 4 * GDN_H)))
    W["gdn_dt_row"] = jnp.pad(dt_cols, ((0, 0), (0, LANES - 4 * GDN_H)))
    W["gdn_w_o"] = P["gdn_w_o"].astype(BF16)
    W["na_w_qkv"] = P["na_w_qkv"].astype(BF16)
    W["na_w_o"] = P["na_w_o"].astype(BF16)
    W["na_rpb"] = P["na_rpb"]
    ne = MOE_G * MOE_E
    W["moe_w1"] = P["moe_w1"].reshape(DEPTH, ne, D_MODEL, MOE_F).astype(BF16)
    W["moe_w3"] = P["moe_w3"].reshape(DEPTH, ne, D_MODEL, MOE_F).astype(BF16)
    W["moe_w2"] = P["moe_w2"].reshape(DEPTH, ne, MOE_F, D_MODEL).astype(BF16)
    w_r = jnp.concatenate([P["moe_wg"], P["moe_we"]], axis=-1)
    n_r = w_r.shape[-1]
    W["moe_w_rt"] = jnp.pad(w_r.transpose(0, 2, 1), ((0, 0), (0, ROUTER_ROWS - n_r), (0, 0)))
    b_r = jnp.concatenate([P["moe_bg"], P["moe_be"]], axis=-1)
    W["moe_b_r"] = jnp.pad(b_r, ((0, 0), (0, ROUTER_ROWS - n_r)))[:, :, None]
    return W


def _trunk(x, mods, P, W, cache):
    B, L, _ = x.shape
    latent = cache is not None
    Bm = mods.shape[1]
    flat = Bm == 1
    tm_seq = min(L, 256)
    tm_tok = 512
    gdn_states, na_ks, na_vs = [], [], []

    def tok(a):
        return a.reshape(1, B * L, a.shape[-1]) if flat else a

    def seq(a):
        return a.reshape(B, L, a.shape[-1])

    for i in range(DEPTH):
        m = mods[i].reshape(Bm, 1, ADA_CHUNKS, D_MODEL)
        sh1, sc1, gt1, sh2, sc2, gt2 = (m[:, :, c] for c in range(ADA_CHUNKS))
        g1 = _row(P["norm1_g"][i])
        kind, j = i % N_MIXERS, i // N_MIXERS
        if kind == 0:
            u = _nm_glu(tok(x), g1, sc1, sh1, W["cv_pw1_w"][j], _row(P["cv_pw1_b"][j]), tm_tok, 512)
            x = _conv_tail(seq(u), P["cv_dw_w"][j], _row(P["cv_dw_b"][j]), _row(P["cv_ln_g"][j]),
                           _row(P["cv_ln_b"][j]), W["cv_pw2_w"][j], _row(P["cv_pw2_b"][j]),
                           seq(x), gt1, tm_seq)
        elif kind == 1:
            qkvz = _nm_linear(tok(x), g1, sc1, sh1, W["gdn_w_qkvz"][j], tm_tok, 512)
            ba = _nm_small(tok(x), g1, sc1, sh1, W["gdn_w_ba"][j], tm_tok)
            qkv = _gdn_prep(seq(qkvz), P["gdn_conv_w"][j], tm_seq)
            if latent:
                s0 = cache[0][:, j].astype(F32)
            else:
                s0 = jnp.zeros((B, 2, GDN_H, GDN_DK, GDN_DK), F32)
            a_row, dt_row = W["gdn_a_row"][j:j + 1], W["gdn_dt_row"][j:j + 1]
            o_f, s_f = _gdn_chunk(qkv, seq(ba), a_row, dt_row, s0[:, 0], rev=False)
            o_b, s_b = _gdn_chunk(qkv, seq(ba), a_row, dt_row, s0[:, 1], rev=True)
            gdn_states.append(jnp.stack([s_f, s_b], axis=1))
            x = _gdn_out(tok(o_f), tok(o_b), qkvz, _row(P["gdn_o_g"][j]), W["gdn_w_o"][j],
                         tok(x), gt1, tm_tok)
        else:
            qkv = _nm_linear(tok(x), g1, sc1, sh1, W["na_w_qkv"][j], tm_tok, 512)
            if latent:
                bias = _na_bias_table(W["na_rpb"][j], L // GRID_W)
                o = _na_attn(seq(qkv), cache[1][:, j], cache[2][:, j], bias)
            else:
                o, k_ctx, v_ctx = _ctx_attn(seq(qkv))
                na_ks.append(k_ctx)
                na_vs.append(v_ctx)
            x = _linear_res(tok(o), W["na_w_o"][j], tok(x), gt1, tm_tok)
        cw = _moe_router(tok(x), _row(P["norm2_g"][i]), sc2, sh2, W["moe_w_rt"][i],
                         W["moe_b_r"][i], tm_tok)
        x = _moe_experts(tok(x), _row(P["norm2_g"][i]), sc2, sh2, cw, W["moe_w1"][i], W["moe_w3"][i],
                         W["moe_w2"][i], gt2, tm_tok)
        x = seq(x)
    y = seq(_final_norm(tok(x), _row(P["final_norm_g"]), tm_tok))
    return y, gdn_states, na_ks, na_vs


def kernel(x_prompt, x_sample, state_gdn, cache_na_k, cache_na_v, c, c_ctx,
           ada_w, ada_b, norm1_g, norm2_g,
           cv_pw1_w, cv_pw1_b, cv_dw_w, cv_dw_b, cv_ln_g, cv_ln_b, cv_pw2_w, cv_pw2_b,
           gdn_w_qkvz, gdn_conv_w, gdn_w_ba, gdn_a_log, gdn_dt_bias, gdn_o_g, gdn_w_o,
           na_w_qkv, na_rpb, na_w_o,
           moe_wg, moe_bg, moe_we, moe_be, moe_w1, moe_w3, moe_w2,
           final_norm_g):
    P = dict(norm1_g=norm1_g, norm2_g=norm2_g,
             cv_pw1_w=cv_pw1_w, cv_pw1_b=cv_pw1_b, cv_dw_w=cv_dw_w, cv_dw_b=cv_dw_b,
             cv_ln_g=cv_ln_g, cv_ln_b=cv_ln_b, cv_pw2_w=cv_pw2_w, cv_pw2_b=cv_pw2_b,
             gdn_w_qkvz=gdn_w_qkvz, gdn_conv_w=gdn_conv_w, gdn_w_ba=gdn_w_ba, gdn_a_log=gdn_a_log,
             gdn_dt_bias=gdn_dt_bias, gdn_o_g=gdn_o_g, gdn_w_o=gdn_w_o,
             na_w_qkv=na_w_qkv, na_rpb=na_rpb, na_w_o=na_w_o,
             moe_wg=moe_wg, moe_bg=moe_bg, moe_we=moe_we, moe_be=moe_be,
             moe_w1=moe_w1, moe_w3=moe_w3, moe_w2=moe_w2, final_norm_g=final_norm_g)
    W = _prepare_weights(P)
    nb = c.shape[0]
    rows = 8 * ((nb + 1 + 7) // 8)
    cond = jnp.concatenate([c, c_ctx[None, :], jnp.zeros((rows - nb - 1, D_MODEL), F32)], axis=0)
    mods = _ada_all(cond, ada_w, ada_b)
    y_prompt, gdn_states, na_ks, na_vs = _trunk(x_prompt, mods[:, nb:nb + 1], P, W, None)
    y_sample, _, _, _ = _trunk(x_sample, mods[:, :nb], P, W, (state_gdn, cache_na_k, cache_na_v))
    return (y_prompt, y_sample, jnp.stack(gdn_states, axis=1),
            jnp.stack(na_ks, axis=1), jnp.stack(na_vs, axis=1))
```

```python
import functools

import numpy as np
import jax
import jax.numpy as jnp
from jax import lax
from jax.experimental import pallas as pl
from jax.experimental.pallas import tpu as pltpu

F32 = jnp.float32
BF16 = jnp.bfloat16

D_MODEL = 1024
DEPTH = 4
N_MIXERS = 3
ADA_CHUNKS = 6
RMS_EPS = 1e-6
LN_EPS = 1e-5
CONV_WIDTH = 31
CONV_HALO = 16
GDN_H = 8
GDN_DK = 128
GDN_CONV = 5
GDN_HALO = 8
GDN_BLOCK = 128
GDN_SUB = 16
NA_HEADS = 16
NA_DH = 64
NA_WIN_R = 8
NA_WIN_C = 16
NA_QROWS = 4
NA_KROWS = 12
GRID_W = 64
MOE_G = 4
MOE_E = 4
MOE_F = 256
MOE_TILE = 512
ROUTER_ROWS = 32
LANES = 128
SUBLANES = 8
NEG_BIG = -1e30

VMEM_LIMIT_BYTES = 48 * 1024 * 1024


def _cparams(*sem):
    return pltpu.CompilerParams(dimension_semantics=sem, vmem_limit_bytes=VMEM_LIMIT_BYTES)


def _sigmoid(x):
    return 1.0 / (1.0 + jnp.exp(-x))


def _silu(x):
    return x * _sigmoid(x)


def _norm_mod(x, g, sc, sh):
    ms = jnp.mean(x * x, axis=-1, keepdims=True)
    y = x * lax.rsqrt(ms + RMS_EPS)
    return (y * g) * (1.0 + sc) + sh


def _dot(a, b):
    return jnp.dot(a, b, preferred_element_type=F32)


def _dot_nt(a, b):
    return lax.dot_general(a, b, (((1,), (1,)), ((), ())), preferred_element_type=F32)


def _dot_f32(a, b):
    return jnp.dot(a, b, preferred_element_type=F32, precision=lax.Precision.HIGHEST)


def _split_bf16(w):
    hi = w.astype(BF16)
    lo = (w - hi.astype(F32)).astype(BF16)
    return hi, lo


def _mod_spec(mod):
    if mod.shape[0] == 1:
        return pl.BlockSpec((None, 1, D_MODEL), lambda b, *_: (0, 0, 0))
    return pl.BlockSpec((None, 1, D_MODEL), lambda b, *_: (b, 0, 0))


def _vec_spec(n):
    return pl.BlockSpec((1, n), lambda *_: (0, 0))


def _ada_kernel(c_ref, w_ref, b_ref, o_ref):
    s = _silu(c_ref[...]).astype(BF16)
    o_ref[...] = _dot(s, w_ref[...].astype(BF16)) + b_ref[...]


def _ada_all(cond, ada_w, ada_b, tn=1024):
    R = cond.shape[0]
    N = ada_w.shape[-1]
    return pl.pallas_call(
        _ada_kernel,
        out_shape=jax.ShapeDtypeStruct((DEPTH, R, N), F32),
        grid=(DEPTH, N // tn),
        in_specs=[pl.BlockSpec((R, D_MODEL), lambda l, j: (0, 0)),
                  pl.BlockSpec((None, D_MODEL, tn), lambda l, j: (l, 0, j)),
                  pl.BlockSpec((None, 1, tn), lambda l, j: (l, 0, j))],
        out_specs=pl.BlockSpec((None, R, tn), lambda l, j: (l, 0, j)),
        compiler_params=_cparams("parallel", "parallel"),
        name="ada",
    )(cond, ada_w, ada_b.reshape(DEPTH, 1, N))


def _nm_linear_kernel(x_ref, g_ref, sc_ref, sh_ref, w_ref, o_ref, h_ref):
    @pl.when(pl.program_id(2) == 0)
    def _():
        h_ref[...] = _norm_mod(x_ref[...], g_ref[...], sc_ref[...], sh_ref[...]).astype(BF16)

    o_ref[...] = _dot(h_ref[...], w_ref[...]).astype(o_ref.dtype)


def _nm_linear(x, g, sc, sh, w, tm, tn):
    B, L, _ = x.shape
    N = w.shape[1]
    return pl.pallas_call(
        _nm_linear_kernel,
        out_shape=jax.ShapeDtypeStruct((B, L, N), F32),
        grid=(B, L // tm, N // tn),
        in_specs=[pl.BlockSpec((None, tm, D_MODEL), lambda b, i, j: (b, i, 0)),
                  _vec_spec(D_MODEL), _mod_spec(sc), _mod_spec(sh),
                  pl.BlockSpec((D_MODEL, tn), lambda b, i, j: (0, j))],
        out_specs=pl.BlockSpec((None, tm, tn), lambda b, i, j: (b, i, j)),
        scratch_shapes=[pltpu.VMEM((tm, D_MODEL), BF16)],
        compiler_params=_cparams("parallel", "parallel", "arbitrary"),
        name="nm_linear",
    )(x, g, sc, sh, w)


def _nm_glu_kernel(x_ref, g_ref, sc_ref, sh_ref, wa_ref, wg_ref, ba_ref, bg_ref, o_ref, h_ref):
    @pl.when(pl.program_id(2) == 0)
    def _():
        h_ref[...] = _norm_mod(x_ref[...], g_ref[...], sc_ref[...], sh_ref[...]).astype(BF16)

    h = h_ref[...]
    a = _dot(h, wa_ref[...]) + ba_ref[...]
    gate = _dot(h, wg_ref[...]) + bg_ref[...]
    o_ref[...] = a * _sigmoid(gate)


def _nm_glu(x, g, sc, sh, w, b, tm, tn):
    B, L, _ = x.shape
    N = w.shape[1] // 2
    nj = N // tn
    return pl.pallas_call(
        _nm_glu_kernel,
        out_shape=jax.ShapeDtypeStruct((B, L, N), F32),
        grid=(B, L // tm, nj),
        in_specs=[pl.BlockSpec((None, tm, D_MODEL), lambda b_, i, j: (b_, i, 0)),
                  _vec_spec(D_MODEL), _mod_spec(sc), _mod_spec(sh),
                  pl.BlockSpec((D_MODEL, tn), lambda b_, i, j: (0, j)),
                  pl.BlockSpec((D_MODEL, tn), lambda b_, i, j: (0, j + nj)),
                  pl.BlockSpec((1, tn), lambda b_, i, j: (0, j)),
                  pl.BlockSpec((1, tn), lambda b_, i, j: (0, j + nj))],
        out_specs=pl.BlockSpec((None, tm, tn), lambda b_, i, j: (b_, i, j)),
        scratch_shapes=[pltpu.VMEM((tm, D_MODEL), BF16)],
        compiler_params=_cparams("parallel", "parallel", "arbitrary"),
        name="nm_glu",
    )(x, g, sc, sh, w, w, b, b)


def _nm_small_kernel(x_ref, g_ref, sc_ref, sh_ref, whi_ref, wlo_ref, o_ref):
    h = _norm_mod(x_ref[...], g_ref[...], sc_ref[...], sh_ref[...])
    h_hi, h_lo = _split_bf16(h)
    o_ref[...] = _dot(h_hi, whi_ref[...]) + _dot(h_hi, wlo_ref[...]) + _dot(h_lo, whi_ref[...])


def _nm_small(x, g, sc, sh, w, tm):
    B, L, _ = x.shape
    w_hi, w_lo = _split_bf16(w)
    return pl.pallas_call(
        _nm_small_kernel,
        out_shape=jax.ShapeDtypeStruct((B, L, LANES), F32),
        grid=(B, L // tm),
        in_specs=[pl.BlockSpec((None, tm, D_MODEL), lambda b, i: (b, i, 0)),
                  _vec_spec(D_MODEL), _mod_spec(sc), _mod_spec(sh),
                  pl.BlockSpec((D_MODEL, LANES), lambda b, i: (0, 0)),
                  pl.BlockSpec((D_MODEL, LANES), lambda b, i: (0, 0))],
        out_specs=pl.BlockSpec((None, tm, LANES), lambda b, i: (b, i, 0)),
        compiler_params=_cparams("parallel", "parallel"),
        name="nm_small",
    )(x, g, sc, sh, w_hi, w_lo)


def _linear_res_kernel(a_ref, w_ref, x_ref, gt_ref, o_ref):
    y = _dot(a_ref[...].astype(BF16), w_ref[...])
    o_ref[...] = x_ref[...] + gt_ref[...] * y


def _linear_res(a, w, x, gt, tm):
    B, L, K = a.shape
    return pl.pallas_call(
        _linear_res_kernel,
        out_shape=jax.ShapeDtypeStruct((B, L, D_MODEL), F32),
        grid=(B, L // tm),
        in_specs=[pl.BlockSpec((None, tm, K), lambda b, i: (b, i, 0)),
                  pl.BlockSpec((K, D_MODEL), lambda b, i: (0, 0)),
                  pl.BlockSpec((None, tm, D_MODEL), lambda b, i: (b, i, 0)),
                  _mod_spec(gt)],
        out_specs=pl.BlockSpec((None, tm, D_MODEL), lambda b, i: (b, i, 0)),
        compiler_params=_cparams("parallel", "parallel"),
        name="linear_res",
    )(a, w, x, gt)


def _conv_tail_kernel(u_ref, up_ref, un_ref, dww_ref, dwb_ref, lng_ref, lnb_ref, w2_ref, b2_ref,
                      x_ref, gt_ref, o_ref, buf_ref, sh_ref, cv_ref, *, tm, rc):
    i = pl.program_id(1)
    last = pl.num_programs(1) - 1
    buf_ref[0:CONV_HALO, :] = jnp.where(i > 0, up_ref[...], 0.0)
    buf_ref[CONV_HALO:CONV_HALO + tm, :] = u_ref[...]
    buf_ref[CONV_HALO + tm:, :] = jnp.where(i < last, un_ref[...], 0.0)
    off = CONV_HALO - CONV_WIDTH // 2
    nsh = sh_ref.shape[1]
    for b in range(SUBLANES):
        sh_ref[b] = buf_ref[b:b + nsh, :]
    for c in range(D_MODEL // LANES):
        cs = slice(c * LANES, (c + 1) * LANES)
        for r0 in range(0, tm, rc):
            acc = jnp.zeros((rc, LANES), F32)
            for k in range(CONV_WIDTH):
                a, b = divmod(off + k, SUBLANES)
                acc = acc + dww_ref[k:k + 1, cs] * sh_ref[b, r0 + a * SUBLANES:r0 + a * SUBLANES + rc, cs]
            cv_ref[r0:r0 + rc, cs] = acc + dwb_ref[:, cs]
    v = cv_ref[...]
    mu = jnp.mean(v, axis=-1, keepdims=True)
    vc = v - mu
    var = jnp.mean(vc * vc, axis=-1, keepdims=True)
    y = _silu(vc * lax.rsqrt(var + LN_EPS) * lng_ref[...] + lnb_ref[...])
    z = _dot(y.astype(BF16), w2_ref[...]) + b2_ref[...]
    o_ref[...] = x_ref[...] + gt_ref[...] * z


def _conv_tail(u, dw_w, dw_b, ln_g, ln_b, w2, b2, x, gt, tm):
    B, L, _ = u.shape
    hb = tm // CONV_HALO
    nh = L // CONV_HALO
    dww = jnp.pad(dw_w, ((0, 32 - CONV_WIDTH), (0, 0)))
    last_tap = CONV_HALO - CONV_WIDTH // 2 + CONV_WIDTH - 1
    kern = functools.partial(_conv_tail_kernel, tm=tm, rc=64)
    return pl.pallas_call(
        kern,
        out_shape=jax.ShapeDtypeStruct((B, L, D_MODEL), F32),
        grid=(B, L // tm),
        in_specs=[pl.BlockSpec((None, tm, D_MODEL), lambda b, i: (b, i, 0)),
                  pl.BlockSpec((None, CONV_HALO, D_MODEL),
                               lambda b, i: (b, jnp.maximum(i * hb - 1, 0), 0)),
                  pl.BlockSpec((None, CONV_HALO, D_MODEL),
                               lambda b, i: (b, jnp.minimum((i + 1) * hb, nh - 1), 0)),
                  pl.BlockSpec((32, D_MODEL), lambda b, i: (0, 0)),
                  _vec_spec(D_MODEL), _vec_spec(D_MODEL), _vec_spec(D_MODEL),
                  pl.BlockSpec((D_MODEL, D_MODEL), lambda b, i: (0, 0)),
                  _vec_spec(D_MODEL),
                  pl.BlockSpec((None, tm, D_MODEL), lambda b, i: (b, i, 0)),
                  _mod_spec(gt)],
        out_specs=pl.BlockSpec((None, tm, D_MODEL), lambda b, i: (b, i, 0)),
        scratch_shapes=[pltpu.VMEM((tm + 2 * CONV_HALO, D_MODEL), F32),
                        pltpu.VMEM((SUBLANES, tm + SUBLANES * (last_tap // SUBLANES), D_MODEL), F32),
                        pltpu.VMEM((tm, D_MODEL), F32)],
        compiler_params=_cparams("parallel", "parallel"),
        name="conv_tail",
    )(u, u, u, dww, dw_b, ln_g, ln_b, w2, b2, x, gt)


def _gdn_prep_kernel(u_ref, up_ref, un_ref, w_ref, o_ref, buf_ref, *, tm):
    i = pl.program_id(1)
    j = pl.program_id(2)
    last = pl.num_programs(1) - 1
    buf_ref[0:GDN_HALO, :] = jnp.where(i > 0, up_ref[...], 0.0)
    buf_ref[GDN_HALO:GDN_HALO + tm, :] = u_ref[...]
    buf_ref[GDN_HALO + tm:, :] = jnp.where(i < last, un_ref[...], 0.0)
    off = GDN_HALO - GDN_CONV // 2
    for h in range(GDN_H):
        cs = slice(h * GDN_DK, (h + 1) * GDN_DK)
        acc = jnp.zeros((tm, GDN_DK), F32)
        for k in range(GDN_CONV):
            acc = acc + w_ref[k:k + 1, cs] * buf_ref[off + k:off + k + tm, cs]
        y = _silu(acc)
        nrm = y * lax.rsqrt(jnp.sum(y * y, axis=-1, keepdims=True) + 1e-6)
        o_ref[:, cs] = jnp.where(j == 0, nrm * (GDN_DK ** -0.5), jnp.where(j == 1, nrm, y))


def _gdn_prep(qkvz, conv_w, tm):
    B, L, _ = qkvz.shape
    W = GDN_H * GDN_DK
    hb = tm // GDN_HALO
    nh = L // GDN_HALO
    cw = jnp.pad(conv_w, ((0, 8 - GDN_CONV), (0, 0)))
    kern = functools.partial(_gdn_prep_kernel, tm=tm)
    return pl.pallas_call(
        kern,
        out_shape=jax.ShapeDtypeStruct((B, L, 3 * W), F32),
        grid=(B, L // tm, 3),
        in_specs=[pl.BlockSpec((None, tm, W), lambda b, i, j: (b, i, j)),
                  pl.BlockSpec((None, GDN_HALO, W), lambda b, i, j: (b, jnp.maximum(i * hb - 1, 0), j)),
                  pl.BlockSpec((None, GDN_HALO, W),
                               lambda b, i, j: (b, jnp.minimum((i + 1) * hb, nh - 1), j)),
                  pl.BlockSpec((8, W), lambda b, i, j: (0, j))],
        out_specs=pl.BlockSpec((None, tm, W), lambda b, i, j: (b, i, j)),
        scratch_shapes=[pltpu.VMEM((tm + 2 * GDN_HALO, W), F32)],
        compiler_params=_cparams("parallel", "parallel", "arbitrary"),
        name="gdn_prep",
    )(qkvz, qkvz, qkvz, cw)


def _split3(x):
    hi = x.astype(BF16)
    r1 = x - hi.astype(F32)
    mid = r1.astype(BF16)
    lo = (r1 - mid.astype(F32)).astype(BF16)
    return hi, mid, lo


def _mm3(a, b):
    lhs = jnp.concatenate([a[0], a[1]], axis=1)
    rhs = jnp.concatenate([b[0], b[0]], axis=0)
    return _dot(lhs, rhs) + _dot(a[0], b[1])


def _gdn_chunk_kernel(q_ref, k_ref, v_ref, ba_ref, a_ref, dtb_ref, s0_ref, o_ref, sfin_ref, s_ref,
                      *, rev, col0):
    n = pl.program_id(1)
    C = GDN_BLOCK

    @pl.when(n == 0)
    def _():
        s_ref[...] = s0_ref[...]

    ri = lax.broadcasted_iota(jnp.int32, (C, C), 0)
    ci = lax.broadcasted_iota(jnp.int32, (C, C), 1)
    if rev:
        incl, strict = ri <= ci, ri < ci
    else:
        incl, strict = ri >= ci, ri > ci
    blk = (ri // GDN_SUB) == (ci // GDN_SUB)
    eye_f = jnp.where(ri == ci, 1.0, 0.0)

    ba = ba_ref[...]
    beta_all = _sigmoid(ba)
    xa = ba + dtb_ref[...]
    softplus = jnp.maximum(xa, 0.0) + jnp.log(1.0 + jnp.exp(-jnp.abs(xa)))
    g_all = -a_ref[...] * softplus
    tri = jnp.where(incl, 1.0, 0.0).astype(BF16)
    g_hi, g_mid, g_lo = _split3(g_all)
    cum_all = _dot(tri, g_hi) + _dot(tri, g_mid) + _dot(tri, g_lo)
    cum_t = cum_all.T
    tot_row = cum_all[0:1, :] if rev else cum_all[C - 1:C, :]
    ecum_all = jnp.exp(cum_all)
    etail_all = jnp.exp(tot_row - cum_all)
    etot_row = jnp.exp(tot_row)

    HS = range(GDN_H)
    cs = [slice(h * GDN_DK, (h + 1) * GDN_DK) for h in HS]
    cg = [col0 + GDN_H + h for h in HS]
    k = [k_ref[:, cs[h]] for h in HS]
    kbf = [k[h].astype(BF16) for h in HS]
    beta = [beta_all[:, col0 + h:col0 + h + 1] for h in HS]
    kb = [k[h] * beta[h] for h in HS]
    decay = [jnp.where(incl, jnp.exp(jnp.where(
        incl, cum_all[:, cg[h]:cg[h] + 1] - cum_t[cg[h]:cg[h] + 1, :], 0.0)), 0.0) for h in HS]
    nm = [jnp.where(strict, _dot_nt(kb[h].astype(BF16), kbf[h]) * decay[h], 0.0) for h in HS]
    nd = [jnp.where(blk, nm[h], 0.0) for h in HS]
    nd_s = [_split_bf16(nd[h]) for h in HS]
    nd2_s = [_split_bf16(_mm3(nd_s[h], nd_s[h])) for h in HS]
    nd4_s = [_split_bf16(_mm3(nd2_s[h], nd2_s[h])) for h in HS]
    nd8_s = [_split_bf16(_mm3(nd4_s[h], nd4_s[h])) for h in HS]
    t = [eye_f - nd[h] for h in HS]
    t = [t[h] + _mm3(_split_bf16(t[h]), nd2_s[h]) for h in HS]
    t = [t[h] + _mm3(_split_bf16(t[h]), nd4_s[h]) for h in HS]
    t = [t[h] + _mm3(_split_bf16(t[h]), nd8_s[h]) for h in HS]
    t_s = [_split_bf16(t[h]) for h in HS]
    m_s = [_split_bf16(_mm3(t_s[h], _split_bf16(nm[h] - nd[h]))) for h in HS]
    m2_s = [_split_bf16(_mm3(m_s[h], m_s[h])) for h in HS]
    m4_s = [_split_bf16(_mm3(m2_s[h], m2_s[h])) for h in HS]
    a_inv = [t[h] + _mm3(m4_s[h], t_s[h]) for h in HS]
    a_inv = [a_inv[h] + _mm3(m2_s[h], _split_bf16(a_inv[h])) for h in HS]
    a_inv = [a_inv[h] - _mm3(m_s[h], _split_bf16(a_inv[h])) for h in HS]
    e_cum = [ecum_all[:, cg[h]:cg[h] + 1] for h in HS]
    rhs = [jnp.concatenate([v_ref[:, cs[h]] * beta[h], kb[h] * e_cum[h]], axis=-1) for h in HS]
    sol = [_mm3(_split_bf16(a_inv[h]), _split_bf16(rhs[h])) for h in HS]
    q = [q_ref[:, cs[h]] for h in HS]
    qk = [jnp.where(incl, _dot_nt(q[h].astype(BF16), kbf[h]) * decay[h], 0.0).astype(BF16)
          for h in HS]
    q_g = [(q[h] * e_cum[h]).astype(BF16) for h in HS]
    k_tail_t = [(k[h] * etail_all[:, cg[h]:cg[h] + 1]).T.astype(BF16) for h in HS]
    s = [s_ref[h] for h in HS]
    sb = [s[h].astype(BF16) for h in HS]
    vb = [(sol[h][:, :GDN_DK] - _dot(sol[h][:, GDN_DK:].astype(BF16), sb[h])).astype(BF16)
          for h in HS]
    for h in HS:
        o_ref[:, cs[h]] = _dot(q_g[h], sb[h]) + _dot(qk[h], vb[h])
    for h in HS:
        s_ref[h] = s[h] * etot_row[:, cg[h]:cg[h] + 1] + _dot(k_tail_t[h], vb[h])

    @pl.when(n == pl.num_programs(1) - 1)
    def _():
        sfin_ref[...] = s_ref[...]


def _gdn_chunk(qkv, ba, a_row, dtb_row, s0, rev):
    B, L, _ = qkv.shape
    W = GDN_H * GDN_DK
    nc = L // GDN_BLOCK
    cidx = (lambda n: nc - 1 - n) if rev else (lambda n: n)
    kern = functools.partial(_gdn_chunk_kernel, rev=rev, col0=2 * GDN_H if rev else 0)
    return pl.pallas_call(
        kern,
        out_shape=(jax.ShapeDtypeStruct((B, L, W), F32),
                   jax.ShapeDtypeStruct((B, GDN_H, GDN_DK, GDN_DK), F32)),
        grid=(B, nc),
        in_specs=[pl.BlockSpec((None, GDN_BLOCK, W), lambda b, n: (b, cidx(n), 0)),
                  pl.BlockSpec((None, GDN_BLOCK, W), lambda b, n: (b, cidx(n), 1)),
                  pl.BlockSpec((None, GDN_BLOCK, W), lambda b, n: (b, cidx(n), 2)),
                  pl.BlockSpec((None, GDN_BLOCK, LANES), lambda b, n: (b, cidx(n), 0)),
                  _vec_spec(LANES), _vec_spec(LANES),
                  pl.BlockSpec((None, GDN_H, GDN_DK, GDN_DK), lambda b, n: (b, 0, 0, 0))],
        out_specs=(pl.BlockSpec((None, GDN_BLOCK, W), lambda b, n: (b, cidx(n), 0)),
                   pl.BlockSpec((None, GDN_H, GDN_DK, GDN_DK), lambda b, n: (b, 0, 0, 0))),
        scratch_shapes=[pltpu.VMEM((GDN_H, GDN_DK, GDN_DK), F32)],
        compiler_params=_cparams("parallel", "arbitrary"),
        name="gdn_chunk_rev" if rev else "gdn_chunk_fwd",
    )(qkv, qkv, qkv, ba, a_row, dtb_row, s0)


def _gdn_out_kernel(of_ref, ob_ref, z_ref, og_ref, w_ref, x_ref, gt_ref, o_ref, y_ref):
    o = of_ref[...] + ob_ref[...]
    z = z_ref[...]
    for h in range(GDN_H):
        cs = slice(h * GDN_DK, (h + 1) * GDN_DK)
        oh = o[:, cs]
        ms = jnp.mean(oh * oh, axis=-1, keepdims=True)
        y = (oh * lax.rsqrt(ms + RMS_EPS)) * og_ref[...]
        y_ref[:, cs] = (y * _silu(z[:, cs])).astype(BF16)
    o_ref[...] = x_ref[...] + gt_ref[...] * _dot(y_ref[...], w_ref[...])


def _gdn_out(o_f, o_b, qkvz, o_g, w_o, x, gt, tm):
    B, L, W = o_f.shape
    return pl.pallas_call(
        _gdn_out_kernel,
        out_shape=jax.ShapeDtypeStruct((B, L, D_MODEL), F32),
        grid=(B, L // tm),
        in_specs=[pl.BlockSpec((None, tm, W), lambda b, i: (b, i, 0)),
                  pl.BlockSpec((None, tm, W), lambda b, i: (b, i, 0)),
                  pl.BlockSpec((None, tm, W), lambda b, i: (b, i, 3)),
                  _vec_spec(GDN_DK),
                  pl.BlockSpec((W, D_MODEL), lambda b, i: (0, 0)),
                  pl.BlockSpec((None, tm, D_MODEL), lambda b, i: (b, i, 0)),
                  _mod_spec(gt)],
        out_specs=pl.BlockSpec((None, tm, D_MODEL), lambda b, i: (b, i, 0)),
        scratch_shapes=[pltpu.VMEM((tm, W), BF16)],
        compiler_params=_cparams("parallel", "parallel"),
        name="gdn_out",
    )(o_f, o_b, qkvz, o_g, w_o, x, gt)


def _ctx_attn_kernel(q_ref, k_ref, v_ref, o_ref, ko_ref, vo_ref):
    scale = NA_DH ** -0.5
    for hh in range(2):
        cs = slice(hh * NA_DH, (hh + 1) * NA_DH)
        q = q_ref[:, cs]
        k = k_ref[:, cs]
        v = v_ref[:, cs]
        s = _dot_nt(q.astype(BF16), k.astype(BF16)) * scale
        p = jnp.exp(s - jnp.max(s, axis=-1, keepdims=True))
        p = p / jnp.sum(p, axis=-1, keepdims=True)
        o_ref[:, cs] = _dot(p.astype(BF16), v.astype(BF16))
        ko_ref[hh] = k
        vo_ref[hh] = v


def _ctx_attn(qkv):
    B, L, _ = qkv.shape
    hp = NA_HEADS // 2
    kv_shape = jax.ShapeDtypeStruct((B, NA_HEADS, L, NA_DH), F32)
    return pl.pallas_call(
        _ctx_attn_kernel,
        out_shape=(jax.ShapeDtypeStruct((B, L, D_MODEL), F32), kv_shape, kv_shape),
        grid=(B, hp),
        in_specs=[pl.BlockSpec((None, L, 2 * NA_DH), lambda b, p: (b, 0, p)),
                  pl.BlockSpec((None, L, 2 * NA_DH), lambda b, p: (b, 0, hp + p)),
                  pl.BlockSpec((None, L, 2 * NA_DH), lambda b, p: (b, 0, 2 * hp + p))],
        out_specs=(pl.BlockSpec((None, L, 2 * NA_DH), lambda b, p: (b, 0, p)),
                   pl.BlockSpec((None, 2, L, NA_DH), lambda b, p: (b, p, 0, 0)),
                   pl.BlockSpec((None, 2, L, NA_DH), lambda b, p: (b, p, 0, 0))),
        compiler_params=_cparams("parallel", "parallel"),
        name="ctx_attn",
    )(qkv, qkv, qkv)


def _na_geometry(rows):
    nblk = rows // NA_QROWS
    kr = min(NA_WIN_R, rows)
    variants, var_of_block, kstart = [], [], []
    for b in range(nblk):
        r0 = b * NA_QROWS
        ks = int(np.clip(r0 - NA_WIN_R // 2, 0, rows - NA_KROWS))
        qr = r0 + np.arange(NA_QROWS)
        rs = np.clip(qr - kr // 2, 0, rows - kr)
        key_row = ks + np.arange(NA_KROWS)
        ok = (key_row[None, :] >= rs[:, None]) & (key_row[None, :] < rs[:, None] + kr)
        dr = np.where(ok, key_row[None, :] - qr[:, None] + NA_WIN_R - 1, 0)
        geo = (ok.tobytes(), dr.tobytes())
        if geo not in [g for g, _, _ in variants]:
            variants.append((geo, ok, dr))
        var_of_block.append([g for g, _, _ in variants].index(geo))
        kstart.append(ks)
    return [(ok, dr) for _, ok, dr in variants], var_of_block, kstart


def _na_attn_kernel(q_ref, k_ref, v_ref, kc_ref, vc_ref, bias_ref, o_ref, *, rows):
    scale = NA_DH ** -0.5
    nblk = rows // NA_QROWS
    nq = NA_QROWS * GRID_W
    nk = NA_KROWS * GRID_W
    kc = [kc_ref[hh].astype(BF16) for hh in range(2)]
    vc = [vc_ref[hh].astype(BF16) for hh in range(2)]

    def body(blk, carry):
        ks = jnp.clip(blk * NA_QROWS - NA_WIN_R // 2, 0, rows - NA_KROWS)
        var = jnp.where(blk == 0, 0, jnp.where(blk == nblk - 1, 2, 1))
        q0 = pl.multiple_of(blk * nq, nq)
        k0 = pl.multiple_of(ks * GRID_W, GRID_W)
        for hh in range(2):
            cs = slice(hh * NA_DH, (hh + 1) * NA_DH)
            q = (q_ref[pl.ds(q0, nq), cs] * scale).astype(BF16)
            kw = k_ref[pl.ds(k0, nk), cs].astype(BF16)
            vw = v_ref[pl.ds(k0, nk), cs].astype(BF16)
            s_loc = _dot_nt(q, kw) + bias_ref[hh, var]
            s_ctx = _dot_nt(q, kc[hh])
            m = jnp.maximum(jnp.max(s_loc, axis=-1, keepdims=True),
                            jnp.max(s_ctx, axis=-1, keepdims=True))
            p_loc = jnp.exp(s_loc - m)
            p_ctx = jnp.exp(s_ctx - m)
            den = jnp.sum(p_loc, axis=-1, keepdims=True) + jnp.sum(p_ctx, axis=-1, keepdims=True)
            o = _dot(p_loc.astype(BF16), vw) + _dot(p_ctx.astype(BF16), vc[hh])
            o_ref[pl.ds(q0, nq), cs] = o / den
        return carry

    lax.fori_loop(0, nblk, body, 0)


def _na_attn(qkv, k_ctx, v_ctx, bias):
    B, L, _ = qkv.shape
    P = k_ctx.shape[2]
    rows = L // GRID_W
    hp = NA_HEADS // 2
    kern = functools.partial(_na_attn_kernel, rows=rows)
    return pl.pallas_call(
        kern,
        out_shape=jax.ShapeDtypeStruct((B, L, D_MODEL), F32),
        grid=(B, hp),
        in_specs=[pl.BlockSpec((None, L, 2 * NA_DH), lambda b, p: (b, 0, p)),
                  pl.BlockSpec((None, L, 2 * NA_DH), lambda b, p: (b, 0, hp + p)),
                  pl.BlockSpec((None, L, 2 * NA_DH), lambda b, p: (b, 0, 2 * hp + p)),
                  pl.BlockSpec((None, 2, P, NA_DH), lambda b, p: (b, p, 0, 0)),
                  pl.BlockSpec((None, 2, P, NA_DH), lambda b, p: (b, p, 0, 0)),
                  pl.BlockSpec((2,) + bias.shape[1:], lambda b, p: (p, 0, 0, 0))],
        out_specs=pl.BlockSpec((None, L, 2 * NA_DH), lambda b, p: (b, 0, p)),
        compiler_params=_cparams("parallel", "parallel"),
        name="na_attn",
    )(qkv, qkv, qkv, k_ctx, v_ctx, bias)


def _na_bias_table(rpb, rows):
    variants, var_of_block, kstart = _na_geometry(rows)
    nblk = rows // NA_QROWS
    expect = [0] + [1] * (nblk - 2) + [2]
    assert rows % NA_QROWS == 0 and rows >= NA_KROWS and var_of_block == expect, (rows, var_of_block)
    assert all(kstart[b] == int(np.clip(b * NA_QROWS - NA_WIN_R // 2, 0, rows - NA_KROWS))
               for b in range(nblk))
    qcol = np.arange(GRID_W)
    kcol = np.arange(GRID_W)
    cstart = np.clip(qcol - NA_WIN_C // 2, 0, GRID_W - NA_WIN_C)
    col_ok = (kcol[None, :] >= cstart[:, None]) & (kcol[None, :] < cstart[:, None] + NA_WIN_C)
    dc = np.clip(kcol[None, :] - qcol[:, None], -(NA_WIN_C - 1), NA_WIN_C - 1) + NA_WIN_C - 1
    plane = jnp.where(col_ok[None, None], rpb[:, :, dc], NEG_BIG)
    masked = jnp.full((rpb.shape[0], GRID_W, GRID_W), NEG_BIG, rpb.dtype)
    tabs = []
    for row_ok, dr in variants:
        q_rows = [jnp.concatenate([plane[:, int(dr[a, c])] if row_ok[a, c] else masked
                                   for c in range(NA_KROWS)], axis=-1) for a in range(NA_QROWS)]
        tabs.append(jnp.concatenate(q_rows, axis=1))
    return jnp.stack(tabs, axis=1)


def _moe_router_kernel(x_ref, g_ref, sc_ref, sh_ref, whi_ref, wlo_ref, br_ref,
                       h_out_ref, gid_ref, w4_ref):
    h = _norm_mod(x_ref[...], g_ref[...], sc_ref[...], sh_ref[...])
    h_hi, h_lo = _split_bf16(h)
    lg = (_dot_nt(whi_ref[...], h_hi) + _dot_nt(wlo_ref[...], h_hi) + _dot_nt(whi_ref[...], h_lo)
          + br_ref[...])
    tm = lg.shape[1]

    def first_max(vals):
        mx = functools.reduce(jnp.maximum, vals)
        taken = jnp.zeros((1, tm), jnp.bool_)
        hot = []
        for v in vals:
            hit = jnp.logical_and(v == mx, jnp.logical_not(taken))
            taken = jnp.logical_or(taken, hit)
            hot.append(hit)
        return mx, hot

    gl = [lg[g:g + 1, :] for g in range(MOE_G)]
    gmax, g_hit = first_max(gl)
    gsum = functools.reduce(lambda a, b: a + b, [jnp.exp(v - gmax) for v in gl])
    g_w = 1.0 / gsum
    g_hot = [jnp.where(hit, 1.0, 0.0) for hit in g_hit]
    el = []
    for e in range(MOE_E):
        acc = jnp.zeros((1, tm), F32)
        for g in range(MOE_G):
            r = MOE_G + g * MOE_E + e
            acc = acc + g_hot[g] * lg[r:r + 1, :]
        el.append(acc)
    emax = functools.reduce(jnp.maximum, el)
    ex = [jnp.exp(v - emax) for v in el]
    esum = functools.reduce(lambda a, b: a + b, ex)
    pe = [v / esum for v in ex]
    m1, hot1 = first_max(pe)
    m2, hot2 = first_max([jnp.where(hot1[e], -1.0, pe[e]) for e in range(MOE_E)])
    den = m1 + m2
    w_e = [(jnp.where(hot1[e], m1, 0.0) + jnp.where(hot2[e], m2, 0.0)) / den * g_w
           for e in range(MOE_E)]
    h_out_ref[...] = h
    gid_ref[...] = functools.reduce(lambda a, b: a + b,
                                    [jnp.where(g_hit[g], g, 0) for g in range(MOE_G)])
    w4_ref[...] = jnp.concatenate(w_e, axis=0)


def _moe_router(x, g, sc, sh, w_rt, b_r, tm):
    B, L, _ = x.shape
    w_hi, w_lo = _split_bf16(w_rt)
    return pl.pallas_call(
        _moe_router_kernel,
        out_shape=(jax.ShapeDtypeStruct((B, L, D_MODEL), F32),
                   jax.ShapeDtypeStruct((B, 1, L), jnp.int32),
                   jax.ShapeDtypeStruct((B, MOE_E, L), F32)),
        grid=(B, L // tm),
        in_specs=[pl.BlockSpec((None, tm, D_MODEL), lambda b, i: (b, i, 0)),
                  _vec_spec(D_MODEL), _mod_spec(sc), _mod_spec(sh),
                  pl.BlockSpec((ROUTER_ROWS, D_MODEL), lambda b, i: (0, 0)),
                  pl.BlockSpec((ROUTER_ROWS, D_MODEL), lambda b, i: (0, 0)),
                  pl.BlockSpec((ROUTER_ROWS, 1), lambda b, i: (0, 0))],
        out_specs=(pl.BlockSpec((None, tm, D_MODEL), lambda b, i: (b, i, 0)),
                   pl.BlockSpec((None, 1, tm), lambda b, i: (b, 0, i)),
                   pl.BlockSpec((None, MOE_E, tm), lambda b, i: (b, 0, i))),
        compiler_params=_cparams("parallel", "parallel"),
        name="moe_router",
    )(x, g, sc, sh, w_hi, w_lo, b_r)


def _moe_plan(gid, w4, tm):
    T = gid.shape[0]
    n_tiles = T // tm + MOE_G
    order = jnp.argsort(gid, stable=True).astype(jnp.int32)
    counts = jnp.sum((gid[:, None] == jnp.arange(MOE_G)[None, :]).astype(jnp.int32), axis=0)
    padded = ((counts + tm - 1) // tm) * tm
    seg_end = jnp.cumsum(padded)
    seg_start = seg_end - padded
    first = jnp.cumsum(counts) - counts
    p = jnp.arange(n_tiles * tm, dtype=jnp.int32)
    g_of_p = jnp.minimum(jnp.searchsorted(seg_end, p, side="right"), MOE_G - 1).astype(jnp.int32)
    local = p - seg_start[g_of_p]
    valid = local < counts[g_of_p]
    tok = order[jnp.clip(first[g_of_p] + local, 0, T - 1)]
    src = jnp.where(valid, tok, 0)
    dst = jnp.where(valid, tok, T + p % (2 * tm))
    cw = jnp.where(valid[:, None], w4[tok], 0.0)
    cw = jnp.pad(cw, ((0, 0), (0, SUBLANES - MOE_E)))
    tile_gid = g_of_p[::tm]
    tile_cnt = jnp.sum(valid.reshape(n_tiles, tm).astype(jnp.int32), axis=1)
    return (src.reshape(n_tiles, 1, tm), dst.reshape(n_tiles, 1, tm), cw, tile_gid, tile_cnt)


def _moe_sparse_kernel(tgid_ref, tcnt_ref, src_ref, srcn_ref, dst_ref, h_hbm, cw_ref, w1_ref, w3_ref,
                       w2_ref, y_hbm, hbuf, ybuf, gsem, ssem, *, tm):
    i = pl.program_id(0)
    n = pl.num_programs(0)
    slot = lax.rem(i, 2)
    nxt = jnp.minimum(i + 1, n - 1)
    active = tcnt_ref[i] > 0
    next_active = jnp.logical_and(i + 1 < n, tcnt_ref[nxt] > 0)

    def start_gather(idx_ref, s):
        def body(r, carry):
            pltpu.make_async_copy(h_hbm.at[pl.ds(idx_ref[0, r], 1)], hbuf.at[s, pl.ds(r, 1)],
                                  gsem.at[s]).start()
            return carry
        lax.fori_loop(0, tm, body, 0, unroll=8)

    def wait_gather(s):
        pltpu.make_async_copy(h_hbm.at[pl.ds(0, tm)], hbuf.at[s], gsem.at[s]).wait()

    def wait_scatter(s):
        pltpu.make_async_copy(ybuf.at[s], y_hbm.at[pl.ds(0, tm)], ssem.at[s]).wait()

    @pl.when(i == 0)
    def _():
        ybuf[...] = jnp.zeros_like(ybuf)
        n_tok = y_hbm.shape[0] - 2 * tm
        for s in range(2):
            cp = pltpu.make_async_copy(ybuf.at[s], y_hbm.at[pl.ds(n_tok + s * tm, tm)], ssem.at[s])
            cp.start()
            cp.wait()

    @pl.when(jnp.logical_and(i == 0, active))
    def _():
        start_gather(src_ref, 0)

    @pl.when(next_active)
    def _():
        start_gather(srcn_ref, 1 - slot)

    @pl.when(active)
    def _():
        wait_gather(slot)

        @pl.when(i >= 2)
        def _():
            wait_scatter(slot)

        h = hbuf[slot].astype(BF16)
        cw = cw_ref[...]
        acc = jnp.zeros((tm, D_MODEL), F32)
        for e in range(MOE_E):
            a = _dot(h, w1_ref[e])
            b = _dot(h, w3_ref[e])
            hid = _silu(a) * b * cw[:, e:e + 1]
            acc = acc + _dot(hid.astype(BF16), w2_ref[e])
        ybuf[slot] = acc

        def body(r, carry):
            pltpu.make_async_copy(ybuf.at[slot, pl.ds(r, 1)], y_hbm.at[pl.ds(dst_ref[0, r], 1)],
                                  ssem.at[slot]).start()
            return carry
        lax.fori_loop(0, tm, body, 0, unroll=8)

        @pl.when(jnp.logical_not(next_active))
        def _():
            wait_scatter(slot)

            @pl.when(i >= 1)
            def _():
                wait_scatter(1 - slot)


def _moe_sparse(h, plan, w1, w3, w2, tm):
    T = h.shape[0]
    src, dst, cw, tile_gid, tile_cnt = plan
    n_tiles = src.shape[0]
    kern = functools.partial(_moe_sparse_kernel, tm=tm)
    smem_row = lambda f: pl.BlockSpec((None, 1, tm), f, memory_space=pltpu.SMEM)
    grid_spec = pltpu.PrefetchScalarGridSpec(
        num_scalar_prefetch=2,
        grid=(n_tiles,),
        in_specs=[smem_row(lambda i, tg, tc: (i, 0, 0)),
                  smem_row(lambda i, tg, tc: (jnp.minimum(i + 1, n_tiles - 1), 0, 0)),
                  smem_row(lambda i, tg, tc: (i, 0, 0)),
                  pl.BlockSpec(memory_space=pl.ANY),
                  pl.BlockSpec((tm, SUBLANES), lambda i, tg, tc: (i, 0)),
                  pl.BlockSpec((None, MOE_E, D_MODEL, MOE_F), lambda i, tg, tc: (tg[i], 0, 0, 0)),
                  pl.BlockSpec((None, MOE_E, D_MODEL, MOE_F), lambda i, tg, tc: (tg[i], 0, 0, 0)),
                  pl.BlockSpec((None, MOE_E, MOE_F, D_MODEL), lambda i, tg, tc: (tg[i], 0, 0, 0))],
        out_specs=pl.BlockSpec(memory_space=pl.ANY),
        scratch_shapes=[pltpu.VMEM((2, tm, D_MODEL), F32), pltpu.VMEM((2, tm, D_MODEL), F32),
                        pltpu.SemaphoreType.DMA((2,)), pltpu.SemaphoreType.DMA((2,))])
    return pl.pallas_call(
        kern,
        out_shape=jax.ShapeDtypeStruct((T + 2 * tm, D_MODEL), F32),
        grid_spec=grid_spec,
        compiler_params=_cparams("arbitrary"),
        name="moe_sparse",
    )(tile_gid, tile_cnt, src, src, dst, h, cw, w1, w3, w2)


def _moe_combine_kernel(x_ref, y_ref, gt_ref, o_ref):
    o_ref[...] = x_ref[...] + gt_ref[...] * y_ref[...]


def _moe_combine(x, y, gt, tm):
    B, L, _ = x.shape
    nb = L // tm
    return pl.pallas_call(
        _moe_combine_kernel,
        out_shape=jax.ShapeDtypeStruct((B, L, D_MODEL), F32),
        grid=(B, nb),
        in_specs=[pl.BlockSpec((None, tm, D_MODEL), lambda b, i: (b, i, 0)),
                  pl.BlockSpec((tm, D_MODEL), lambda b, i: (b * nb + i, 0)),
                  _mod_spec(gt)],
        out_specs=pl.BlockSpec((None, tm, D_MODEL), lambda b, i: (b, i, 0)),
        compiler_params=_cparams("parallel", "parallel"),
        name="moe_combine",
    )(x, y, gt)


def _final_norm_kernel(x_ref, g_ref, o_ref):
    x = x_ref[...]
    ms = jnp.mean(x * x, axis=-1, keepdims=True)
    o_ref[...] = (x * lax.rsqrt(ms + RMS_EPS)) * g_ref[...]


def _final_norm(x, g, tm):
    B, L, _ = x.shape
    return pl.pallas_call(
        _final_norm_kernel,
        out_shape=jax.ShapeDtypeStruct((B, L, D_MODEL), F32),
        grid=(B, L // tm),
        in_specs=[pl.BlockSpec((None, tm, D_MODEL), lambda b, i: (b, i, 0)), _vec_spec(D_MODEL)],
        out_specs=pl.BlockSpec((None, tm, D_MODEL), lambda b, i: (b, i, 0)),
        compiler_params=_cparams("parallel", "parallel"),
        name="final_norm",
    )(x, g)


def _row(v):
    return v.reshape(1, -1)


def _prepare_weights(P):
    W = {}
    W["cv_pw1_w"] = P["cv_pw1_w"].astype(BF16)
    W["cv_pw2_w"] = P["cv_pw2_w"].astype(BF16)
    W["gdn_w_qkvz"] = P["gdn_w_qkvz"].astype(BF16)
    W["gdn_w_ba"] = jnp.pad(P["gdn_w_ba"], ((0, 0), (0, 0), (0, LANES - 4 * GDN_H)))
    a_neg = jnp.exp(P["gdn_a_log"].astype(F32))
    zeros = jnp.zeros_like(a_neg)
    a_cols = jnp.stack([zeros, a_neg], axis=2).reshape(a_neg.shape[0], 4 * GDN_H)
    dt_cols = jnp.stack([zeros, P["gdn_dt_bias"].astype(F32)], axis=2).reshape(a_neg.shape[0], 4 * GDN_H)
    W["gdn_a_row"] = jnp.pad(a_cols, ((0, 0), (0, LANES - 4 * GDN_H)))
    W["gdn_dt_row"] = jnp.pad(dt_cols, ((0, 0), (0, LANES - 4 * GDN_H)))
    W["gdn_w_o"] = P["gdn_w_o"].astype(BF16)
    W["na_w_qkv"] = P["na_w_qkv"].astype(BF16)
    W["na_w_o"] = P["na_w_o"].astype(BF16)
    W["na_rpb"] = P["na_rpb"]
    W["moe_w1"] = P["moe_w1"].astype(BF16)
    W["moe_w3"] = P["moe_w3"].astype(BF16)
    W["moe_w2"] = P["moe_w2"].astype(BF16)
    w_r = jnp.concatenate([P["moe_wg"], P["moe_we"]], axis=-1)
    n_r = w_r.shape[-1]
    W["moe_w_rt"] = jnp.pad(w_r.transpose(0, 2, 1), ((0, 0), (0, ROUTER_ROWS - n_r), (0, 0)))
    b_r = jnp.concatenate([P["moe_bg"], P["moe_be"]], axis=-1)
    W["moe_b_r"] = jnp.pad(b_r, ((0, 0), (0, ROUTER_ROWS - n_r)))[:, :, None]
    return W


def _trunk(x, mods, P, W, cache):
    B, L, _ = x.shape
    latent = cache is not None
    Bm = mods.shape[1]
    flat = Bm == 1
    tm_seq = min(L, 256)
    tm_tok = 512
    gdn_states, na_ks, na_vs = [], [], []

    def tok(a):
        return a.reshape(1, B * L, a.shape[-1]) if flat else a

    def seq(a):
        return a.reshape(B, L, a.shape[-1])

    for i in range(DEPTH):
        m = mods[i].reshape(Bm, 1, ADA_CHUNKS, D_MODEL)
        sh1, sc1, gt1, sh2, sc2, gt2 = (m[:, :, c] for c in range(ADA_CHUNKS))
        g1 = _row(P["norm1_g"][i])
        kind, j = i % N_MIXERS, i // N_MIXERS
        if kind == 0:
            u = _nm_glu(tok(x), g1, sc1, sh1, W["cv_pw1_w"][j], _row(P["cv_pw1_b"][j]), tm_tok, 512)
            x = _conv_tail(seq(u), P["cv_dw_w"][j], _row(P["cv_dw_b"][j]), _row(P["cv_ln_g"][j]),
                           _row(P["cv_ln_b"][j]), W["cv_pw2_w"][j], _row(P["cv_pw2_b"][j]),
                           seq(x), gt1, tm_seq)
        elif kind == 1:
            qkvz = _nm_linear(tok(x), g1, sc1, sh1, W["gdn_w_qkvz"][j], tm_tok, 512)
            ba = _nm_small(tok(x), g1, sc1, sh1, W["gdn_w_ba"][j], tm_tok)
            qkv = _gdn_prep(seq(qkvz), P["gdn_conv_w"][j], tm_seq)
            if latent:
                s0 = cache[0][:, j].astype(F32)
            else:
                s0 = jnp.zeros((B, 2, GDN_H, GDN_DK, GDN_DK), F32)
            a_row, dt_row = W["gdn_a_row"][j:j + 1], W["gdn_dt_row"][j:j + 1]
            o_f, s_f = _gdn_chunk(qkv, seq(ba), a_row, dt_row, s0[:, 0], rev=False)
            o_b, s_b = _gdn_chunk(qkv, seq(ba), a_row, dt_row, s0[:, 1], rev=True)
            gdn_states.append(jnp.stack([s_f, s_b], axis=1))
            x = _gdn_out(tok(o_f), tok(o_b), qkvz, _row(P["gdn_o_g"][j]), W["gdn_w_o"][j],
                         tok(x), gt1, tm_tok)
        else:
            qkv = _nm_linear(tok(x), g1, sc1, sh1, W["na_w_qkv"][j], tm_tok, 512)
            if latent:
                bias = _na_bias_table(W["na_rpb"][j], L // GRID_W)
                o = _na_attn(seq(qkv), cache[1][:, j], cache[2][:, j], bias)
            else:
                o, k_ctx, v_ctx = _ctx_attn(seq(qkv))
                na_ks.append(k_ctx)
                na_vs.append(v_ctx)
            x = _linear_res(tok(o), W["na_w_o"][j], tok(x), gt1, tm_tok)
        h2, gid, w4 = _moe_router(tok(x), _row(P["norm2_g"][i]), sc2, sh2, W["moe_w_rt"][i],
                                  W["moe_b_r"][i], tm_tok)
        plan = _moe_plan(gid.reshape(B * L), w4.transpose(0, 2, 1).reshape(B * L, MOE_E), MOE_TILE)
        y = _moe_sparse(h2.reshape(B * L, D_MODEL), plan, W["moe_w1"][i], W["moe_w3"][i],
                        W["moe_w2"][i], MOE_TILE)
        x = seq(_moe_combine(tok(x), y, gt2, tm_tok))
    y = seq(_final_norm(tok(x), _row(P["final_norm_g"]), tm_tok))
    return y, gdn_states, na_ks, na_vs


def kernel(x_prompt, x_sample, state_gdn, cache_na_k, cache_na_v, c, c_ctx,
           ada_w, ada_b, norm1_g, norm2_g,
           cv_pw1_w, cv_pw1_b, cv_dw_w, cv_dw_b, cv_ln_g, cv_ln_b, cv_pw2_w, cv_pw2_b,
           gdn_w_qkvz, gdn_conv_w, gdn_w_ba, gdn_a_log, gdn_dt_bias, gdn_o_g, gdn_w_o,
           na_w_qkv, na_rpb, na_w_o,
           moe_wg, moe_bg, moe_we, moe_be, moe_w1, moe_w3, moe_w2,
           final_norm_g):
    P = dict(norm1_g=norm1_g, norm2_g=norm2_g,
             cv_pw1_w=cv_pw1_w, cv_pw1_b=cv_pw1_b, cv_dw_w=cv_dw_w, cv_dw_b=cv_dw_b,
             cv_ln_g=cv_ln_g, cv_ln_b=cv_ln_b, cv_pw2_w=cv_pw2_w, cv_pw2_b=cv_pw2_b,
             gdn_w_qkvz=gdn_w_qkvz, gdn_conv_w=gdn_conv_w, gdn_w_ba=gdn_w_ba, gdn_a_log=gdn_a_log,
             gdn_dt_bias=gdn_dt_bias, gdn_o_g=gdn_o_g, gdn_w_o=gdn_w_o,
             na_w_qkv=na_w_qkv, na_rpb=na_rpb, na_w_o=na_w_o,
             moe_wg=moe_wg, moe_bg=moe_bg, moe_we=moe_we, moe_be=moe_be,
             moe_w1=moe_w1, moe_w3=moe_w3, moe_w2=moe_w2, final_norm_g=final_norm_g)
    W = _prepare_weights(P)
    nb = c.shape[0]
    rows = 8 * ((nb + 1 + 7) // 8)
    cond = jnp.concatenate([c, c_ctx[None, :], jnp.zeros((rows - nb - 1, D_MODEL), F32)], axis=0)
    mods = _ada_all(cond, ada_w, ada_b)
    y_prompt, gdn_states, na_ks, na_vs = _trunk(x_prompt, mods[:, nb:nb + 1], P, W, None)
    y_sample, _, _, _ = _trunk(x_sample, mods[:, :nb], P, W, (state_gdn, cache_na_k, cache_na_v))
    return (y_prompt, y_sample, jnp.stack(gdn_states, axis=1),
            jnp.stack(na_ks, axis=1), jnp.stack(na_vs, axis=1))
```

```python
import functools

import numpy as np
import jax
import jax.numpy as jnp
from jax import lax
from jax.experimental import pallas as pl
from jax.experimental.pallas import tpu as pltpu

F32 = jnp.float32
BF16 = jnp.bfloat16

D_MODEL = 1024
DEPTH = 4
N_MIXERS = 3
ADA_CHUNKS = 6
RMS_EPS = 1e-6
LN_EPS = 1e-5
CONV_WIDTH = 31
CONV_HALO = 16
GDN_H = 8
GDN_DK = 128
GDN_CONV = 5
GDN_HALO = 8
GDN_BLOCK = 128
GDN_SUB = 16
NA_HEADS = 16
NA_DH = 64
NA_WIN_R = 8
NA_WIN_C = 16
NA_QROWS = 4
NA_KROWS = 12
GRID_W = 64
MOE_G = 4
MOE_E = 4
MOE_F = 256
MOE_TILE = 512
ROUTER_ROWS = 32
LANES = 128
SUBLANES = 8
NEG_BIG = -1e30

VMEM_LIMIT_BYTES = 48 * 1024 * 1024


def _cparams(*sem):
    return pltpu.CompilerParams(dimension_semantics=sem, vmem_limit_bytes=VMEM_LIMIT_BYTES)


def _sigmoid(x):
    return 1.0 / (1.0 + jnp.exp(-x))


def _silu(x):
    return x * _sigmoid(x)


def _norm_mod(x, g, sc, sh):
    ms = jnp.mean(x * x, axis=-1, keepdims=True)
    y = x * lax.rsqrt(ms + RMS_EPS)
    return (y * g) * (1.0 + sc) + sh


def _dot(a, b):
    return jnp.dot(a, b, preferred_element_type=F32)


def _dot_nt(a, b):
    return lax.dot_general(a, b, (((1,), (1,)), ((), ())), preferred_element_type=F32)


def _dot_f32(a, b):
    return jnp.dot(a, b, preferred_element_type=F32, precision=lax.Precision.HIGHEST)


def _split_bf16(w):
    hi = w.astype(BF16)
    lo = (w - hi.astype(F32)).astype(BF16)
    return hi, lo


def _mod_spec(mod):
    if mod.shape[0] == 1:
        return pl.BlockSpec((None, 1, D_MODEL), lambda b, *_: (0, 0, 0))
    return pl.BlockSpec((None, 1, D_MODEL), lambda b, *_: (b, 0, 0))


def _vec_spec(n):
    return pl.BlockSpec((1, n), lambda *_: (0, 0))


def _ada_kernel(c_ref, w_ref, b_ref, o_ref):
    s = _silu(c_ref[...]).astype(BF16)
    o_ref[...] = _dot(s, w_ref[...].astype(BF16)) + b_ref[...]


def _ada_all(cond, ada_w, ada_b, tn=1024):
    R = cond.shape[0]
    N = ada_w.shape[-1]
    return pl.pallas_call(
        _ada_kernel,
        out_shape=jax.ShapeDtypeStruct((DEPTH, R, N), F32),
        grid=(DEPTH, N // tn),
        in_specs=[pl.BlockSpec((R, D_MODEL), lambda l, j: (0, 0)),
                  pl.BlockSpec((None, D_MODEL, tn), lambda l, j: (l, 0, j)),
                  pl.BlockSpec((None, 1, tn), lambda l, j: (l, 0, j))],
        out_specs=pl.BlockSpec((None, R, tn), lambda l, j: (l, 0, j)),
        compiler_params=_cparams("parallel", "parallel"),
        name="ada",
    )(cond, ada_w, ada_b.reshape(DEPTH, 1, N))


def _nm_linear_kernel(x_ref, g_ref, sc_ref, sh_ref, w_ref, o_ref, h_ref):
    @pl.when(pl.program_id(2) == 0)
    def _():
        h_ref[...] = _norm_mod(x_ref[...], g_ref[...], sc_ref[...], sh_ref[...]).astype(BF16)

    o_ref[...] = _dot(h_ref[...], w_ref[...]).astype(o_ref.dtype)


def _nm_linear(x, g, sc, sh, w, tm, tn):
    B, L, _ = x.shape
    N = w.shape[1]
    return pl.pallas_call(
        _nm_linear_kernel,
        out_shape=jax.ShapeDtypeStruct((B, L, N), F32),
        grid=(B, L // tm, N // tn),
        in_specs=[pl.BlockSpec((None, tm, D_MODEL), lambda b, i, j: (b, i, 0)),
                  _vec_spec(D_MODEL), _mod_spec(sc), _mod_spec(sh),
                  pl.BlockSpec((D_MODEL, tn), lambda b, i, j: (0, j))],
        out_specs=pl.BlockSpec((None, tm, tn), lambda b, i, j: (b, i, j)),
        scratch_shapes=[pltpu.VMEM((tm, D_MODEL), BF16)],
        compiler_params=_cparams("parallel", "parallel", "arbitrary"),
        name="nm_linear",
    )(x, g, sc, sh, w)


def _nm_glu_kernel(x_ref, g_ref, sc_ref, sh_ref, wa_ref, wg_ref, ba_ref, bg_ref, o_ref, h_ref):
    @pl.when(pl.program_id(2) == 0)
    def _():
        h_ref[...] = _norm_mod(x_ref[...], g_ref[...], sc_ref[...], sh_ref[...]).astype(BF16)

    h = h_ref[...]
    a = _dot(h, wa_ref[...]) + ba_ref[...]
    gate = _dot(h, wg_ref[...]) + bg_ref[...]
    o_ref[...] = a * _sigmoid(gate)


def _nm_glu(x, g, sc, sh, w, b, tm, tn):
    B, L, _ = x.shape
    N = w.shape[1] // 2
    nj = N // tn
    return pl.pallas_call(
        _nm_glu_kernel,
        out_shape=jax.ShapeDtypeStruct((B, L, N), F32),
        grid=(B, L // tm, nj),
        in_specs=[pl.BlockSpec((None, tm, D_MODEL), lambda b_, i, j: (b_, i, 0)),
                  _vec_spec(D_MODEL), _mod_spec(sc), _mod_spec(sh),
                  pl.BlockSpec((D_MODEL, tn), lambda b_, i, j: (0, j)),
                  pl.BlockSpec((D_MODEL, tn), lambda b_, i, j: (0, j + nj)),
                  pl.BlockSpec((1, tn), lambda b_, i, j: (0, j)),
                  pl.BlockSpec((1, tn), lambda b_, i, j: (0, j + nj))],
        out_specs=pl.BlockSpec((None, tm, tn), lambda b_, i, j: (b_, i, j)),
        scratch_shapes=[pltpu.VMEM((tm, D_MODEL), BF16)],
        compiler_params=_cparams("parallel", "parallel", "arbitrary"),
        name="nm_glu",
    )(x, g, sc, sh, w, w, b, b)


def _nm_small_kernel(x_ref, g_ref, sc_ref, sh_ref, whi_ref, wlo_ref, o_ref):
    h = _norm_mod(x_ref[...], g_ref[...], sc_ref[...], sh_ref[...])
    h_hi, h_lo = _split_bf16(h)
    o_ref[...] = _dot(h_hi, whi_ref[...]) + _dot(h_hi, wlo_ref[...]) + _dot(h_lo, whi_ref[...])


def _nm_small(x, g, sc, sh, w, tm):
    B, L, _ = x.shape
    w_hi, w_lo = _split_bf16(w)
    return pl.pallas_call(
        _nm_small_kernel,
        out_shape=jax.ShapeDtypeStruct((B, L, LANES), F32),
        grid=(B, L // tm),
        in_specs=[pl.BlockSpec((None, tm, D_MODEL), lambda b, i: (b, i, 0)),
                  _vec_spec(D_MODEL), _mod_spec(sc), _mod_spec(sh),
                  pl.BlockSpec((D_MODEL, LANES), lambda b, i: (0, 0)),
                  pl.BlockSpec((D_MODEL, LANES), lambda b, i: (0, 0))],
        out_specs=pl.BlockSpec((None, tm, LANES), lambda b, i: (b, i, 0)),
        compiler_params=_cparams("parallel", "parallel"),
        name="nm_small",
    )(x, g, sc, sh, w_hi, w_lo)


def _linear_res_kernel(a_ref, w_ref, x_ref, gt_ref, o_ref):
    y = _dot(a_ref[...].astype(BF16), w_ref[...])
    o_ref[...] = x_ref[...] + gt_ref[...] * y


def _linear_res(a, w, x, gt, tm):
    B, L, K = a.shape
    return pl.pallas_call(
        _linear_res_kernel,
        out_shape=jax.ShapeDtypeStruct((B, L, D_MODEL), F32),
        grid=(B, L // tm),
        in_specs=[pl.BlockSpec((None, tm, K), lambda b, i: (b, i, 0)),
                  pl.BlockSpec((K, D_MODEL), lambda b, i: (0, 0)),
                  pl.BlockSpec((None, tm, D_MODEL), lambda b, i: (b, i, 0)),
                  _mod_spec(gt)],
        out_specs=pl.BlockSpec((None, tm, D_MODEL), lambda b, i: (b, i, 0)),
        compiler_params=_cparams("parallel", "parallel"),
        name="linear_res",
    )(a, w, x, gt)


def _conv_tail_kernel(u_ref, up_ref, un_ref, dww_ref, dwb_ref, lng_ref, lnb_ref, w2_ref, b2_ref,
                      x_ref, gt_ref, o_ref, buf_ref, sh_ref, cv_ref, *, tm, rc):
    i = pl.program_id(1)
    last = pl.num_programs(1) - 1
    buf_ref[0:CONV_HALO, :] = jnp.where(i > 0, up_ref[...], 0.0)
    buf_ref[CONV_HALO:CONV_HALO + tm, :] = u_ref[...]
    buf_ref[CONV_HALO + tm:, :] = jnp.where(i < last, un_ref[...], 0.0)
    off = CONV_HALO - CONV_WIDTH // 2
    nsh = sh_ref.shape[1]
    for b in range(SUBLANES):
        sh_ref[b] = buf_ref[b:b + nsh, :]
    for c in range(D_MODEL // LANES):
        cs = slice(c * LANES, (c + 1) * LANES)
        for r0 in range(0, tm, rc):
            acc = jnp.zeros((rc, LANES), F32)
            for k in range(CONV_WIDTH):
                a, b = divmod(off + k, SUBLANES)
                acc = acc + dww_ref[k:k + 1, cs] * sh_ref[b, r0 + a * SUBLANES:r0 + a * SUBLANES + rc, cs]
            cv_ref[r0:r0 + rc, cs] = acc + dwb_ref[:, cs]
    v = cv_ref[...]
    mu = jnp.mean(v, axis=-1, keepdims=True)
    vc = v - mu
    var = jnp.mean(vc * vc, axis=-1, keepdims=True)
    y = _silu(vc * lax.rsqrt(var + LN_EPS) * lng_ref[...] + lnb_ref[...])
    z = _dot(y.astype(BF16), w2_ref[...]) + b2_ref[...]
    o_ref[...] = x_ref[...] + gt_ref[...] * z


def _conv_tail(u, dw_w, dw_b, ln_g, ln_b, w2, b2, x, gt, tm):
    B, L, _ = u.shape
    hb = tm // CONV_HALO
    nh = L // CONV_HALO
    dww = jnp.pad(dw_w, ((0, 32 - CONV_WIDTH), (0, 0)))
    last_tap = CONV_HALO - CONV_WIDTH // 2 + CONV_WIDTH - 1
    kern = functools.partial(_conv_tail_kernel, tm=tm, rc=64)
    return pl.pallas_call(
        kern,
        out_shape=jax.ShapeDtypeStruct((B, L, D_MODEL), F32),
        grid=(B, L // tm),
        in_specs=[pl.BlockSpec((None, tm, D_MODEL), lambda b, i: (b, i, 0)),
                  pl.BlockSpec((None, CONV_HALO, D_MODEL),
                               lambda b, i: (b, jnp.maximum(i * hb - 1, 0), 0)),
                  pl.BlockSpec((None, CONV_HALO, D_MODEL),
                               lambda b, i: (b, jnp.minimum((i + 1) * hb, nh - 1), 0)),
                  pl.BlockSpec((32, D_MODEL), lambda b, i: (0, 0)),
                  _vec_spec(D_MODEL), _vec_spec(D_MODEL), _vec_spec(D_MODEL),
                  pl.BlockSpec((D_MODEL, D_MODEL), lambda b, i: (0, 0)),
                  _vec_spec(D_MODEL),
                  pl.BlockSpec((None, tm, D_MODEL), lambda b, i: (b, i, 0)),
                  _mod_spec(gt)],
        out_specs=pl.BlockSpec((None, tm, D_MODEL), lambda b, i: (b, i, 0)),
        scratch_shapes=[pltpu.VMEM((tm + 2 * CONV_HALO, D_MODEL), F32),
                        pltpu.VMEM((SUBLANES, tm + SUBLANES * (last_tap // SUBLANES), D_MODEL), F32),
                        pltpu.VMEM((tm, D_MODEL), F32)],
        compiler_params=_cparams("parallel", "parallel"),
        name="conv_tail",
    )(u, u, u, dww, dw_b, ln_g, ln_b, w2, b2, x, gt)


def _gdn_prep_kernel(u_ref, up_ref, un_ref, w_ref, o_ref, buf_ref, *, tm):
    i = pl.program_id(1)
    j = pl.program_id(2)
    last = pl.num_programs(1) - 1
    buf_ref[0:GDN_HALO, :] = jnp.where(i > 0, up_ref[...], 0.0)
    buf_ref[GDN_HALO:GDN_HALO + tm, :] = u_ref[...]
    buf_ref[GDN_HALO + tm:, :] = jnp.where(i < last, un_ref[...], 0.0)
    off = GDN_HALO - GDN_CONV // 2
    for h in range(GDN_H):
        cs = slice(h * GDN_DK, (h + 1) * GDN_DK)
        acc = jnp.zeros((tm, GDN_DK), F32)
        for k in range(GDN_CONV):
            acc = acc + w_ref[k:k + 1, cs] * buf_ref[off + k:off + k + tm, cs]
        y = _silu(acc)
        nrm = y * lax.rsqrt(jnp.sum(y * y, axis=-1, keepdims=True) + 1e-6)
        o_ref[:, cs] = jnp.where(j == 0, nrm * (GDN_DK ** -0.5), jnp.where(j == 1, nrm, y))


def _gdn_prep(qkvz, conv_w, tm):
    B, L, _ = qkvz.shape
    W = GDN_H * GDN_DK
    hb = tm // GDN_HALO
    nh = L // GDN_HALO
    cw = jnp.pad(conv_w, ((0, 8 - GDN_CONV), (0, 0)))
    kern = functools.partial(_gdn_prep_kernel, tm=tm)
    return pl.pallas_call(
        kern,
        out_shape=jax.ShapeDtypeStruct((B, L, 3 * W), F32),
        grid=(B, L // tm, 3),
        in_specs=[pl.BlockSpec((None, tm, W), lambda b, i, j: (b, i, j)),
                  pl.BlockSpec((None, GDN_HALO, W), lambda b, i, j: (b, jnp.maximum(i * hb - 1, 0), j)),
                  pl.BlockSpec((None, GDN_HALO, W),
                               lambda b, i, j: (b, jnp.minimum((i + 1) * hb, nh - 1), j)),
                  pl.BlockSpec((8, W), lambda b, i, j: (0, j))],
        out_specs=pl.BlockSpec((None, tm, W), lambda b, i, j: (b, i, j)),
        scratch_shapes=[pltpu.VMEM((tm + 2 * GDN_HALO, W), F32)],
        compiler_params=_cparams("parallel", "parallel", "arbitrary"),
        name="gdn_prep",
    )(qkvz, qkvz, qkvz, cw)


def _split3(x):
    hi = x.astype(BF16)
    r1 = x - hi.astype(F32)
    mid = r1.astype(BF16)
    lo = (r1 - mid.astype(F32)).astype(BF16)
    return hi, mid, lo


def _mm3(a, b):
    lhs = jnp.concatenate([a[0], a[1]], axis=1)
    rhs = jnp.concatenate([b[0], b[0]], axis=0)
    return _dot(lhs, rhs) + _dot(a[0], b[1])


def _gdn_chunk_kernel(q_ref, k_ref, v_ref, ba_ref, a_ref, dtb_ref, s0_ref, o_ref, sfin_ref, s_ref,
                      *, rev, col0):
    n = pl.program_id(1)
    C = GDN_BLOCK

    @pl.when(n == 0)
    def _():
        s_ref[...] = s0_ref[...]

    ri = lax.broadcasted_iota(jnp.int32, (C, C), 0)
    ci = lax.broadcasted_iota(jnp.int32, (C, C), 1)
    if rev:
        incl, strict = ri <= ci, ri < ci
    else:
        incl, strict = ri >= ci, ri > ci
    blk = (ri // GDN_SUB) == (ci // GDN_SUB)
    eye_f = jnp.where(ri == ci, 1.0, 0.0)

    ba = ba_ref[...]
    beta_all = _sigmoid(ba)
    xa = ba + dtb_ref[...]
    softplus = jnp.maximum(xa, 0.0) + jnp.log(1.0 + jnp.exp(-jnp.abs(xa)))
    g_all = -a_ref[...] * softplus
    tri = jnp.where(incl, 1.0, 0.0).astype(BF16)
    g_hi, g_mid, g_lo = _split3(g_all)
    cum_all = _dot(tri, g_hi) + _dot(tri, g_mid) + _dot(tri, g_lo)
    cum_t = cum_all.T
    tot_row = cum_all[0:1, :] if rev else cum_all[C - 1:C, :]
    ecum_all = jnp.exp(cum_all)
    etail_all = jnp.exp(tot_row - cum_all)
    etot_row = jnp.exp(tot_row)

    HS = range(GDN_H)
    cs = [slice(h * GDN_DK, (h + 1) * GDN_DK) for h in HS]
    cg = [col0 + GDN_H + h for h in HS]
    k = [k_ref[:, cs[h]] for h in HS]
    kbf = [k[h].astype(BF16) for h in HS]
    beta = [beta_all[:, col0 + h:col0 + h + 1] for h in HS]
    kb = [k[h] * beta[h] for h in HS]
    decay = [jnp.where(incl, jnp.exp(jnp.where(
        incl, cum_all[:, cg[h]:cg[h] + 1] - cum_t[cg[h]:cg[h] + 1, :], 0.0)), 0.0) for h in HS]
    nm = [jnp.where(strict, _dot_nt(kb[h].astype(BF16), kbf[h]) * decay[h], 0.0) for h in HS]
    nd = [jnp.where(blk, nm[h], 0.0) for h in HS]
    nd_s = [_split_bf16(nd[h]) for h in HS]
    nd2_s = [_split_bf16(_mm3(nd_s[h], nd_s[h])) for h in HS]
    nd4_s = [_split_bf16(_mm3(nd2_s[h], nd2_s[h])) for h in HS]
    nd8_s = [_split_bf16(_mm3(nd4_s[h], nd4_s[h])) for h in HS]
    t = [eye_f - nd[h] for h in HS]
    t = [t[h] + _mm3(_split_bf16(t[h]), nd2_s[h]) for h in HS]
    t = [t[h] + _mm3(_split_bf16(t[h]), nd4_s[h]) for h in HS]
    t = [t[h] + _mm3(_split_bf16(t[h]), nd8_s[h]) for h in HS]
    t_s = [_split_bf16(t[h]) for h in HS]
    m_s = [_split_bf16(_mm3(t_s[h], _split_bf16(nm[h] - nd[h]))) for h in HS]
    m2_s = [_split_bf16(_mm3(m_s[h], m_s[h])) for h in HS]
    m4_s = [_split_bf16(_mm3(m2_s[h], m2_s[h])) for h in HS]
    a_inv = [t[h] + _mm3(m4_s[h], t_s[h]) for h in HS]
    a_inv = [a_inv[h] + _mm3(m2_s[h], _split_bf16(a_inv[h])) for h in HS]
    a_inv = [a_inv[h] - _mm3(m_s[h], _split_bf16(a_inv[h])) for h in HS]
    e_cum = [ecum_all[:, cg[h]:cg[h] + 1] for h in HS]
    rhs = [jnp.concatenate([v_ref[:, cs[h]] * beta[h], kb[h] * e_cum[h]], axis=-1) for h in HS]
    sol = [_mm3(_split_bf16(a_inv[h]), _split_bf16(rhs[h])) for h in HS]
    q = [q_ref[:, cs[h]] for h in HS]
    qk = [jnp.where(incl, _dot_nt(q[h].astype(BF16), kbf[h]) * decay[h], 0.0).astype(BF16)
          for h in HS]
    q_g = [(q[h] * e_cum[h]).astype(BF16) for h in HS]
    k_tail_t = [(k[h] * etail_all[:, cg[h]:cg[h] + 1]).T.astype(BF16) for h in HS]
    s = [s_ref[h] for h in HS]
    sb = [s[h].astype(BF16) for h in HS]
    vb = [(sol[h][:, :GDN_DK] - _dot(sol[h][:, GDN_DK:].astype(BF16), sb[h])).astype(BF16)
          for h in HS]
    for h in HS:
        o_ref[:, cs[h]] = _dot(q_g[h], sb[h]) + _dot(qk[h], vb[h])
    for h in HS:
        s_ref[h] = s[h] * etot_row[:, cg[h]:cg[h] + 1] + _dot(k_tail_t[h], vb[h])

    @pl.when(n == pl.num_programs(1) - 1)
    def _():
        sfin_ref[...] = s_ref[...]


def _gdn_chunk(qkv, ba, a_row, dtb_row, s0, rev):
    B, L, _ = qkv.shape
    W = GDN_H * GDN_DK
    nc = L // GDN_BLOCK
    cidx = (lambda n: nc - 1 - n) if rev else (lambda n: n)
    kern = functools.partial(_gdn_chunk_kernel, rev=rev, col0=2 * GDN_H if rev else 0)
    return pl.pallas_call(
        kern,
        out_shape=(jax.ShapeDtypeStruct((B, L, W), F32),
                   jax.ShapeDtypeStruct((B, GDN_H, GDN_DK, GDN_DK), F32)),
        grid=(B, nc),
        in_specs=[pl.BlockSpec((None, GDN_BLOCK, W), lambda b, n: (b, cidx(n), 0)),
                  pl.BlockSpec((None, GDN_BLOCK, W), lambda b, n: (b, cidx(n), 1)),
                  pl.BlockSpec((None, GDN_BLOCK, W), lambda b, n: (b, cidx(n), 2)),
                  pl.BlockSpec((None, GDN_BLOCK, LANES), lambda b, n: (b, cidx(n), 0)),
                  _vec_spec(LANES), _vec_spec(LANES),
                  pl.BlockSpec((None, GDN_H, GDN_DK, GDN_DK), lambda b, n: (b, 0, 0, 0))],
        out_specs=(pl.BlockSpec((None, GDN_BLOCK, W), lambda b, n: (b, cidx(n), 0)),
                   pl.BlockSpec((None, GDN_H, GDN_DK, GDN_DK), lambda b, n: (b, 0, 0, 0))),
        scratch_shapes=[pltpu.VMEM((GDN_H, GDN_DK, GDN_DK), F32)],
        compiler_params=_cparams("parallel", "arbitrary"),
        name="gdn_chunk_rev" if rev else "gdn_chunk_fwd",
    )(qkv, qkv, qkv, ba, a_row, dtb_row, s0)


def _gdn_out_kernel(of_ref, ob_ref, z_ref, og_ref, w_ref, x_ref, gt_ref, o_ref, y_ref):
    o = of_ref[...] + ob_ref[...]
    z = z_ref[...]
    for h in range(GDN_H):
        cs = slice(h * GDN_DK, (h + 1) * GDN_DK)
        oh = o[:, cs]
        ms = jnp.mean(oh * oh, axis=-1, keepdims=True)
        y = (oh * lax.rsqrt(ms + RMS_EPS)) * og_ref[...]
        y_ref[:, cs] = (y * _silu(z[:, cs])).astype(BF16)
    o_ref[...] = x_ref[...] + gt_ref[...] * _dot(y_ref[...], w_ref[...])


def _gdn_out(o_f, o_b, qkvz, o_g, w_o, x, gt, tm):
    B, L, W = o_f.shape
    return pl.pallas_call(
        _gdn_out_kernel,
        out_shape=jax.ShapeDtypeStruct((B, L, D_MODEL), F32),
        grid=(B, L // tm),
        in_specs=[pl.BlockSpec((None, tm, W), lambda b, i: (b, i, 0)),
                  pl.BlockSpec((None, tm, W), lambda b, i: (b, i, 0)),
                  pl.BlockSpec((None, tm, W), lambda b, i: (b, i, 3)),
                  _vec_spec(GDN_DK),
                  pl.BlockSpec((W, D_MODEL), lambda b, i: (0, 0)),
                  pl.BlockSpec((None, tm, D_MODEL), lambda b, i: (b, i, 0)),
                  _mod_spec(gt)],
        out_specs=pl.BlockSpec((None, tm, D_MODEL), lambda b, i: (b, i, 0)),
        scratch_shapes=[pltpu.VMEM((tm, W), BF16)],
        compiler_params=_cparams("parallel", "parallel"),
        name="gdn_out",
    )(o_f, o_b, qkvz, o_g, w_o, x, gt)


def _ctx_attn_kernel(q_ref, k_ref, v_ref, o_ref, ko_ref, vo_ref):
    scale = NA_DH ** -0.5
    for hh in range(2):
        cs = slice(hh * NA_DH, (hh + 1) * NA_DH)
        q = q_ref[:, cs]
        k = k_ref[:, cs]
        v = v_ref[:, cs]
        s = _dot_nt(q.astype(BF16), k.astype(BF16)) * scale
        p = jnp.exp(s - jnp.max(s, axis=-1, keepdims=True))
        p = p / jnp.sum(p, axis=-1, keepdims=True)
        o_ref[:, cs] = _dot(p.astype(BF16), v.astype(BF16))
        ko_ref[hh] = k
        vo_ref[hh] = v


def _ctx_attn(qkv):
    B, L, _ = qkv.shape
    hp = NA_HEADS // 2
    kv_shape = jax.ShapeDtypeStruct((B, NA_HEADS, L, NA_DH), F32)
    return pl.pallas_call(
        _ctx_attn_kernel,
        out_shape=(jax.ShapeDtypeStruct((B, L, D_MODEL), F32), kv_shape, kv_shape),
        grid=(B, hp),
        in_specs=[pl.BlockSpec((None, L, 2 * NA_DH), lambda b, p: (b, 0, p)),
                  pl.BlockSpec((None, L, 2 * NA_DH), lambda b, p: (b, 0, hp + p)),
                  pl.BlockSpec((None, L, 2 * NA_DH), lambda b, p: (b, 0, 2 * hp + p))],
        out_specs=(pl.BlockSpec((None, L, 2 * NA_DH), lambda b, p: (b, 0, p)),
                   pl.BlockSpec((None, 2, L, NA_DH), lambda b, p: (b, p, 0, 0)),
                   pl.BlockSpec((None, 2, L, NA_DH), lambda b, p: (b, p, 0, 0))),
        compiler_params=_cparams("parallel", "parallel"),
        name="ctx_attn",
    )(qkv, qkv, qkv)


def _na_geometry(rows):
    nblk = rows // NA_QROWS
    kr = min(NA_WIN_R, rows)
    variants, var_of_block, kstart = [], [], []
    for b in range(nblk):
        r0 = b * NA_QROWS
        ks = int(np.clip(r0 - NA_WIN_R // 2, 0, rows - NA_KROWS))
        qr = r0 + np.arange(NA_QROWS)
        rs = np.clip(qr - kr // 2, 0, rows - kr)
        key_row = ks + np.arange(NA_KROWS)
        ok = (key_row[None, :] >= rs[:, None]) & (key_row[None, :] < rs[:, None] + kr)
        dr = np.where(ok, key_row[None, :] - qr[:, None] + NA_WIN_R - 1, 0)
        geo = (ok.tobytes(), dr.tobytes())
        if geo not in [g for g, _, _ in variants]:
            variants.append((geo, ok, dr))
        var_of_block.append([g for g, _, _ in variants].index(geo))
        kstart.append(ks)
    return [(ok, dr) for _, ok, dr in variants], var_of_block, kstart


def _na_attn_kernel(q_ref, k_ref, v_ref, kc_ref, vc_ref, bias_ref, o_ref, *, rows):
    scale = NA_DH ** -0.5
    nblk = rows // NA_QROWS
    nq = NA_QROWS * GRID_W
    nk = NA_KROWS * GRID_W
    kc = [kc_ref[hh].astype(BF16) for hh in range(2)]
    vc = [vc_ref[hh].astype(BF16) for hh in range(2)]

    def body(blk, carry):
        ks = jnp.clip(blk * NA_QROWS - NA_WIN_R // 2, 0, rows - NA_KROWS)
        var = jnp.where(blk == 0, 0, jnp.where(blk == nblk - 1, 2, 1))
        q0 = pl.multiple_of(blk * nq, nq)
        k0 = pl.multiple_of(ks * GRID_W, GRID_W)
        for hh in range(2):
            cs = slice(hh * NA_DH, (hh + 1) * NA_DH)
            q = (q_ref[pl.ds(q0, nq), cs] * scale).astype(BF16)
            kw = k_ref[pl.ds(k0, nk), cs].astype(BF16)
            vw = v_ref[pl.ds(k0, nk), cs].astype(BF16)
            s_loc = _dot_nt(q, kw) + bias_ref[hh, var]
            s_ctx = _dot_nt(q, kc[hh])
            m = jnp.maximum(jnp.max(s_loc, axis=-1, keepdims=True),
                            jnp.max(s_ctx, axis=-1, keepdims=True))
            p_loc = jnp.exp(s_loc - m)
            p_ctx = jnp.exp(s_ctx - m)
            den = jnp.sum(p_loc, axis=-1, keepdims=True) + jnp.sum(p_ctx, axis=-1, keepdims=True)
            o = _dot(p_loc.astype(BF16), vw) + _dot(p_ctx.astype(BF16), vc[hh])
            o_ref[pl.ds(q0, nq), cs] = o / den
        return carry

    lax.fori_loop(0, nblk, body, 0)


def _na_attn(qkv, k_ctx, v_ctx, bias):
    B, L, _ = qkv.shape
    P = k_ctx.shape[2]
    rows = L // GRID_W
    hp = NA_HEADS // 2
    kern = functools.partial(_na_attn_kernel, rows=rows)
    return pl.pallas_call(
        kern,
        out_shape=jax.ShapeDtypeStruct((B, L, D_MODEL), F32),
        grid=(B, hp),
        in_specs=[pl.BlockSpec((None, L, 2 * NA_DH), lambda b, p: (b, 0, p)),
                  pl.BlockSpec((None, L, 2 * NA_DH), lambda b, p: (b, 0, hp + p)),
                  pl.BlockSpec((None, L, 2 * NA_DH), lambda b, p: (b, 0, 2 * hp + p)),
                  pl.BlockSpec((None, 2, P, NA_DH), lambda b, p: (b, p, 0, 0)),
                  pl.BlockSpec((None, 2, P, NA_DH), lambda b, p: (b, p, 0, 0)),
                  pl.BlockSpec((2,) + bias.shape[1:], lambda b, p: (p, 0, 0, 0))],
        out_specs=pl.BlockSpec((None, L, 2 * NA_DH), lambda b, p: (b, 0, p)),
        compiler_params=_cparams("parallel", "parallel"),
        name="na_attn",
    )(qkv, qkv, qkv, k_ctx, v_ctx, bias)


def _na_bias_table(rpb, rows):
    variants, var_of_block, kstart = _na_geometry(rows)
    nblk = rows // NA_QROWS
    expect = [0] + [1] * (nblk - 2) + [2]
    assert rows % NA_QROWS == 0 and rows >= NA_KROWS and var_of_block == expect, (rows, var_of_block)
    assert all(kstart[b] == int(np.clip(b * NA_QROWS - NA_WIN_R // 2, 0, rows - NA_KROWS))
               for b in range(nblk))
    qcol = np.arange(GRID_W)
    kcol = np.arange(GRID_W)
    cstart = np.clip(qcol - NA_WIN_C // 2, 0, GRID_W - NA_WIN_C)
    col_ok = (kcol[None, :] >= cstart[:, None]) & (kcol[None, :] < cstart[:, None] + NA_WIN_C)
    dc = np.clip(kcol[None, :] - qcol[:, None], -(NA_WIN_C - 1), NA_WIN_C - 1) + NA_WIN_C - 1
    plane = jnp.where(col_ok[None, None], rpb[:, :, dc], NEG_BIG)
    masked = jnp.full((rpb.shape[0], GRID_W, GRID_W), NEG_BIG, rpb.dtype)
    tabs = []
    for row_ok, dr in variants:
        q_rows = [jnp.concatenate([plane[:, int(dr[a, c])] if row_ok[a, c] else masked
                                   for c in range(NA_KROWS)], axis=-1) for a in range(NA_QROWS)]
        tabs.append(jnp.concatenate(q_rows, axis=1))
    return jnp.stack(tabs, axis=1)


def _moe_router_kernel(x_ref, g_ref, sc_ref, sh_ref, whi_ref, wlo_ref, br_ref,
                       h_out_ref, gid_ref):
    h = _norm_mod(x_ref[...], g_ref[...], sc_ref[...], sh_ref[...])
    h_hi, h_lo = _split_bf16(h)
    lg = (_dot_nt(whi_ref[...], h_hi) + _dot_nt(wlo_ref[...], h_hi) + _dot_nt(whi_ref[...], h_lo)
          + br_ref[...])
    tm = lg.shape[1]

    def first_max(vals):
        mx = functools.reduce(jnp.maximum, vals)
        taken = jnp.zeros((1, tm), jnp.bool_)
        hot = []
        for v in vals:
            hit = jnp.logical_and(v == mx, jnp.logical_not(taken))
            taken = jnp.logical_or(taken, hit)
            hot.append(hit)
        return mx, hot

    gl = [lg[g:g + 1, :] for g in range(MOE_G)]
    gmax, g_hit = first_max(gl)
    gsum = functools.reduce(lambda a, b: a + b, [jnp.exp(v - gmax) for v in gl])
    g_w = 1.0 / gsum
    g_hot = [jnp.where(hit, 1.0, 0.0) for hit in g_hit]
    el = []
    for e in range(MOE_E):
        acc = jnp.zeros((1, tm), F32)
        for g in range(MOE_G):
            r = MOE_G + g * MOE_E + e
            acc = acc + g_hot[g] * lg[r:r + 1, :]
        el.append(acc)
    emax = functools.reduce(jnp.maximum, el)
    ex = [jnp.exp(v - emax) for v in el]
    esum = functools.reduce(lambda a, b: a + b, ex)
    pe = [v / esum for v in ex]
    m1, hot1 = first_max(pe)
    m2, hot2 = first_max([jnp.where(hot1[e], -1.0, pe[e]) for e in range(MOE_E)])
    den = m1 + m2
    w_e = [(jnp.where(hot1[e], m1, 0.0) + jnp.where(hot2[e], m2, 0.0)) / den * g_w
           for e in range(MOE_E)]
    h_out_ref[:, :D_MODEL] = h
    w_e.append(jnp.zeros((LANES - MOE_E, tm), F32))
    h_out_ref[:, D_MODEL:] = jnp.concatenate(w_e, axis=0).T
    gid_ref[...] = functools.reduce(lambda a, b: a + b,
                                    [jnp.where(g_hit[g], g, 0) for g in range(MOE_G)])


def _moe_router(x, g, sc, sh, w_rt, b_r, tm):
    B, L, _ = x.shape
    w_hi, w_lo = _split_bf16(w_rt)
    return pl.pallas_call(
        _moe_router_kernel,
        out_shape=(jax.ShapeDtypeStruct((B, L, D_MODEL + LANES), F32),
                   jax.ShapeDtypeStruct((B, 1, L), jnp.int32)),
        grid=(B, L // tm),
        in_specs=[pl.BlockSpec((None, tm, D_MODEL), lambda b, i: (b, i, 0)),
                  _vec_spec(D_MODEL), _mod_spec(sc), _mod_spec(sh),
                  pl.BlockSpec((ROUTER_ROWS, D_MODEL), lambda b, i: (0, 0)),
                  pl.BlockSpec((ROUTER_ROWS, D_MODEL), lambda b, i: (0, 0)),
                  pl.BlockSpec((ROUTER_ROWS, 1), lambda b, i: (0, 0))],
        out_specs=(pl.BlockSpec((None, tm, D_MODEL + LANES), lambda b, i: (b, i, 0)),
                   pl.BlockSpec((None, 1, tm), lambda b, i: (b, 0, i))),
        compiler_params=_cparams("parallel", "parallel"),
        name="moe_router",
    )(x, g, sc, sh, w_hi, w_lo, b_r)


def _moe_plan(gid, tm):
    T = gid.shape[0]
    n_tiles = T // tm + MOE_G
    tp = n_tiles * tm
    order = jnp.argsort(gid, stable=True).astype(jnp.int32)
    counts = jnp.sum((gid[:, None] == jnp.arange(MOE_G)[None, :]).astype(jnp.int32), axis=0)
    padded = ((counts + tm - 1) // tm) * tm
    seg_end = jnp.cumsum(padded)
    seg_start = seg_end - padded
    first = jnp.cumsum(counts) - counts

    def pick(g_idx, table):
        return functools.reduce(lambda a, b: a + b,
                                [jnp.where(g_idx == g, table[g], 0) for g in range(MOE_G)])

    p = jnp.arange(tp, dtype=jnp.int32)
    g_of_p = functools.reduce(lambda a, b: a + b,
                              [(p >= seg_end[g]).astype(jnp.int32) for g in range(MOE_G - 1)])
    valid = p - pick(g_of_p, seg_start) < pick(g_of_p, counts)
    order_ext = jnp.concatenate([jnp.zeros((tp,), jnp.int32), order, jnp.zeros((tp,), jnp.int32)])
    tok = jnp.zeros((tp,), jnp.int32)
    for g in range(MOE_G):
        win = lax.dynamic_slice(order_ext, (tp + first[g] - seg_start[g],), (tp,))
        tok = jnp.where(g_of_p == g, win, tok)
    src = jnp.where(valid, tok, 0)
    dst = jnp.where(valid, tok, T + p % (2 * tm))
    t0 = jnp.arange(n_tiles, dtype=jnp.int32) * tm
    tile_gid = functools.reduce(lambda a, b: a + b,
                                [(t0 >= seg_end[g]).astype(jnp.int32) for g in range(MOE_G - 1)])
    tile_cnt = jnp.clip(pick(tile_gid, counts) - (t0 - pick(tile_gid, seg_start)), 0, tm)
    return (src.reshape(n_tiles, 1, tm), dst.reshape(n_tiles, 1, tm), tile_gid, tile_cnt)


def _moe_sparse_kernel(tgid_ref, tcnt_ref, src_ref, srcn_ref, dstp_ref, dst_ref, h_hbm, w1_ref, w3_ref,
                       w2_ref, y_hbm, hbuf, ybuf, gsem, ssem, *, tm):
    i = pl.program_id(0)
    n = pl.num_programs(0)
    slot = lax.rem(i, 2)
    other = 1 - slot

    def start_gather(idx_ref, s):
        for r in range(tm):
            pltpu.make_async_copy(h_hbm.at[pl.ds(idx_ref[0, r], 1)], hbuf.at[s, pl.ds(r, 1)],
                                  gsem.at[s]).start(priority=r % 2)

    def start_scatter(idx_ref, s):
        for r in range(tm):
            pltpu.make_async_copy(ybuf.at[s, pl.ds(r, 1)], y_hbm.at[pl.ds(idx_ref[0, r], 1)],
                                  ssem.at[s]).start(priority=r % 2)

    def wait_gather(s):
        pltpu.make_async_copy(h_hbm.at[pl.ds(0, tm)], hbuf.at[s], gsem.at[s]).wait()

    def wait_scatter(s):
        pltpu.make_async_copy(ybuf.at[s], y_hbm.at[pl.ds(0, tm)], ssem.at[s]).wait()

    @pl.when(i == 0)
    def _():
        ybuf[...] = jnp.zeros_like(ybuf)
        n_tok = y_hbm.shape[0] - 2 * tm
        for s in range(2):
            cp = pltpu.make_async_copy(ybuf.at[s], y_hbm.at[pl.ds(n_tok + s * tm, tm)], ssem.at[s])
            cp.start()
            cp.wait()
        start_gather(src_ref, 0)

    wait_gather(slot)

    @pl.when(i >= 1)
    def _():
        wait_scatter(slot)

    start_gather(srcn_ref, other)
    start_scatter(dstp_ref, other)
    h = hbuf[slot, :, :D_MODEL].astype(BF16)
    row = lax.broadcasted_iota(jnp.int32, (tm, LANES), 0)
    cw = jnp.where(row < tcnt_ref[i], hbuf[slot, :, D_MODEL:], 0.0)
    acc = jnp.zeros((tm, D_MODEL), F32)
    for e in range(MOE_E):
        a = _dot(h, w1_ref[e])
        b = _dot(h, w3_ref[e])
        hid = _silu(a) * b * cw[:, e:e + 1]
        acc = acc + _dot(hid.astype(BF16), w2_ref[e])
    ybuf[slot] = acc

    @pl.when(i == n - 1)
    def _():
        start_scatter(dst_ref, slot)
        wait_gather(other)
        wait_scatter(other)
        wait_scatter(slot)


def _moe_sparse(h_ext, plan, w1, w3, w2, tm):
    T = h_ext.shape[0]
    src, dst, tile_gid, tile_cnt = plan
    n_tiles = src.shape[0]
    kern = functools.partial(_moe_sparse_kernel, tm=tm)
    smem_row = lambda f: pl.BlockSpec((None, 1, tm), f, memory_space=pltpu.SMEM)
    grid_spec = pltpu.PrefetchScalarGridSpec(
        num_scalar_prefetch=2,
        grid=(n_tiles,),
        in_specs=[smem_row(lambda i, tg, tc: (i, 0, 0)),
                  smem_row(lambda i, tg, tc: (jnp.minimum(i + 1, n_tiles - 1), 0, 0)),
                  smem_row(lambda i, tg, tc: (jnp.maximum(i - 1, 0), 0, 0)),
                  smem_row(lambda i, tg, tc: (i, 0, 0)),
                  pl.BlockSpec(memory_space=pl.ANY),
                  pl.BlockSpec((None, MOE_E, D_MODEL, MOE_F), lambda i, tg, tc: (tg[i], 0, 0, 0)),
                  pl.BlockSpec((None, MOE_E, D_MODEL, MOE_F), lambda i, tg, tc: (tg[i], 0, 0, 0)),
                  pl.BlockSpec((None, MOE_E, MOE_F, D_MODEL), lambda i, tg, tc: (tg[i], 0, 0, 0))],
        out_specs=pl.BlockSpec(memory_space=pl.ANY),
        scratch_shapes=[pltpu.VMEM((2, tm, D_MODEL + LANES), F32), pltpu.VMEM((2, tm, D_MODEL), F32),
                        pltpu.SemaphoreType.DMA((2,)), pltpu.SemaphoreType.DMA((2,))])
    return pl.pallas_call(
        kern,
        out_shape=jax.ShapeDtypeStruct((T + 2 * tm, D_MODEL), F32),
        grid_spec=grid_spec,
        compiler_params=_cparams("arbitrary"),
        name="moe_sparse",
    )(tile_gid, tile_cnt, src, src, dst, dst, h_ext, w1, w3, w2)


def _moe_combine_kernel(x_ref, y_ref, gt_ref, o_ref):
    o_ref[...] = x_ref[...] + gt_ref[...] * y_ref[...]


def _moe_combine(x, y, gt, tm):
    B, L, _ = x.shape
    nb = L // tm
    return pl.pallas_call(
        _moe_combine_kernel,
        out_shape=jax.ShapeDtypeStruct((B, L, D_MODEL), F32),
        grid=(B, nb),
        in_specs=[pl.BlockSpec((None, tm, D_MODEL), lambda b, i: (b, i, 0)),
                  pl.BlockSpec((tm, D_MODEL), lambda b, i: (b * nb + i, 0)),
                  _mod_spec(gt)],
        out_specs=pl.BlockSpec((None, tm, D_MODEL), lambda b, i: (b, i, 0)),
        compiler_params=_cparams("parallel", "parallel"),
        name="moe_combine",
    )(x, y, gt)


def _final_norm_kernel(x_ref, g_ref, o_ref):
    x = x_ref[...]
    ms = jnp.mean(x * x, axis=-1, keepdims=True)
    o_ref[...] = (x * lax.rsqrt(ms + RMS_EPS)) * g_ref[...]


def _final_norm(x, g, tm):
    B, L, _ = x.shape
    return pl.pallas_call(
        _final_norm_kernel,
        out_shape=jax.ShapeDtypeStruct((B, L, D_MODEL), F32),
        grid=(B, L // tm),
        in_specs=[pl.BlockSpec((None, tm, D_MODEL), lambda b, i: (b, i, 0)), _vec_spec(D_MODEL)],
        out_specs=pl.BlockSpec((None, tm, D_MODEL), lambda b, i: (b, i, 0)),
        compiler_params=_cparams("parallel", "parallel"),
        name="final_norm",
    )(x, g)


def _row(v):
    return v.reshape(1, -1)


def _prepare_weights(P):
    W = {}
    W["cv_pw1_w"] = P["cv_pw1_w"].astype(BF16)
    W["cv_pw2_w"] = P["cv_pw2_w"].astype(BF16)
    W["gdn_w_qkvz"] = P["gdn_w_qkvz"].astype(BF16)
    W["gdn_w_ba"] = jnp.pad(P["gdn_w_ba"], ((0, 0), (0, 0), (0, LANES - 4 * GDN_H)))
    a_neg = jnp.exp(P["gdn_a_log"].astype(F32))
    zeros = jnp.zeros_like(a_neg)
    a_cols = jnp.stack([zeros, a_neg], axis=2).reshape(a_neg.shape[0], 4 * GDN_H)
    dt_cols = jnp.stack([zeros, P["gdn_dt_bias"].astype(F32)], axis=2).reshape(a_neg.shape[0], 4 * GDN_H)
    W["gdn_a_row"] = jnp.pad(a_cols, ((0, 0), (0, LANES - 4 * GDN_H)))
    W["gdn_dt_row"] = jnp.pad(dt_cols, ((0, 0), (0, LANES - 4 * GDN_H)))
    W["gdn_w_o"] = P["gdn_w_o"].astype(BF16)
    W["na_w_qkv"] = P["na_w_qkv"].astype(BF16)
    W["na_w_o"] = P["na_w_o"].astype(BF16)
    W["na_rpb"] = P["na_rpb"]
    W["moe_w1"] = P["moe_w1"].astype(BF16)
    W["moe_w3"] = P["moe_w3"].astype(BF16)
    W["moe_w2"] = P["moe_w2"].astype(BF16)
    w_r = jnp.concatenate([P["moe_wg"], P["moe_we"]], axis=-1)
    n_r = w_r.shape[-1]
    W["moe_w_rt"] = jnp.pad(w_r.transpose(0, 2, 1), ((0, 0), (0, ROUTER_ROWS - n_r), (0, 0)))
    b_r = jnp.concatenate([P["moe_bg"], P["moe_be"]], axis=-1)
    W["moe_b_r"] = jnp.pad(b_r, ((0, 0), (0, ROUTER_ROWS - n_r)))[:, :, None]
    return W


def _trunk(x, mods, P, W, cache):
    B, L, _ = x.shape
    latent = cache is not None
    Bm = mods.shape[1]
    flat = Bm == 1
    tm_seq = min(L, 256)
    tm_tok = 512
    tm_big = 1024
    gdn_states, na_ks, na_vs = [], [], []

    def tok(a):
        return a.reshape(1, B * L, a.shape[-1]) if flat else a

    def seq(a):
        return a.reshape(B, L, a.shape[-1])

    for i in range(DEPTH):
        m = mods[i].reshape(Bm, 1, ADA_CHUNKS, D_MODEL)
        sh1, sc1, gt1, sh2, sc2, gt2 = (m[:, :, c] for c in range(ADA_CHUNKS))
        g1 = _row(P["norm1_g"][i])
        kind, j = i % N_MIXERS, i // N_MIXERS
        if kind == 0:
            u = _nm_glu(tok(x), g1, sc1, sh1, W["cv_pw1_w"][j], _row(P["cv_pw1_b"][j]), tm_big, 512)
            x = _conv_tail(seq(u), P["cv_dw_w"][j], _row(P["cv_dw_b"][j]), _row(P["cv_ln_g"][j]),
                           _row(P["cv_ln_b"][j]), W["cv_pw2_w"][j], _row(P["cv_pw2_b"][j]),
                           seq(x), gt1, tm_seq)
        elif kind == 1:
            qkvz = _nm_linear(tok(x), g1, sc1, sh1, W["gdn_w_qkvz"][j], tm_big, 1024)
            ba = _nm_small(tok(x), g1, sc1, sh1, W["gdn_w_ba"][j], tm_tok)
            qkv = _gdn_prep(seq(qkvz), P["gdn_conv_w"][j], tm_seq)
            if latent:
                s0 = cache[0][:, j].astype(F32)
            else:
                s0 = jnp.zeros((B, 2, GDN_H, GDN_DK, GDN_DK), F32)
            a_row, dt_row = W["gdn_a_row"][j:j + 1], W["gdn_dt_row"][j:j + 1]
            o_f, s_f = _gdn_chunk(qkv, seq(ba), a_row, dt_row, s0[:, 0], rev=False)
            o_b, s_b = _gdn_chunk(qkv, seq(ba), a_row, dt_row, s0[:, 1], rev=True)
            gdn_states.append(jnp.stack([s_f, s_b], axis=1))
            x = _gdn_out(tok(o_f), tok(o_b), qkvz, _row(P["gdn_o_g"][j]), W["gdn_w_o"][j],
                         tok(x), gt1, tm_tok)
        else:
            qkv = _nm_linear(tok(x), g1, sc1, sh1, W["na_w_qkv"][j], tm_big, 1024)
            if latent:
                bias = _na_bias_table(W["na_rpb"][j], L // GRID_W)
                o = _na_attn(seq(qkv), cache[1][:, j], cache[2][:, j], bias)
            else:
                o, k_ctx, v_ctx = _ctx_attn(seq(qkv))
                na_ks.append(k_ctx)
                na_vs.append(v_ctx)
            x = _linear_res(tok(o), W["na_w_o"][j], tok(x), gt1, tm_tok)
        h2, gid = _moe_router(tok(x), _row(P["norm2_g"][i]), sc2, sh2, W["moe_w_rt"][i],
                              W["moe_b_r"][i], tm_tok)
        plan = _moe_plan(gid.reshape(B * L), MOE_TILE)
        y = _moe_sparse(h2.reshape(B * L, D_MODEL + LANES), plan, W["moe_w1"][i], W["moe_w3"][i],
                        W["moe_w2"][i], MOE_TILE)
        x = seq(_moe_combine(tok(x), y, gt2, tm_tok))
    y = seq(_final_norm(tok(x), _row(P["final_norm_g"]), tm_tok))
    return y, gdn_states, na_ks, na_vs


def kernel(x_prompt, x_sample, state_gdn, cache_na_k, cache_na_v, c, c_ctx,
           ada_w, ada_b, norm1_g, norm2_g,
           cv_pw1_w, cv_pw1_b, cv_dw_w, cv_dw_b, cv_ln_g, cv_ln_b, cv_pw2_w, cv_pw2_b,
           gdn_w_qkvz, gdn_conv_w, gdn_w_ba, gdn_a_log, gdn_dt_bias, gdn_o_g, gdn_w_o,
           na_w_qkv, na_rpb, na_w_o,
           moe_wg, moe_bg, moe_we, moe_be, moe_w1, moe_w3, moe_w2,
           final_norm_g):
    P = dict(norm1_g=norm1_g, norm2_g=norm2_g,
             cv_pw1_w=cv_pw1_w, cv_pw1_b=cv_pw1_b, cv_dw_w=cv_dw_w, cv_dw_b=cv_dw_b,
             cv_ln_g=cv_ln_g, cv_ln_b=cv_ln_b, cv_pw2_w=cv_pw2_w, cv_pw2_b=cv_pw2_b,
             gdn_w_qkvz=gdn_w_qkvz, gdn_conv_w=gdn_conv_w, gdn_w_ba=gdn_w_ba, gdn_a_log=gdn_a_log,
             gdn_dt_bias=gdn_dt_bias, gdn_o_g=gdn_o_g, gdn_w_o=gdn_w_o,
             na_w_qkv=na_w_qkv, na_rpb=na_rpb, na_w_o=na_w_o,
             moe_wg=moe_wg, moe_bg=moe_bg, moe_we=moe_we, moe_be=moe_be,
             moe_w1=moe_w1, moe_w3=moe_w3, moe_w2=moe_w2, final_norm_g=final_norm_g)
    W = _prepare_weights(P)
    nb = c.shape[0]
    rows = 8 * ((nb + 1 + 7) // 8)
    cond = jnp.concatenate([c, c_ctx[None, :], jnp.zeros((rows - nb - 1, D_MODEL), F32)], axis=0)
    mods = _ada_all(cond, ada_w, ada_b)
    y_prompt, gdn_states, na_ks, na_vs = _trunk(x_prompt, mods[:, nb:nb + 1], P, W, None)
    y_sample, _, _, _ = _trunk(x_sample, mods[:, :nb], P, W, (state_gdn, cache_na_k, cache_na_v))
    return (y_prompt, y_sample, jnp.stack(gdn_states, axis=1),
            jnp.stack(na_ks, axis=1), jnp.stack(na_vs, axis=1))
```

```python
import functools

import numpy as np
import jax
import jax.numpy as jnp
from jax import lax
from jax.experimental import pallas as pl
from jax.experimental.pallas import tpu as pltpu

F32 = jnp.float32
BF16 = jnp.bfloat16

D_MODEL = 1024
DEPTH = 4
N_MIXERS = 3
ADA_CHUNKS = 6
RMS_EPS = 1e-6
LN_EPS = 1e-5
CONV_WIDTH = 31
CONV_HALO = 16
GDN_H = 8
GDN_DK = 128
GDN_CONV = 5
GDN_HALO = 8
GDN_BLOCK = 128
GDN_SUB = 16
NA_HEADS = 16
NA_DH = 64
NA_WIN_R = 8
NA_WIN_C = 16
NA_QROWS = 4
NA_KROWS = 12
GRID_W = 64
MOE_G = 4
MOE_E = 4
MOE_F = 256
MOE_TILE = 512
ROUTER_ROWS = 32
LANES = 128
SUBLANES = 8
NEG_BIG = -1e30

VMEM_LIMIT_BYTES = 48 * 1024 * 1024


def _cparams(*sem):
    return pltpu.CompilerParams(dimension_semantics=sem, vmem_limit_bytes=VMEM_LIMIT_BYTES)


def _sigmoid(x):
    return 1.0 / (1.0 + jnp.exp(-x))


def _silu(x):
    return x * _sigmoid(x)


def _norm_mod(x, g, sc, sh):
    ms = jnp.mean(x * x, axis=-1, keepdims=True)
    y = x * lax.rsqrt(ms + RMS_EPS)
    return (y * g) * (1.0 + sc) + sh


def _dot(a, b):
    return jnp.dot(a, b, preferred_element_type=F32)


def _dot_nt(a, b):
    return lax.dot_general(a, b, (((1,), (1,)), ((), ())), preferred_element_type=F32)


def _dot_f32(a, b):
    return jnp.dot(a, b, preferred_element_type=F32, precision=lax.Precision.HIGHEST)


def _split_bf16(w):
    hi = w.astype(BF16)
    lo = (w - hi.astype(F32)).astype(BF16)
    return hi, lo


def _mod_spec(mod):
    if mod.shape[0] == 1:
        return pl.BlockSpec((None, 1, D_MODEL), lambda b, *_: (0, 0, 0))
    return pl.BlockSpec((None, 1, D_MODEL), lambda b, *_: (b, 0, 0))


def _vec_spec(n):
    return pl.BlockSpec((1, n), lambda *_: (0, 0))


def _ada_kernel(c_ref, w_ref, b_ref, o_ref):
    s = _silu(c_ref[...]).astype(BF16)
    o_ref[...] = _dot(s, w_ref[...].astype(BF16)) + b_ref[...]


def _ada_all(cond, ada_w, ada_b, tn=1024):
    R = cond.shape[0]
    N = ada_w.shape[-1]
    return pl.pallas_call(
        _ada_kernel,
        out_shape=jax.ShapeDtypeStruct((DEPTH, R, N), F32),
        grid=(DEPTH, N // tn),
        in_specs=[pl.BlockSpec((R, D_MODEL), lambda l, j: (0, 0)),
                  pl.BlockSpec((None, D_MODEL, tn), lambda l, j: (l, 0, j)),
                  pl.BlockSpec((None, 1, tn), lambda l, j: (l, 0, j))],
        out_specs=pl.BlockSpec((None, R, tn), lambda l, j: (l, 0, j)),
        compiler_params=_cparams("parallel", "parallel"),
        name="ada",
    )(cond, ada_w, ada_b.reshape(DEPTH, 1, N))


def _nm_linear_kernel(x_ref, g_ref, sc_ref, sh_ref, w_ref, o_ref, h_ref):
    @pl.when(pl.program_id(2) == 0)
    def _():
        h_ref[...] = _norm_mod(x_ref[...], g_ref[...], sc_ref[...], sh_ref[...]).astype(BF16)

    o_ref[...] = _dot(h_ref[...], w_ref[...]).astype(o_ref.dtype)


def _nm_linear(x, g, sc, sh, w, tm, tn):
    B, L, _ = x.shape
    N = w.shape[1]
    return pl.pallas_call(
        _nm_linear_kernel,
        out_shape=jax.ShapeDtypeStruct((B, L, N), F32),
        grid=(B, L // tm, N // tn),
        in_specs=[pl.BlockSpec((None, tm, D_MODEL), lambda b, i, j: (b, i, 0)),
                  _vec_spec(D_MODEL), _mod_spec(sc), _mod_spec(sh),
                  pl.BlockSpec((D_MODEL, tn), lambda b, i, j: (0, j))],
        out_specs=pl.BlockSpec((None, tm, tn), lambda b, i, j: (b, i, j)),
        scratch_shapes=[pltpu.VMEM((tm, D_MODEL), BF16)],
        compiler_params=_cparams("parallel", "parallel", "arbitrary"),
        name="nm_linear",
    )(x, g, sc, sh, w)


def _nm_glu_kernel(x_ref, g_ref, sc_ref, sh_ref, wa_ref, wg_ref, ba_ref, bg_ref, o_ref, h_ref):
    @pl.when(pl.program_id(2) == 0)
    def _():
        h_ref[...] = _norm_mod(x_ref[...], g_ref[...], sc_ref[...], sh_ref[...]).astype(BF16)

    h = h_ref[...]
    a = _dot(h, wa_ref[...]) + ba_ref[...]
    gate = _dot(h, wg_ref[...]) + bg_ref[...]
    o_ref[...] = a * _sigmoid(gate)


def _nm_glu(x, g, sc, sh, w, b, tm, tn):
    B, L, _ = x.shape
    N = w.shape[1] // 2
    nj = N // tn
    return pl.pallas_call(
        _nm_glu_kernel,
        out_shape=jax.ShapeDtypeStruct((B, L, N), F32),
        grid=(B, L // tm, nj),
        in_specs=[pl.BlockSpec((None, tm, D_MODEL), lambda b_, i, j: (b_, i, 0)),
                  _vec_spec(D_MODEL), _mod_spec(sc), _mod_spec(sh),
                  pl.BlockSpec((D_MODEL, tn), lambda b_, i, j: (0, j)),
                  pl.BlockSpec((D_MODEL, tn), lambda b_, i, j: (0, j + nj)),
                  pl.BlockSpec((1, tn), lambda b_, i, j: (0, j)),
                  pl.BlockSpec((1, tn), lambda b_, i, j: (0, j + nj))],
        out_specs=pl.BlockSpec((None, tm, tn), lambda b_, i, j: (b_, i, j)),
        scratch_shapes=[pltpu.VMEM((tm, D_MODEL), BF16)],
        compiler_params=_cparams("parallel", "parallel", "arbitrary"),
        name="nm_glu",
    )(x, g, sc, sh, w, w, b, b)


def _nm_small_kernel(x_ref, g_ref, sc_ref, sh_ref, whi_ref, wlo_ref, o_ref):
    h = _norm_mod(x_ref[...], g_ref[...], sc_ref[...], sh_ref[...])
    h_hi, h_lo = _split_bf16(h)
    o_ref[...] = _dot(h_hi, whi_ref[...]) + _dot(h_hi, wlo_ref[...]) + _dot(h_lo, whi_ref[...])


def _nm_small(x, g, sc, sh, w, tm):
    B, L, _ = x.shape
    w_hi, w_lo = _split_bf16(w)
    return pl.pallas_call(
        _nm_small_kernel,
        out_shape=jax.ShapeDtypeStruct((B, L, LANES), F32),
        grid=(B, L // tm),
        in_specs=[pl.BlockSpec((None, tm, D_MODEL), lambda b, i: (b, i, 0)),
                  _vec_spec(D_MODEL), _mod_spec(sc), _mod_spec(sh),
                  pl.BlockSpec((D_MODEL, LANES), lambda b, i: (0, 0)),
                  pl.BlockSpec((D_MODEL, LANES), lambda b, i: (0, 0))],
        out_specs=pl.BlockSpec((None, tm, LANES), lambda b, i: (b, i, 0)),
        compiler_params=_cparams("parallel", "parallel"),
        name="nm_small",
    )(x, g, sc, sh, w_hi, w_lo)


def _linear_res_kernel(a_ref, w_ref, x_ref, gt_ref, o_ref):
    y = _dot(a_ref[...].astype(BF16), w_ref[...])
    o_ref[...] = x_ref[...] + gt_ref[...] * y


def _linear_res(a, w, x, gt, tm):
    B, L, K = a.shape
    return pl.pallas_call(
        _linear_res_kernel,
        out_shape=jax.ShapeDtypeStruct((B, L, D_MODEL), F32),
        grid=(B, L // tm),
        in_specs=[pl.BlockSpec((None, tm, K), lambda b, i: (b, i, 0)),
                  pl.BlockSpec((K, D_MODEL), lambda b, i: (0, 0)),
                  pl.BlockSpec((None, tm, D_MODEL), lambda b, i: (b, i, 0)),
                  _mod_spec(gt)],
        out_specs=pl.BlockSpec((None, tm, D_MODEL), lambda b, i: (b, i, 0)),
        compiler_params=_cparams("parallel", "parallel"),
        name="linear_res",
    )(a, w, x, gt)


def _conv_tail_kernel(u_ref, up_ref, un_ref, dww_ref, dwb_ref, lng_ref, lnb_ref, w2_ref, b2_ref,
                      x_ref, gt_ref, o_ref, buf_ref, sh_ref, cv_ref, *, tm, rc):
    i = pl.program_id(1)
    last = pl.num_programs(1) - 1
    buf_ref[0:CONV_HALO, :] = jnp.where(i > 0, up_ref[...], 0.0)
    buf_ref[CONV_HALO:CONV_HALO + tm, :] = u_ref[...]
    buf_ref[CONV_HALO + tm:, :] = jnp.where(i < last, un_ref[...], 0.0)
    off = CONV_HALO - CONV_WIDTH // 2
    nsh = sh_ref.shape[1]
    for b in range(SUBLANES):
        sh_ref[b] = buf_ref[b:b + nsh, :]
    for c in range(D_MODEL // LANES):
        cs = slice(c * LANES, (c + 1) * LANES)
        for r0 in range(0, tm, rc):
            acc = jnp.zeros((rc, LANES), F32)
            for k in range(CONV_WIDTH):
                a, b = divmod(off + k, SUBLANES)
                acc = acc + dww_ref[k:k + 1, cs] * sh_ref[b, r0 + a * SUBLANES:r0 + a * SUBLANES + rc, cs]
            cv_ref[r0:r0 + rc, cs] = acc + dwb_ref[:, cs]
    v = cv_ref[...]
    mu = jnp.mean(v, axis=-1, keepdims=True)
    vc = v - mu
    var = jnp.mean(vc * vc, axis=-1, keepdims=True)
    y = _silu(vc * lax.rsqrt(var + LN_EPS) * lng_ref[...] + lnb_ref[...])
    z = _dot(y.astype(BF16), w2_ref[...]) + b2_ref[...]
    o_ref[...] = x_ref[...] + gt_ref[...] * z


def _conv_tail(u, dw_w, dw_b, ln_g, ln_b, w2, b2, x, gt, tm):
    B, L, _ = u.shape
    hb = tm // CONV_HALO
    nh = L // CONV_HALO
    dww = jnp.pad(dw_w, ((0, 32 - CONV_WIDTH), (0, 0)))
    last_tap = CONV_HALO - CONV_WIDTH // 2 + CONV_WIDTH - 1
    kern = functools.partial(_conv_tail_kernel, tm=tm, rc=64)
    return pl.pallas_call(
        kern,
        out_shape=jax.ShapeDtypeStruct((B, L, D_MODEL), F32),
        grid=(B, L // tm),
        in_specs=[pl.BlockSpec((None, tm, D_MODEL), lambda b, i: (b, i, 0)),
                  pl.BlockSpec((None, CONV_HALO, D_MODEL),
                               lambda b, i: (b, jnp.maximum(i * hb - 1, 0), 0)),
                  pl.BlockSpec((None, CONV_HALO, D_MODEL),
                               lambda b, i: (b, jnp.minimum((i + 1) * hb, nh - 1), 0)),
                  pl.BlockSpec((32, D_MODEL), lambda b, i: (0, 0)),
                  _vec_spec(D_MODEL), _vec_spec(D_MODEL), _vec_spec(D_MODEL),
                  pl.BlockSpec((D_MODEL, D_MODEL), lambda b, i: (0, 0)),
                  _vec_spec(D_MODEL),
                  pl.BlockSpec((None, tm, D_MODEL), lambda b, i: (b, i, 0)),
                  _mod_spec(gt)],
        out_specs=pl.BlockSpec((None, tm, D_MODEL), lambda b, i: (b, i, 0)),
        scratch_shapes=[pltpu.VMEM((tm + 2 * CONV_HALO, D_MODEL), F32),
                        pltpu.VMEM((SUBLANES, tm + SUBLANES * (last_tap // SUBLANES), D_MODEL), F32),
                        pltpu.VMEM((tm, D_MODEL), F32)],
        compiler_params=_cparams("parallel", "parallel"),
        name="conv_tail",
    )(u, u, u, dww, dw_b, ln_g, ln_b, w2, b2, x, gt)


def _gdn_prep_kernel(u_ref, up_ref, un_ref, w_ref, o_ref, buf_ref, *, tm):
    i = pl.program_id(1)
    j = pl.program_id(2)
    last = pl.num_programs(1) - 1
    buf_ref[0:GDN_HALO, :] = jnp.where(i > 0, up_ref[...], 0.0)
    buf_ref[GDN_HALO:GDN_HALO + tm, :] = u_ref[...]
    buf_ref[GDN_HALO + tm:, :] = jnp.where(i < last, un_ref[...], 0.0)
    off = GDN_HALO - GDN_CONV // 2
    for h in range(GDN_H):
        cs = slice(h * GDN_DK, (h + 1) * GDN_DK)
        acc = jnp.zeros((tm, GDN_DK), F32)
        for k in range(GDN_CONV):
            acc = acc + w_ref[k:k + 1, cs] * buf_ref[off + k:off + k + tm, cs]
        y = _silu(acc)
        inv = lax.rsqrt(jnp.sum(y * y, axis=-1, keepdims=True) + 1e-6)
        f = jnp.where(j < 2, inv, 1.0) * jnp.where(j == 0, GDN_DK ** -0.5, 1.0)
        o_ref[:, cs] = y * f


def _gdn_prep(qkvz, conv_w, tm):
    B, L, _ = qkvz.shape
    W = GDN_H * GDN_DK
    hb = tm // GDN_HALO
    nh = L // GDN_HALO
    cw = jnp.pad(conv_w, ((0, 8 - GDN_CONV), (0, 0)))
    kern = functools.partial(_gdn_prep_kernel, tm=tm)
    return pl.pallas_call(
        kern,
        out_shape=jax.ShapeDtypeStruct((B, L, 3 * W), F32),
        grid=(B, L // tm, 3),
        in_specs=[pl.BlockSpec((None, tm, W), lambda b, i, j: (b, i, j)),
                  pl.BlockSpec((None, GDN_HALO, W), lambda b, i, j: (b, jnp.maximum(i * hb - 1, 0), j)),
                  pl.BlockSpec((None, GDN_HALO, W),
                               lambda b, i, j: (b, jnp.minimum((i + 1) * hb, nh - 1), j)),
                  pl.BlockSpec((8, W), lambda b, i, j: (0, j))],
        out_specs=pl.BlockSpec((None, tm, W), lambda b, i, j: (b, i, j)),
        scratch_shapes=[pltpu.VMEM((tm + 2 * GDN_HALO, W), F32)],
        compiler_params=_cparams("parallel", "parallel", "arbitrary"),
        name="gdn_prep",
    )(qkvz, qkvz, qkvz, cw)


def _split3(x):
    hi = x.astype(BF16)
    r1 = x - hi.astype(F32)
    mid = r1.astype(BF16)
    lo = (r1 - mid.astype(F32)).astype(BF16)
    return hi, mid, lo


def _mm3(a, b):
    lhs = jnp.concatenate([a[0], a[1]], axis=1)
    rhs = jnp.concatenate([b[0], b[0]], axis=0)
    return _dot(lhs, rhs) + _dot(a[0], b[1])


def _gdn_chunk_kernel(q_ref, k_ref, v_ref, ba_ref, a_ref, dtb_ref, s0_ref, o_ref, sfin_ref, s_ref,
                      *, rev, col0):
    n = pl.program_id(1)
    C = GDN_BLOCK

    @pl.when(n == 0)
    def _():
        s_ref[...] = s0_ref[...]

    ri = lax.broadcasted_iota(jnp.int32, (C, C), 0)
    ci = lax.broadcasted_iota(jnp.int32, (C, C), 1)
    if rev:
        incl, strict = ri <= ci, ri < ci
    else:
        incl, strict = ri >= ci, ri > ci
    blk = (ri // GDN_SUB) == (ci // GDN_SUB)
    eye_f = jnp.where(ri == ci, 1.0, 0.0)

    ba = ba_ref[...]
    beta_all = _sigmoid(ba)
    xa = ba + dtb_ref[...]
    softplus = jnp.maximum(xa, 0.0) + jnp.log(1.0 + jnp.exp(-jnp.abs(xa)))
    g_all = -a_ref[...] * softplus
    tri = jnp.where(incl, 1.0, 0.0).astype(BF16)
    g_hi, g_mid, g_lo = _split3(g_all)
    cum_all = _dot(tri, g_hi) + _dot(tri, g_mid) + _dot(tri, g_lo)
    cum_t = cum_all.T
    tot_row = cum_all[0:1, :] if rev else cum_all[C - 1:C, :]
    ecum_all = jnp.exp(cum_all)
    etail_all = jnp.exp(tot_row - cum_all)
    etot_row = jnp.exp(tot_row)

    HS = range(GDN_H)
    cs = [slice(h * GDN_DK, (h + 1) * GDN_DK) for h in HS]
    cg = [col0 + GDN_H + h for h in HS]
    k = [k_ref[:, cs[h]] for h in HS]
    kbf = [k[h].astype(BF16) for h in HS]
    beta = [beta_all[:, col0 + h:col0 + h + 1] for h in HS]
    kb = [k[h] * beta[h] for h in HS]
    decay = [jnp.where(incl, jnp.exp(jnp.where(
        incl, cum_all[:, cg[h]:cg[h] + 1] - cum_t[cg[h]:cg[h] + 1, :], 0.0)), 0.0) for h in HS]
    nm = [jnp.where(strict, _dot_nt(kb[h].astype(BF16), kbf[h]) * decay[h], 0.0) for h in HS]
    nd = [jnp.where(blk, nm[h], 0.0) for h in HS]
    nd_s = [_split_bf16(nd[h]) for h in HS]
    nd2_s = [_split_bf16(_mm3(nd_s[h], nd_s[h])) for h in HS]
    nd4_s = [_split_bf16(_mm3(nd2_s[h], nd2_s[h])) for h in HS]
    nd8_s = [_split_bf16(_mm3(nd4_s[h], nd4_s[h])) for h in HS]
    t = [eye_f - nd[h] for h in HS]
    t = [t[h] + _mm3(_split_bf16(t[h]), nd2_s[h]) for h in HS]
    t = [t[h] + _mm3(_split_bf16(t[h]), nd4_s[h]) for h in HS]
    t = [t[h] + _mm3(_split_bf16(t[h]), nd8_s[h]) for h in HS]
    t_s = [_split_bf16(t[h]) for h in HS]
    m_s = [_split_bf16(_mm3(t_s[h], _split_bf16(nm[h] - nd[h]))) for h in HS]
    m2_s = [_split_bf16(_mm3(m_s[h], m_s[h])) for h in HS]
    m4_s = [_split_bf16(_mm3(m2_s[h], m2_s[h])) for h in HS]
    a_inv = [t[h] + _mm3(m4_s[h], t_s[h]) for h in HS]
    a_inv = [a_inv[h] + _mm3(m2_s[h], _split_bf16(a_inv[h])) for h in HS]
    a_inv = [a_inv[h] - _mm3(m_s[h], _split_bf16(a_inv[h])) for h in HS]
    e_cum = [ecum_all[:, cg[h]:cg[h] + 1] for h in HS]
    rhs = [jnp.concatenate([v_ref[:, cs[h]] * beta[h], kb[h] * e_cum[h]], axis=-1) for h in HS]
    sol = [_mm3(_split_bf16(a_inv[h]), _split_bf16(rhs[h])) for h in HS]
    q = [q_ref[:, cs[h]] for h in HS]
    qk = [jnp.where(incl, _dot_nt(q[h].astype(BF16), kbf[h]) * decay[h], 0.0).astype(BF16)
          for h in HS]
    q_g = [(q[h] * e_cum[h]).astype(BF16) for h in HS]
    k_tail_t = [(k[h] * etail_all[:, cg[h]:cg[h] + 1]).T.astype(BF16) for h in HS]
    s = [s_ref[h] for h in HS]
    sb = [s[h].astype(BF16) for h in HS]
    vb = [(sol[h][:, :GDN_DK] - _dot(sol[h][:, GDN_DK:].astype(BF16), sb[h])).astype(BF16)
          for h in HS]
    for h in HS:
        o_ref[:, cs[h]] = _dot(q_g[h], sb[h]) + _dot(qk[h], vb[h])
    for h in HS:
        s_ref[h] = s[h] * etot_row[:, cg[h]:cg[h] + 1] + _dot(k_tail_t[h], vb[h])

    @pl.when(n == pl.num_programs(1) - 1)
    def _():
        sfin_ref[...] = s_ref[...]


def _gdn_chunk(qkv, ba, a_row, dtb_row, s0, rev):
    B, L, _ = qkv.shape
    W = GDN_H * GDN_DK
    nc = L // GDN_BLOCK
    cidx = (lambda n: nc - 1 - n) if rev else (lambda n: n)
    kern = functools.partial(_gdn_chunk_kernel, rev=rev, col0=2 * GDN_H if rev else 0)
    return pl.pallas_call(
        kern,
        out_shape=(jax.ShapeDtypeStruct((B, L, W), F32),
                   jax.ShapeDtypeStruct((B, GDN_H, GDN_DK, GDN_DK), F32)),
        grid=(B, nc),
        in_specs=[pl.BlockSpec((None, GDN_BLOCK, W), lambda b, n: (b, cidx(n), 0)),
                  pl.BlockSpec((None, GDN_BLOCK, W), lambda b, n: (b, cidx(n), 1)),
                  pl.BlockSpec((None, GDN_BLOCK, W), lambda b, n: (b, cidx(n), 2)),
                  pl.BlockSpec((None, GDN_BLOCK, LANES), lambda b, n: (b, cidx(n), 0)),
                  _vec_spec(LANES), _vec_spec(LANES),
                  pl.BlockSpec((None, GDN_H, GDN_DK, GDN_DK), lambda b, n: (b, 0, 0, 0))],
        out_specs=(pl.BlockSpec((None, GDN_BLOCK, W), lambda b, n: (b, cidx(n), 0)),
                   pl.BlockSpec((None, GDN_H, GDN_DK, GDN_DK), lambda b, n: (b, 0, 0, 0))),
        scratch_shapes=[pltpu.VMEM((GDN_H, GDN_DK, GDN_DK), F32)],
        compiler_params=_cparams("parallel", "arbitrary"),
        name="gdn_chunk_rev" if rev else "gdn_chunk_fwd",
    )(qkv, qkv, qkv, ba, a_row, dtb_row, s0)


def _gdn_out_kernel(of_ref, ob_ref, z_ref, og_ref, w_ref, x_ref, gt_ref, o_ref, y_ref):
    o = of_ref[...] + ob_ref[...]
    z = z_ref[...]
    for h in range(GDN_H):
        cs = slice(h * GDN_DK, (h + 1) * GDN_DK)
        oh = o[:, cs]
        ms = jnp.mean(oh * oh, axis=-1, keepdims=True)
        y = (oh * lax.rsqrt(ms + RMS_EPS)) * og_ref[...]
        y_ref[:, cs] = (y * _silu(z[:, cs])).astype(BF16)
    o_ref[...] = x_ref[...] + gt_ref[...] * _dot(y_ref[...], w_ref[...])


def _gdn_out(o_f, o_b, qkvz, o_g, w_o, x, gt, tm):
    B, L, W = o_f.shape
    return pl.pallas_call(
        _gdn_out_kernel,
        out_shape=jax.ShapeDtypeStruct((B, L, D_MODEL), F32),
        grid=(B, L // tm),
        in_specs=[pl.BlockSpec((None, tm, W), lambda b, i: (b, i, 0)),
                  pl.BlockSpec((None, tm, W), lambda b, i: (b, i, 0)),
                  pl.BlockSpec((None, tm, W), lambda b, i: (b, i, 3)),
                  _vec_spec(GDN_DK),
                  pl.BlockSpec((W, D_MODEL), lambda b, i: (0, 0)),
                  pl.BlockSpec((None, tm, D_MODEL), lambda b, i: (b, i, 0)),
                  _mod_spec(gt)],
        out_specs=pl.BlockSpec((None, tm, D_MODEL), lambda b, i: (b, i, 0)),
        scratch_shapes=[pltpu.VMEM((tm, W), BF16)],
        compiler_params=_cparams("parallel", "parallel"),
        name="gdn_out",
    )(o_f, o_b, qkvz, o_g, w_o, x, gt)


def _ctx_attn_kernel(q_ref, k_ref, v_ref, o_ref, ko_ref, vo_ref):
    scale = NA_DH ** -0.5
    for hh in range(2):
        cs = slice(hh * NA_DH, (hh + 1) * NA_DH)
        q = q_ref[:, cs]
        k = k_ref[:, cs]
        v = v_ref[:, cs]
        s = _dot_nt(q.astype(BF16), k.astype(BF16)) * scale
        p = jnp.exp(s - jnp.max(s, axis=-1, keepdims=True))
        p = p / jnp.sum(p, axis=-1, keepdims=True)
        o_ref[:, cs] = _dot(p.astype(BF16), v.astype(BF16))
        ko_ref[hh] = k
        vo_ref[hh] = v


def _ctx_attn(qkv):
    B, L, _ = qkv.shape
    hp = NA_HEADS // 2
    kv_shape = jax.ShapeDtypeStruct((B, NA_HEADS, L, NA_DH), F32)
    return pl.pallas_call(
        _ctx_attn_kernel,
        out_shape=(jax.ShapeDtypeStruct((B, L, D_MODEL), F32), kv_shape, kv_shape),
        grid=(B, hp),
        in_specs=[pl.BlockSpec((None, L, 2 * NA_DH), lambda b, p: (b, 0, p)),
                  pl.BlockSpec((None, L, 2 * NA_DH), lambda b, p: (b, 0, hp + p)),
                  pl.BlockSpec((None, L, 2 * NA_DH), lambda b, p: (b, 0, 2 * hp + p))],
        out_specs=(pl.BlockSpec((None, L, 2 * NA_DH), lambda b, p: (b, 0, p)),
                   pl.BlockSpec((None, 2, L, NA_DH), lambda b, p: (b, p, 0, 0)),
                   pl.BlockSpec((None, 2, L, NA_DH), lambda b, p: (b, p, 0, 0))),
        compiler_params=_cparams("parallel", "parallel"),
        name="ctx_attn",
    )(qkv, qkv, qkv)


def _na_geometry(rows):
    nblk = rows // NA_QROWS
    kr = min(NA_WIN_R, rows)
    variants, var_of_block, kstart = [], [], []
    for b in range(nblk):
        r0 = b * NA_QROWS
        ks = int(np.clip(r0 - NA_WIN_R // 2, 0, rows - NA_KROWS))
        qr = r0 + np.arange(NA_QROWS)
        rs = np.clip(qr - kr // 2, 0, rows - kr)
        key_row = ks + np.arange(NA_KROWS)
        ok = (key_row[None, :] >= rs[:, None]) & (key_row[None, :] < rs[:, None] + kr)
        dr = np.where(ok, key_row[None, :] - qr[:, None] + NA_WIN_R - 1, 0)
        geo = (ok.tobytes(), dr.tobytes())
        if geo not in [g for g, _, _ in variants]:
            variants.append((geo, ok, dr))
        var_of_block.append([g for g, _, _ in variants].index(geo))
        kstart.append(ks)
    return [(ok, dr) for _, ok, dr in variants], var_of_block, kstart


def _na_attn_kernel(q_ref, k_ref, v_ref, kc_ref, vc_ref, bias_ref, o_ref, *, rows):
    scale = NA_DH ** -0.5
    nblk = rows // NA_QROWS
    nq = NA_QROWS * GRID_W
    nk = NA_KROWS * GRID_W
    kc = [kc_ref[hh].astype(BF16) for hh in range(2)]
    vc = [vc_ref[hh].astype(BF16) for hh in range(2)]

    def body(blk, carry):
        ks = jnp.clip(blk * NA_QROWS - NA_WIN_R // 2, 0, rows - NA_KROWS)
        var = jnp.where(blk == 0, 0, jnp.where(blk == nblk - 1, 2, 1))
        q0 = pl.multiple_of(blk * nq, nq)
        k0 = pl.multiple_of(ks * GRID_W, GRID_W)
        HH = range(2)
        cs = [slice(hh * NA_DH, (hh + 1) * NA_DH) for hh in HH]
        q = [(q_ref[pl.ds(q0, nq), cs[hh]] * scale).astype(BF16) for hh in HH]
        kw = [k_ref[pl.ds(k0, nk), cs[hh]].astype(BF16) for hh in HH]
        vw = [v_ref[pl.ds(k0, nk), cs[hh]].astype(BF16) for hh in HH]
        s_loc = [_dot_nt(q[hh], kw[hh]) + bias_ref[hh, var] for hh in HH]
        s_ctx = [_dot_nt(q[hh], kc[hh]) for hh in HH]
        m = [jnp.maximum(jnp.max(s_loc[hh], axis=-1, keepdims=True),
                         jnp.max(s_ctx[hh], axis=-1, keepdims=True)) for hh in HH]
        p_loc = [jnp.exp(s_loc[hh] - m[hh]) for hh in HH]
        p_ctx = [jnp.exp(s_ctx[hh] - m[hh]) for hh in HH]
        den = [jnp.sum(p_loc[hh], axis=-1, keepdims=True) + jnp.sum(p_ctx[hh], axis=-1, keepdims=True)
               for hh in HH]
        o = [_dot(p_loc[hh].astype(BF16), vw[hh]) + _dot(p_ctx[hh].astype(BF16), vc[hh]) for hh in HH]
        o_ref[pl.ds(q0, nq), :] = jnp.concatenate([o[hh] / den[hh] for hh in HH], axis=-1)
        return carry

    lax.fori_loop(0, nblk, body, 0)


def _na_attn(qkv, k_ctx, v_ctx, bias):
    B, L, _ = qkv.shape
    P = k_ctx.shape[2]
    rows = L // GRID_W
    hp = NA_HEADS // 2
    kern = functools.partial(_na_attn_kernel, rows=rows)
    return pl.pallas_call(
        kern,
        out_shape=jax.ShapeDtypeStruct((B, L, D_MODEL), F32),
        grid=(hp, B),
        in_specs=[pl.BlockSpec((None, L, 2 * NA_DH), lambda p, b: (b, 0, p)),
                  pl.BlockSpec((None, L, 2 * NA_DH), lambda p, b: (b, 0, hp + p)),
                  pl.BlockSpec((None, L, 2 * NA_DH), lambda p, b: (b, 0, 2 * hp + p)),
                  pl.BlockSpec((None, 2, P, NA_DH), lambda p, b: (b, p, 0, 0)),
                  pl.BlockSpec((None, 2, P, NA_DH), lambda p, b: (b, p, 0, 0)),
                  pl.BlockSpec((2,) + bias.shape[1:], lambda p, b: (p, 0, 0, 0))],
        out_specs=pl.BlockSpec((None, L, 2 * NA_DH), lambda p, b: (b, 0, p)),
        compiler_params=_cparams("parallel", "parallel"),
        name="na_attn",
    )(qkv, qkv, qkv, k_ctx, v_ctx, bias)


def _na_bias_table(rpb, rows):
    variants, var_of_block, kstart = _na_geometry(rows)
    nblk = rows // NA_QROWS
    expect = [0] + [1] * (nblk - 2) + [2]
    assert rows % NA_QROWS == 0 and rows >= NA_KROWS and var_of_block == expect, (rows, var_of_block)
    assert all(kstart[b] == int(np.clip(b * NA_QROWS - NA_WIN_R // 2, 0, rows - NA_KROWS))
               for b in range(nblk))
    qcol = np.arange(GRID_W)
    kcol = np.arange(GRID_W)
    cstart = np.clip(qcol - NA_WIN_C // 2, 0, GRID_W - NA_WIN_C)
    col_ok = (kcol[None, :] >= cstart[:, None]) & (kcol[None, :] < cstart[:, None] + NA_WIN_C)
    dc = np.clip(kcol[None, :] - qcol[:, None], -(NA_WIN_C - 1), NA_WIN_C - 1) + NA_WIN_C - 1
    plane = jnp.where(col_ok[None, None], rpb[:, :, dc], NEG_BIG)
    masked = jnp.full((rpb.shape[0], GRID_W, GRID_W), NEG_BIG, rpb.dtype)
    tabs = []
    for row_ok, dr in variants:
        q_rows = [jnp.concatenate([plane[:, int(dr[a, c])] if row_ok[a, c] else masked
                                   for c in range(NA_KROWS)], axis=-1) for a in range(NA_QROWS)]
        tabs.append(jnp.concatenate(q_rows, axis=1))
    return jnp.stack(tabs, axis=1)


def _moe_router_kernel(x_ref, g_ref, sc_ref, sh_ref, whi_ref, wlo_ref, br_ref,
                       h_out_ref, gid_ref):
    h = _norm_mod(x_ref[...], g_ref[...], sc_ref[...], sh_ref[...])
    h_hi, h_lo = _split_bf16(h)
    lg = (_dot_nt(whi_ref[...], h_hi) + _dot_nt(wlo_ref[...], h_hi) + _dot_nt(whi_ref[...], h_lo)
          + br_ref[...])
    tm = lg.shape[1]

    def first_max(vals):
        mx = functools.reduce(jnp.maximum, vals)
        taken = jnp.zeros((1, tm), jnp.bool_)
        hot = []
        for v in vals:
            hit = jnp.logical_and(v == mx, jnp.logical_not(taken))
            taken = jnp.logical_or(taken, hit)
            hot.append(hit)
        return mx, hot

    gl = [lg[g:g + 1, :] for g in range(MOE_G)]
    gmax, g_hit = first_max(gl)
    gsum = functools.reduce(lambda a, b: a + b, [jnp.exp(v - gmax) for v in gl])
    g_w = 1.0 / gsum
    g_hot = [jnp.where(hit, 1.0, 0.0) for hit in g_hit]
    el = []
    for e in range(MOE_E):
        acc = jnp.zeros((1, tm), F32)
        for g in range(MOE_G):
            r = MOE_G + g * MOE_E + e
            acc = acc + g_hot[g] * lg[r:r + 1, :]
        el.append(acc)
    emax = functools.reduce(jnp.maximum, el)
    ex = [jnp.exp(v - emax) for v in el]
    esum = functools.reduce(lambda a, b: a + b, ex)
    pe = [v / esum for v in ex]
    m1, hot1 = first_max(pe)
    m2, hot2 = first_max([jnp.where(hot1[e], -1.0, pe[e]) for e in range(MOE_E)])
    den = m1 + m2
    w_e = [(jnp.where(hot1[e], m1, 0.0) + jnp.where(hot2[e], m2, 0.0)) / den * g_w
           for e in range(MOE_E)]
    h_out_ref[:, :D_MODEL] = h
    w_e.append(jnp.zeros((LANES - MOE_E, tm), F32))
    h_out_ref[:, D_MODEL:] = jnp.concatenate(w_e, axis=0).T
    gid_ref[...] = functools.reduce(lambda a, b: a + b,
                                    [jnp.where(g_hit[g], g, 0) for g in range(MOE_G)])


def _moe_router(x, g, sc, sh, w_rt, b_r, tm):
    B, L, _ = x.shape
    w_hi, w_lo = _split_bf16(w_rt)
    return pl.pallas_call(
        _moe_router_kernel,
        out_shape=(jax.ShapeDtypeStruct((B, L, D_MODEL + LANES), F32),
                   jax.ShapeDtypeStruct((B, 1, L), jnp.int32)),
        grid=(B, L // tm),
        in_specs=[pl.BlockSpec((None, tm, D_MODEL), lambda b, i: (b, i, 0)),
                  _vec_spec(D_MODEL), _mod_spec(sc), _mod_spec(sh),
                  pl.BlockSpec((ROUTER_ROWS, D_MODEL), lambda b, i: (0, 0)),
                  pl.BlockSpec((ROUTER_ROWS, D_MODEL), lambda b, i: (0, 0)),
                  pl.BlockSpec((ROUTER_ROWS, 1), lambda b, i: (0, 0))],
        out_specs=(pl.BlockSpec((None, tm, D_MODEL + LANES), lambda b, i: (b, i, 0)),
                   pl.BlockSpec((None, 1, tm), lambda b, i: (b, 0, i))),
        compiler_params=_cparams("parallel", "parallel"),
        name="moe_router",
    )(x, g, sc, sh, w_hi, w_lo, b_r)


def _moe_plan(gid, tm):
    T = gid.shape[0]
    n_tiles = T // tm + MOE_G
    tp = n_tiles * tm
    order = jnp.argsort(gid, stable=True).astype(jnp.int32)
    counts = jnp.sum((gid[:, None] == jnp.arange(MOE_G)[None, :]).astype(jnp.int32), axis=0)
    padded = ((counts + tm - 1) // tm) * tm
    seg_end = jnp.cumsum(padded)
    seg_start = seg_end - padded
    first = jnp.cumsum(counts) - counts

    def pick(g_idx, table):
        return functools.reduce(lambda a, b: a + b,
                                [jnp.where(g_idx == g, table[g], 0) for g in range(MOE_G)])

    p = jnp.arange(tp, dtype=jnp.int32)
    g_of_p = functools.reduce(lambda a, b: a + b,
                              [(p >= seg_end[g]).astype(jnp.int32) for g in range(MOE_G - 1)])
    valid = p - pick(g_of_p, seg_start) < pick(g_of_p, counts)
    order_ext = jnp.concatenate([jnp.zeros((tp,), jnp.int32), order, jnp.zeros((tp,), jnp.int32)])
    tok = jnp.zeros((tp,), jnp.int32)
    for g in range(MOE_G):
        win = lax.dynamic_slice(order_ext, (tp + first[g] - seg_start[g],), (tp,))
        tok = jnp.where(g_of_p == g, win, tok)
    src = jnp.where(valid, tok, 0)
    dst = jnp.where(valid, tok, T + p % (2 * tm))
    t0 = jnp.arange(n_tiles, dtype=jnp.int32) * tm
    tile_gid = functools.reduce(lambda a, b: a + b,
                                [(t0 >= seg_end[g]).astype(jnp.int32) for g in range(MOE_G - 1)])
    tile_cnt = jnp.clip(pick(tile_gid, counts) - (t0 - pick(tile_gid, seg_start)), 0, tm)
    return (src.reshape(n_tiles, 1, tm), dst.reshape(n_tiles, 1, tm), tile_gid, tile_cnt)


def _moe_sparse_kernel(tgid_ref, tcnt_ref, src_ref, srcn_ref, dstp_ref, dst_ref, h_hbm, w1_ref, w3_ref,
                       w2_ref, y_hbm, hbuf, ybuf, gsem, ssem, *, tm):
    i = pl.program_id(0)
    n = pl.num_programs(0)
    slot = lax.rem(i, 2)
    other = 1 - slot

    def start_gather(idx_ref, s):
        for r in range(tm):
            pltpu.make_async_copy(h_hbm.at[pl.ds(idx_ref[0, r], 1)], hbuf.at[s, pl.ds(r, 1)],
                                  gsem.at[s]).start(priority=r % 2)

    def start_scatter(idx_ref, s):
        for r in range(tm):
            pltpu.make_async_copy(ybuf.at[s, pl.ds(r, 1)], y_hbm.at[pl.ds(idx_ref[0, r], 1)],
                                  ssem.at[s]).start(priority=r % 2)

    def wait_gather(s):
        pltpu.make_async_copy(h_hbm.at[pl.ds(0, tm)], hbuf.at[s], gsem.at[s]).wait()

    def wait_scatter(s):
        pltpu.make_async_copy(ybuf.at[s], y_hbm.at[pl.ds(0, tm)], ssem.at[s]).wait()

    @pl.when(i == 0)
    def _():
        ybuf[...] = jnp.zeros_like(ybuf)
        n_tok = y_hbm.shape[0] - 2 * tm
        for s in range(2):
            cp = pltpu.make_async_copy(ybuf.at[s], y_hbm.at[pl.ds(n_tok + s * tm, tm)], ssem.at[s])
            cp.start()
            cp.wait()
        start_gather(src_ref, 0)

    wait_gather(slot)

    @pl.when(i >= 1)
    def _():
        wait_scatter(slot)

    start_gather(srcn_ref, other)
    start_scatter(dstp_ref, other)
    h = hbuf[slot, :, :D_MODEL].astype(BF16)
    row = lax.broadcasted_iota(jnp.int32, (tm, LANES), 0)
    cw = jnp.where(row < tcnt_ref[i], hbuf[slot, :, D_MODEL:], 0.0)
    acc = jnp.zeros((tm, D_MODEL), F32)
    for e in range(MOE_E):
        a = _dot(h, w1_ref[e])
        b = _dot(h, w3_ref[e])
        hid = _silu(a) * b * cw[:, e:e + 1]
        acc = acc + _dot(hid.astype(BF16), w2_ref[e])
    ybuf[slot] = acc

    @pl.when(i == n - 1)
    def _():
        start_scatter(dst_ref, slot)
        wait_gather(other)
        wait_scatter(other)
        wait_scatter(slot)


def _moe_sparse(h_ext, plan, w1, w3, w2, tm):
    T = h_ext.shape[0]
    src, dst, tile_gid, tile_cnt = plan
    n_tiles = src.shape[0]
    kern = functools.partial(_moe_sparse_kernel, tm=tm)
    smem_row = lambda f: pl.BlockSpec((None, 1, tm), f, memory_space=pltpu.SMEM)
    grid_spec = pltpu.PrefetchScalarGridSpec(
        num_scalar_prefetch=2,
        grid=(n_tiles,),
        in_specs=[smem_row(lambda i, tg, tc: (i, 0, 0)),
                  smem_row(lambda i, tg, tc: (jnp.minimum(i + 1, n_tiles - 1), 0, 0)),
                  smem_row(lambda i, tg, tc: (jnp.maximum(i - 1, 0), 0, 0)),
                  smem_row(lambda i, tg, tc: (i, 0, 0)),
                  pl.BlockSpec(memory_space=pl.ANY),
                  pl.BlockSpec((None, MOE_E, D_MODEL, MOE_F), lambda i, tg, tc: (tg[i], 0, 0, 0)),
                  pl.BlockSpec((None, MOE_E, D_MODEL, MOE_F), lambda i, tg, tc: (tg[i], 0, 0, 0)),
                  pl.BlockSpec((None, MOE_E, MOE_F, D_MODEL), lambda i, tg, tc: (tg[i], 0, 0, 0))],
        out_specs=pl.BlockSpec(memory_space=pl.ANY),
        scratch_shapes=[pltpu.VMEM((2, tm, D_MODEL + LANES), F32), pltpu.VMEM((2, tm, D_MODEL), F32),
                        pltpu.SemaphoreType.DMA((2,)), pltpu.SemaphoreType.DMA((2,))])
    return pl.pallas_call(
        kern,
        out_shape=jax.ShapeDtypeStruct((T + 2 * tm, D_MODEL), F32),
        grid_spec=grid_spec,
        compiler_params=_cparams("arbitrary"),
        name="moe_sparse",
    )(tile_gid, tile_cnt, src, src, dst, dst, h_ext, w1, w3, w2)


def _moe_combine_kernel(x_ref, y_ref, gt_ref, o_ref):
    o_ref[...] = x_ref[...] + gt_ref[...] * y_ref[...]


def _moe_combine(x, y, gt, tm):
    B, L, _ = x.shape
    nb = L // tm
    return pl.pallas_call(
        _moe_combine_kernel,
        out_shape=jax.ShapeDtypeStruct((B, L, D_MODEL), F32),
        grid=(B, nb),
        in_specs=[pl.BlockSpec((None, tm, D_MODEL), lambda b, i: (b, i, 0)),
                  pl.BlockSpec((tm, D_MODEL), lambda b, i: (b * nb + i, 0)),
                  _mod_spec(gt)],
        out_specs=pl.BlockSpec((None, tm, D_MODEL), lambda b, i: (b, i, 0)),
        compiler_params=_cparams("parallel", "parallel"),
        name="moe_combine",
    )(x, y, gt)


def _final_norm_kernel(x_ref, g_ref, o_ref):
    x = x_ref[...]
    ms = jnp.mean(x * x, axis=-1, keepdims=True)
    o_ref[...] = (x * lax.rsqrt(ms + RMS_EPS)) * g_ref[...]


def _final_norm(x, g, tm):
    B, L, _ = x.shape
    return pl.pallas_call(
        _final_norm_kernel,
        out_shape=jax.ShapeDtypeStruct((B, L, D_MODEL), F32),
        grid=(B, L // tm),
        in_specs=[pl.BlockSpec((None, tm, D_MODEL), lambda b, i: (b, i, 0)), _vec_spec(D_MODEL)],
        out_specs=pl.BlockSpec((None, tm, D_MODEL), lambda b, i: (b, i, 0)),
        compiler_params=_cparams("parallel", "parallel"),
        name="final_norm",
    )(x, g)


def _row(v):
    return v.reshape(1, -1)


def _prepare_weights(P):
    W = {}
    W["cv_pw1_w"] = P["cv_pw1_w"].astype(BF16)
    W["cv_pw2_w"] = P["cv_pw2_w"].astype(BF16)
    W["gdn_w_qkvz"] = P["gdn_w_qkvz"].astype(BF16)
    W["gdn_w_ba"] = jnp.pad(P["gdn_w_ba"], ((0, 0), (0, 0), (0, LANES - 4 * GDN_H)))
    a_neg = jnp.exp(P["gdn_a_log"].astype(F32))
    zeros = jnp.zeros_like(a_neg)
    a_cols = jnp.stack([zeros, a_neg], axis=2).reshape(a_neg.shape[0], 4 * GDN_H)
    dt_cols = jnp.stack([zeros, P["gdn_dt_bias"].astype(F32)], axis=2).reshape(a_neg.shape[0], 4 * GDN_H)
    W["gdn_a_row"] = jnp.pad(a_cols, ((0, 0), (0, LANES - 4 * GDN_H)))
    W["gdn_dt_row"] = jnp.pad(dt_cols, ((0, 0), (0, LANES - 4 * GDN_H)))
    W["gdn_w_o"] = P["gdn_w_o"].astype(BF16)
    W["na_w_qkv"] = P["na_w_qkv"].astype(BF16)
    W["na_w_o"] = P["na_w_o"].astype(BF16)
    W["na_rpb"] = P["na_rpb"]
    W["moe_w1"] = P["moe_w1"].astype(BF16)
    W["moe_w3"] = P["moe_w3"].astype(BF16)
    W["moe_w2"] = P["moe_w2"].astype(BF16)
    w_r = jnp.concatenate([P["moe_wg"], P["moe_we"]], axis=-1)
    n_r = w_r.shape[-1]
    W["moe_w_rt"] = jnp.pad(w_r.transpose(0, 2, 1), ((0, 0), (0, ROUTER_ROWS - n_r), (0, 0)))
    b_r = jnp.concatenate([P["moe_bg"], P["moe_be"]], axis=-1)
    W["moe_b_r"] = jnp.pad(b_r, ((0, 0), (0, ROUTER_ROWS - n_r)))[:, :, None]
    return W


def _trunk(x, mods, P, W, cache):
    B, L, _ = x.shape
    latent = cache is not None
    Bm = mods.shape[1]
    flat = Bm == 1
    tm_seq = min(L, 256)
    n_rows = B * L if flat else L
    tm_tok = min(512, n_rows)
    tm_big = min(1024, n_rows)
    gdn_states, na_ks, na_vs = [], [], []

    def tok(a):
        return a.reshape(1, B * L, a.shape[-1]) if flat else a

    def seq(a):
        return a.reshape(B, L, a.shape[-1])

    for i in range(DEPTH):
        m = mods[i].reshape(Bm, 1, ADA_CHUNKS, D_MODEL)
        sh1, sc1, gt1, sh2, sc2, gt2 = (m[:, :, c] for c in range(ADA_CHUNKS))
        g1 = _row(P["norm1_g"][i])
        kind, j = i % N_MIXERS, i // N_MIXERS
        if kind == 0:
            u = _nm_glu(tok(x), g1, sc1, sh1, W["cv_pw1_w"][j], _row(P["cv_pw1_b"][j]), tm_big, 512)
            x = _conv_tail(seq(u), P["cv_dw_w"][j], _row(P["cv_dw_b"][j]), _row(P["cv_ln_g"][j]),
                           _row(P["cv_ln_b"][j]), W["cv_pw2_w"][j], _row(P["cv_pw2_b"][j]),
                           seq(x), gt1, tm_seq)
        elif kind == 1:
            qkvz = _nm_linear(tok(x), g1, sc1, sh1, W["gdn_w_qkvz"][j], tm_big, 1024)
            ba = _nm_small(tok(x), g1, sc1, sh1, W["gdn_w_ba"][j], tm_tok)
            qkv = _gdn_prep(seq(qkvz), P["gdn_conv_w"][j], min(L, tm_tok))
            if latent:
                s0 = cache[0][:, j].astype(F32)
            else:
                s0 = jnp.zeros((B, 2, GDN_H, GDN_DK, GDN_DK), F32)
            a_row, dt_row = W["gdn_a_row"][j:j + 1], W["gdn_dt_row"][j:j + 1]
            o_f, s_f = _gdn_chunk(qkv, seq(ba), a_row, dt_row, s0[:, 0], rev=False)
            o_b, s_b = _gdn_chunk(qkv, seq(ba), a_row, dt_row, s0[:, 1], rev=True)
            gdn_states.append(jnp.stack([s_f, s_b], axis=1))
            x = _gdn_out(tok(o_f), tok(o_b), qkvz, _row(P["gdn_o_g"][j]), W["gdn_w_o"][j],
                         tok(x), gt1, tm_tok)
        else:
            qkv = _nm_linear(tok(x), g1, sc1, sh1, W["na_w_qkv"][j], tm_big, 1024)
            if latent:
                bias = _na_bias_table(W["na_rpb"][j], L // GRID_W)
                o = _na_attn(seq(qkv), cache[1][:, j], cache[2][:, j], bias)
            else:
                o, k_ctx, v_ctx = _ctx_attn(seq(qkv))
                na_ks.append(k_ctx)
                na_vs.append(v_ctx)
            x = _linear_res(tok(o), W["na_w_o"][j], tok(x), gt1, tm_tok)
        h2, gid = _moe_router(tok(x), _row(P["norm2_g"][i]), sc2, sh2, W["moe_w_rt"][i],
                              W["moe_b_r"][i], tm_tok)
        plan = _moe_plan(gid.reshape(B * L), MOE_TILE)
        y = _moe_sparse(h2.reshape(B * L, D_MODEL + LANES), plan, W["moe_w1"][i], W["moe_w3"][i],
                        W["moe_w2"][i], MOE_TILE)
        x = seq(_moe_combine(tok(x), y, gt2, tm_tok))
    y = seq(_final_norm(tok(x), _row(P["final_norm_g"]), tm_tok))
    return y, gdn_states, na_ks, na_vs


def kernel(x_prompt, x_sample, state_gdn, cache_na_k, cache_na_v, c, c_ctx,
           ada_w, ada_b, norm1_g, norm2_g,
           cv_pw1_w, cv_pw1_b, cv_dw_w, cv_dw_b, cv_ln_g, cv_ln_b, cv_pw2_w, cv_pw2_b,
           gdn_w_qkvz, gdn_conv_w, gdn_w_ba, gdn_a_log, gdn_dt_bias, gdn_o_g, gdn_w_o,
           na_w_qkv, na_rpb, na_w_o,
           moe_wg, moe_bg, moe_we, moe_be, moe_w1, moe_w3, moe_w2,
           final_norm_g):
    P = dict(norm1_g=norm1_g, norm2_g=norm2_g,
             cv_pw1_w=cv_pw1_w, cv_pw1_b=cv_pw1_b, cv_dw_w=cv_dw_w, cv_dw_b=cv_dw_b,
             cv_ln_g=cv_ln_g, cv_ln_b=cv_ln_b, cv_pw2_w=cv_pw2_w, cv_pw2_b=cv_pw2_b,
             gdn_w_qkvz=gdn_w_qkvz, gdn_conv_w=gdn_conv_w, gdn_w_ba=gdn_w_ba, gdn_a_log=gdn_a_log,
             gdn_dt_bias=gdn_dt_bias, gdn_o_g=gdn_o_g, gdn_w_o=gdn_w_o,
             na_w_qkv=na_w_qkv, na_rpb=na_rpb, na_w_o=na_w_o,
             moe_wg=moe_wg, moe_bg=moe_bg, moe_we=moe_we, moe_be=moe_be,
             moe_w1=moe_w1, moe_w3=moe_w3, moe_w2=moe_w2, final_norm_g=final_norm_g)
    W = _prepare_weights(P)
    nb = c.shape[0]
    rows = 8 * ((nb + 1 + 7) // 8)
    cond = jnp.concatenate([c, c_ctx[None, :], jnp.zeros((rows - nb - 1, D_MODEL), F32)], axis=0)
    mods = _ada_all(cond, ada_w, ada_b)
    y_prompt, gdn_states, na_ks, na_vs = _trunk(x_prompt, mods[:, nb:nb + 1], P, W, None)
    y_sample, _, _, _ = _trunk(x_sample, mods[:, :nb], P, W, (state_gdn, cache_na_k, cache_na_v))
    return (y_prompt, y_sample, jnp.stack(gdn_states, axis=1),
            jnp.stack(na_ks, axis=1), jnp.stack(na_vs, axis=1))
```

```python
import functools

import numpy as np
import jax
import jax.numpy as jnp
from jax import lax
from jax.experimental import pallas as pl
from jax.experimental.pallas import tpu as pltpu

F32 = jnp.float32
BF16 = jnp.bfloat16

D_MODEL = 1024
DEPTH = 4
N_MIXERS = 3
ADA_CHUNKS = 6
RMS_EPS = 1e-6
LN_EPS = 1e-5
CONV_WIDTH = 31
CONV_HALO = 16
GDN_H = 8
GDN_DK = 128
GDN_CONV = 5
GDN_HALO = 8
GDN_BLOCK = 128
GDN_SUB = 16
NA_HEADS = 16
NA_DH = 64
NA_WIN_R = 8
NA_WIN_C = 16
NA_QROWS = 4
NA_KROWS = 12
GRID_W = 64
MOE_G = 4
MOE_E = 4
MOE_F = 256
MOE_TILE = 512
ROUTER_ROWS = 32
LANES = 128
SUBLANES = 8
CONV_TAPS_PAD = -(-CONV_WIDTH // SUBLANES) * SUBLANES
CONV_ROW_CHUNK = 64
TN_WIDE = 1024
TN_GLU = 512
NEG_BIG = -1e30

VMEM_LIMIT_BYTES = 48 * 1024 * 1024


def _cparams(*sem):
    return pltpu.CompilerParams(dimension_semantics=sem, vmem_limit_bytes=VMEM_LIMIT_BYTES)


def _sigmoid(x):
    return 1.0 / (1.0 + jnp.exp(-x))


def _silu(x):
    return x * _sigmoid(x)


def _norm_mod(x, g, sc, sh):
    ms = jnp.mean(x * x, axis=-1, keepdims=True)
    y = x * lax.rsqrt(ms + RMS_EPS)
    return (y * g) * (1.0 + sc) + sh


def _dot(a, b):
    return jnp.dot(a, b, preferred_element_type=F32)


def _dot_nt(a, b):
    return lax.dot_general(a, b, (((1,), (1,)), ((), ())), preferred_element_type=F32)


def _dot_f32(a, b):
    return jnp.dot(a, b, preferred_element_type=F32, precision=lax.Precision.HIGHEST)


def _split_bf16(w):
    hi = w.astype(BF16)
    lo = (w - hi.astype(F32)).astype(BF16)
    return hi, lo


def _mod_spec(mod):
    if mod.shape[0] == 1:
        return pl.BlockSpec((None, 1, D_MODEL), lambda b, *_: (0, 0, 0))
    return pl.BlockSpec((None, 1, D_MODEL), lambda b, *_: (b, 0, 0))


def _vec_spec(n):
    return pl.BlockSpec((1, n), lambda *_: (0, 0))


def _ada_kernel(c_ref, w_ref, b_ref, o_ref):
    s = _silu(c_ref[...]).astype(BF16)
    o_ref[...] = _dot(s, w_ref[...].astype(BF16)) + b_ref[...]


def _ada_all(cond, ada_w, ada_b, tn=TN_WIDE):
    R = cond.shape[0]
    N = ada_w.shape[-1]
    return pl.pallas_call(
        _ada_kernel,
        out_shape=jax.ShapeDtypeStruct((DEPTH, R, N), F32),
        grid=(DEPTH, N // tn),
        in_specs=[pl.BlockSpec((R, D_MODEL), lambda l, j: (0, 0)),
                  pl.BlockSpec((None, D_MODEL, tn), lambda l, j: (l, 0, j)),
                  pl.BlockSpec((None, 1, tn), lambda l, j: (l, 0, j))],
        out_specs=pl.BlockSpec((None, R, tn), lambda l, j: (l, 0, j)),
        compiler_params=_cparams("parallel", "parallel"),
        name="ada",
    )(cond, ada_w, ada_b.reshape(DEPTH, 1, N))


def _nm_linear_kernel(x_ref, g_ref, sc_ref, sh_ref, w_ref, o_ref, h_ref):
    @pl.when(pl.program_id(2) == 0)
    def _():
        h_ref[...] = _norm_mod(x_ref[...], g_ref[...], sc_ref[...], sh_ref[...]).astype(BF16)

    o_ref[...] = _dot(h_ref[...], w_ref[...]).astype(o_ref.dtype)


def _nm_linear(x, g, sc, sh, w, tm, tn):
    B, L, _ = x.shape
    N = w.shape[1]
    return pl.pallas_call(
        _nm_linear_kernel,
        out_shape=jax.ShapeDtypeStruct((B, L, N), F32),
        grid=(B, L // tm, N // tn),
        in_specs=[pl.BlockSpec((None, tm, D_MODEL), lambda b, i, j: (b, i, 0)),
                  _vec_spec(D_MODEL), _mod_spec(sc), _mod_spec(sh),
                  pl.BlockSpec((D_MODEL, tn), lambda b, i, j: (0, j))],
        out_specs=pl.BlockSpec((None, tm, tn), lambda b, i, j: (b, i, j)),
        scratch_shapes=[pltpu.VMEM((tm, D_MODEL), BF16)],
        compiler_params=_cparams("parallel", "parallel", "arbitrary"),
        name="nm_linear",
    )(x, g, sc, sh, w)


def _nm_glu_kernel(x_ref, g_ref, sc_ref, sh_ref, wa_ref, wg_ref, ba_ref, bg_ref, o_ref, h_ref):
    @pl.when(pl.program_id(2) == 0)
    def _():
        h_ref[...] = _norm_mod(x_ref[...], g_ref[...], sc_ref[...], sh_ref[...]).astype(BF16)

    h = h_ref[...]
    a = _dot(h, wa_ref[...]) + ba_ref[...]
    gate = _dot(h, wg_ref[...]) + bg_ref[...]
    o_ref[...] = a * _sigmoid(gate)


def _nm_glu(x, g, sc, sh, w, b, tm, tn):
    B, L, _ = x.shape
    N = w.shape[1] // 2
    nj = N // tn
    return pl.pallas_call(
        _nm_glu_kernel,
        out_shape=jax.ShapeDtypeStruct((B, L, N), F32),
        grid=(B, L // tm, nj),
        in_specs=[pl.BlockSpec((None, tm, D_MODEL), lambda b_, i, j: (b_, i, 0)),
                  _vec_spec(D_MODEL), _mod_spec(sc), _mod_spec(sh),
                  pl.BlockSpec((D_MODEL, tn), lambda b_, i, j: (0, j)),
                  pl.BlockSpec((D_MODEL, tn), lambda b_, i, j: (0, j + nj)),
                  pl.BlockSpec((1, tn), lambda b_, i, j: (0, j)),
                  pl.BlockSpec((1, tn), lambda b_, i, j: (0, j + nj))],
        out_specs=pl.BlockSpec((None, tm, tn), lambda b_, i, j: (b_, i, j)),
        scratch_shapes=[pltpu.VMEM((tm, D_MODEL), BF16)],
        compiler_params=_cparams("parallel", "parallel", "arbitrary"),
        name="nm_glu",
    )(x, g, sc, sh, w, w, b, b)


def _nm_small_kernel(x_ref, g_ref, sc_ref, sh_ref, whi_ref, wlo_ref, o_ref):
    h = _norm_mod(x_ref[...], g_ref[...], sc_ref[...], sh_ref[...])
    h_hi, h_lo = _split_bf16(h)
    o_ref[...] = _dot(h_hi, whi_ref[...]) + _dot(h_hi, wlo_ref[...]) + _dot(h_lo, whi_ref[...])


def _nm_small(x, g, sc, sh, w, tm):
    B, L, _ = x.shape
    w_hi, w_lo = _split_bf16(w)
    return pl.pallas_call(
        _nm_small_kernel,
        out_shape=jax.ShapeDtypeStruct((B, L, LANES), F32),
        grid=(B, L // tm),
        in_specs=[pl.BlockSpec((None, tm, D_MODEL), lambda b, i: (b, i, 0)),
                  _vec_spec(D_MODEL), _mod_spec(sc), _mod_spec(sh),
                  pl.BlockSpec((D_MODEL, LANES), lambda b, i: (0, 0)),
                  pl.BlockSpec((D_MODEL, LANES), lambda b, i: (0, 0))],
        out_specs=pl.BlockSpec((None, tm, LANES), lambda b, i: (b, i, 0)),
        compiler_params=_cparams("parallel", "parallel"),
        name="nm_small",
    )(x, g, sc, sh, w_hi, w_lo)


def _linear_res_kernel(a_ref, w_ref, x_ref, gt_ref, o_ref):
    y = _dot(a_ref[...].astype(BF16), w_ref[...])
    o_ref[...] = x_ref[...] + gt_ref[...] * y


def _linear_res(a, w, x, gt, tm):
    B, L, K = a.shape
    return pl.pallas_call(
        _linear_res_kernel,
        out_shape=jax.ShapeDtypeStruct((B, L, D_MODEL), F32),
        grid=(B, L // tm),
        in_specs=[pl.BlockSpec((None, tm, K), lambda b, i: (b, i, 0)),
                  pl.BlockSpec((K, D_MODEL), lambda b, i: (0, 0)),
                  pl.BlockSpec((None, tm, D_MODEL), lambda b, i: (b, i, 0)),
                  _mod_spec(gt)],
        out_specs=pl.BlockSpec((None, tm, D_MODEL), lambda b, i: (b, i, 0)),
        compiler_params=_cparams("parallel", "parallel"),
        name="linear_res",
    )(a, w, x, gt)


def _conv_tail_kernel(u_ref, up_ref, un_ref, dww_ref, dwb_ref, lng_ref, lnb_ref, w2_ref, b2_ref,
                      x_ref, gt_ref, o_ref, buf_ref, sh_ref, cv_ref, *, tm, rc):
    i = pl.program_id(1)
    last = pl.num_programs(1) - 1
    buf_ref[0:CONV_HALO, :] = jnp.where(i > 0, up_ref[...], 0.0)
    buf_ref[CONV_HALO:CONV_HALO + tm, :] = u_ref[...]
    buf_ref[CONV_HALO + tm:, :] = jnp.where(i < last, un_ref[...], 0.0)
    off = CONV_HALO - CONV_WIDTH // 2
    nsh = sh_ref.shape[1]
    for b in range(SUBLANES):
        sh_ref[b] = buf_ref[b:b + nsh, :]
    for c in range(D_MODEL // LANES):
        cs = slice(c * LANES, (c + 1) * LANES)
        for r0 in range(0, tm, rc):
            acc = jnp.zeros((rc, LANES), F32)
            for k in range(CONV_WIDTH):
                a, b = divmod(off + k, SUBLANES)
                acc = acc + dww_ref[k:k + 1, cs] * sh_ref[b, r0 + a * SUBLANES:r0 + a * SUBLANES + rc, cs]
            cv_ref[r0:r0 + rc, cs] = acc + dwb_ref[:, cs]
    v = cv_ref[...]
    mu = jnp.mean(v, axis=-1, keepdims=True)
    vc = v - mu
    var = jnp.mean(vc * vc, axis=-1, keepdims=True)
    y = _silu(vc * lax.rsqrt(var + LN_EPS) * lng_ref[...] + lnb_ref[...])
    z = _dot(y.astype(BF16), w2_ref[...]) + b2_ref[...]
    o_ref[...] = x_ref[...] + gt_ref[...] * z


def _conv_tail(u, dw_w, dw_b, ln_g, ln_b, w2, b2, x, gt, tm):
    B, L, _ = u.shape
    hb = tm // CONV_HALO
    nh = L // CONV_HALO
    dww = jnp.pad(dw_w, ((0, CONV_TAPS_PAD - CONV_WIDTH), (0, 0)))
    last_tap = CONV_HALO - CONV_WIDTH // 2 + CONV_WIDTH - 1
    kern = functools.partial(_conv_tail_kernel, tm=tm, rc=CONV_ROW_CHUNK)
    return pl.pallas_call(
        kern,
        out_shape=jax.ShapeDtypeStruct((B, L, D_MODEL), F32),
        grid=(B, L // tm),
        in_specs=[pl.BlockSpec((None, tm, D_MODEL), lambda b, i: (b, i, 0)),
                  pl.BlockSpec((None, CONV_HALO, D_MODEL),
                               lambda b, i: (b, jnp.maximum(i * hb - 1, 0), 0)),
                  pl.BlockSpec((None, CONV_HALO, D_MODEL),
                               lambda b, i: (b, jnp.minimum((i + 1) * hb, nh - 1), 0)),
                  pl.BlockSpec((CONV_TAPS_PAD, D_MODEL), lambda b, i: (0, 0)),
                  _vec_spec(D_MODEL), _vec_spec(D_MODEL), _vec_spec(D_MODEL),
                  pl.BlockSpec((D_MODEL, D_MODEL), lambda b, i: (0, 0)),
                  _vec_spec(D_MODEL),
                  pl.BlockSpec((None, tm, D_MODEL), lambda b, i: (b, i, 0)),
                  _mod_spec(gt)],
        out_specs=pl.BlockSpec((None, tm, D_MODEL), lambda b, i: (b, i, 0)),
        scratch_shapes=[pltpu.VMEM((tm + 2 * CONV_HALO, D_MODEL), F32),
                        pltpu.VMEM((SUBLANES, tm + SUBLANES * (last_tap // SUBLANES), D_MODEL), F32),
                        pltpu.VMEM((tm, D_MODEL), F32)],
        compiler_params=_cparams("parallel", "parallel"),
        name="conv_tail",
    )(u, u, u, dww, dw_b, ln_g, ln_b, w2, b2, x, gt)


def _gdn_prep_kernel(u_ref, up_ref, un_ref, w_ref, o_ref, buf_ref, *, tm):
    i = pl.program_id(1)
    j = pl.program_id(2)
    last = pl.num_programs(1) - 1
    buf_ref[0:GDN_HALO, :] = jnp.where(i > 0, up_ref[...], 0.0)
    buf_ref[GDN_HALO:GDN_HALO + tm, :] = u_ref[...]
    buf_ref[GDN_HALO + tm:, :] = jnp.where(i < last, un_ref[...], 0.0)
    off = GDN_HALO - GDN_CONV // 2
    for h in range(GDN_H):
        cs = slice(h * GDN_DK, (h + 1) * GDN_DK)
        acc = jnp.zeros((tm, GDN_DK), F32)
        for k in range(GDN_CONV):
            acc = acc + w_ref[k:k + 1, cs] * buf_ref[off + k:off + k + tm, cs]
        y = _silu(acc)
        inv = lax.rsqrt(jnp.sum(y * y, axis=-1, keepdims=True) + 1e-6)
        f = jnp.where(j < 2, inv, 1.0) * jnp.where(j == 0, GDN_DK ** -0.5, 1.0)
        o_ref[:, cs] = y * f


def _gdn_prep(qkvz, conv_w, tm):
    B, L, _ = qkvz.shape
    W = GDN_H * GDN_DK
    hb = tm // GDN_HALO
    nh = L // GDN_HALO
    cw = jnp.pad(conv_w, ((0, SUBLANES - GDN_CONV), (0, 0)))
    kern = functools.partial(_gdn_prep_kernel, tm=tm)
    return pl.pallas_call(
        kern,
        out_shape=jax.ShapeDtypeStruct((B, L, 3 * W), F32),
        grid=(B, L // tm, 3),
        in_specs=[pl.BlockSpec((None, tm, W), lambda b, i, j: (b, i, j)),
                  pl.BlockSpec((None, GDN_HALO, W), lambda b, i, j: (b, jnp.maximum(i * hb - 1, 0), j)),
                  pl.BlockSpec((None, GDN_HALO, W),
                               lambda b, i, j: (b, jnp.minimum((i + 1) * hb, nh - 1), j)),
                  pl.BlockSpec((SUBLANES, W), lambda b, i, j: (0, j))],
        out_specs=pl.BlockSpec((None, tm, W), lambda b, i, j: (b, i, j)),
        scratch_shapes=[pltpu.VMEM((tm + 2 * GDN_HALO, W), F32)],
        compiler_params=_cparams("parallel", "parallel", "arbitrary"),
        name="gdn_prep",
    )(qkvz, qkvz, qkvz, cw)


def _split3(x):
    hi = x.astype(BF16)
    r1 = x - hi.astype(F32)
    mid = r1.astype(BF16)
    lo = (r1 - mid.astype(F32)).astype(BF16)
    return hi, mid, lo


def _mm3(a, b):
    lhs = jnp.concatenate([a[0], a[1]], axis=1)
    rhs = jnp.concatenate([b[0], b[0]], axis=0)
    return _dot(lhs, rhs) + _dot(a[0], b[1])


def _gdn_chunk_kernel(q_ref, k_ref, v_ref, ba_ref, a_ref, dtb_ref, s0_ref, o_ref, sfin_ref, s_ref,
                      *, rev, col0):
    n = pl.program_id(1)
    C = GDN_BLOCK

    @pl.when(n == 0)
    def _():
        s_ref[...] = s0_ref[...]

    ri = lax.broadcasted_iota(jnp.int32, (C, C), 0)
    ci = lax.broadcasted_iota(jnp.int32, (C, C), 1)
    if rev:
        incl, strict = ri <= ci, ri < ci
    else:
        incl, strict = ri >= ci, ri > ci
    blk = (ri // GDN_SUB) == (ci // GDN_SUB)
    eye_f = jnp.where(ri == ci, 1.0, 0.0)

    ba = ba_ref[...]
    beta_all = _sigmoid(ba)
    xa = ba + dtb_ref[...]
    softplus = jnp.maximum(xa, 0.0) + jnp.log(1.0 + jnp.exp(-jnp.abs(xa)))
    g_all = -a_ref[...] * softplus
    tri = jnp.where(incl, 1.0, 0.0).astype(BF16)
    g_hi, g_mid, g_lo = _split3(g_all)
    cum_all = _dot(tri, g_hi) + _dot(tri, g_mid) + _dot(tri, g_lo)
    cum_t = cum_all.T
    tot_row = cum_all[0:1, :] if rev else cum_all[C - 1:C, :]
    ecum_all = jnp.exp(cum_all)
    etail_all = jnp.exp(tot_row - cum_all)
    etot_row = jnp.exp(tot_row)

    HS = range(GDN_H)
    cs = [slice(h * GDN_DK, (h + 1) * GDN_DK) for h in HS]
    cg = [col0 + GDN_H + h for h in HS]
    k = [k_ref[:, cs[h]] for h in HS]
    kbf = [k[h].astype(BF16) for h in HS]
    beta = [beta_all[:, col0 + h:col0 + h + 1] for h in HS]
    kb = [k[h] * beta[h] for h in HS]
    decay = [jnp.where(incl, jnp.exp(jnp.where(
        incl, cum_all[:, cg[h]:cg[h] + 1] - cum_t[cg[h]:cg[h] + 1, :], 0.0)), 0.0) for h in HS]
    nm = [jnp.where(strict, _dot_nt(kb[h].astype(BF16), kbf[h]) * decay[h], 0.0) for h in HS]
    nd = [jnp.where(blk, nm[h], 0.0) for h in HS]
    nd_s = [_split_bf16(nd[h]) for h in HS]
    nd2_s = [_split_bf16(_mm3(nd_s[h], nd_s[h])) for h in HS]
    nd4_s = [_split_bf16(_mm3(nd2_s[h], nd2_s[h])) for h in HS]
    nd8_s = [_split_bf16(_mm3(nd4_s[h], nd4_s[h])) for h in HS]
    t = [eye_f - nd[h] for h in HS]
    t = [t[h] + _mm3(_split_bf16(t[h]), nd2_s[h]) for h in HS]
    t = [t[h] + _mm3(_split_bf16(t[h]), nd4_s[h]) for h in HS]
    t = [t[h] + _mm3(_split_bf16(t[h]), nd8_s[h]) for h in HS]
    t_s = [_split_bf16(t[h]) for h in HS]
    m_s = [_split_bf16(_mm3(t_s[h], _split_bf16(nm[h] - nd[h]))) for h in HS]
    m2_s = [_split_bf16(_mm3(m_s[h], m_s[h])) for h in HS]
    m4_s = [_split_bf16(_mm3(m2_s[h], m2_s[h])) for h in HS]
    a_inv = [t[h] + _mm3(m4_s[h], t_s[h]) for h in HS]
    a_inv = [a_inv[h] + _mm3(m2_s[h], _split_bf16(a_inv[h])) for h in HS]
    a_inv = [a_inv[h] - _mm3(m_s[h], _split_bf16(a_inv[h])) for h in HS]
    e_cum = [ecum_all[:, cg[h]:cg[h] + 1] for h in HS]
    rhs = [jnp.concatenate([v_ref[:, cs[h]] * beta[h], kb[h] * e_cum[h]], axis=-1) for h in HS]
    sol = [_mm3(_split_bf16(a_inv[h]), _split_bf16(rhs[h])) for h in HS]
    q = [q_ref[:, cs[h]] for h in HS]
    qk = [jnp.where(incl, _dot_nt(q[h].astype(BF16), kbf[h]) * decay[h], 0.0).astype(BF16)
          for h in HS]
    q_g = [(q[h] * e_cum[h]).astype(BF16) for h in HS]
    k_tail_t = [(k[h] * etail_all[:, cg[h]:cg[h] + 1]).T.astype(BF16) for h in HS]
    s = [s_ref[h] for h in HS]
    sb = [s[h].astype(BF16) for h in HS]
    vb = [(sol[h][:, :GDN_DK] - _dot(sol[h][:, GDN_DK:].astype(BF16), sb[h])).astype(BF16)
          for h in HS]
    for h in HS:
        o_ref[:, cs[h]] = _dot(q_g[h], sb[h]) + _dot(qk[h], vb[h])
    for h in HS:
        s_ref[h] = s[h] * etot_row[:, cg[h]:cg[h] + 1] + _dot(k_tail_t[h], vb[h])

    @pl.when(n == pl.num_programs(1) - 1)
    def _():
        sfin_ref[...] = s_ref[...]


def _gdn_chunk(qkv, ba, a_row, dtb_row, s0, rev):
    B, L, _ = qkv.shape
    W = GDN_H * GDN_DK
    nc = L // GDN_BLOCK
    cidx = (lambda n: nc - 1 - n) if rev else (lambda n: n)
    kern = functools.partial(_gdn_chunk_kernel, rev=rev, col0=2 * GDN_H if rev else 0)
    return pl.pallas_call(
        kern,
        out_shape=(jax.ShapeDtypeStruct((B, L, W), F32),
                   jax.ShapeDtypeStruct((B, GDN_H, GDN_DK, GDN_DK), F32)),
        grid=(B, nc),
        in_specs=[pl.BlockSpec((None, GDN_BLOCK, W), lambda b, n: (b, cidx(n), 0)),
                  pl.BlockSpec((None, GDN_BLOCK, W), lambda b, n: (b, cidx(n), 1)),
                  pl.BlockSpec((None, GDN_BLOCK, W), lambda b, n: (b, cidx(n), 2)),
                  pl.BlockSpec((None, GDN_BLOCK, LANES), lambda b, n: (b, cidx(n), 0)),
                  _vec_spec(LANES), _vec_spec(LANES),
                  pl.BlockSpec((None, GDN_H, GDN_DK, GDN_DK), lambda b, n: (b, 0, 0, 0))],
        out_specs=(pl.BlockSpec((None, GDN_BLOCK, W), lambda b, n: (b, cidx(n), 0)),
                   pl.BlockSpec((None, GDN_H, GDN_DK, GDN_DK), lambda b, n: (b, 0, 0, 0))),
        scratch_shapes=[pltpu.VMEM((GDN_H, GDN_DK, GDN_DK), F32)],
        compiler_params=_cparams("parallel", "arbitrary"),
        name="gdn_chunk_rev" if rev else "gdn_chunk_fwd",
    )(qkv, qkv, qkv, ba, a_row, dtb_row, s0)


def _gdn_out_kernel(of_ref, ob_ref, z_ref, og_ref, w_ref, x_ref, gt_ref, o_ref, y_ref):
    o = of_ref[...] + ob_ref[...]
    z = z_ref[...]
    for h in range(GDN_H):
        cs = slice(h * GDN_DK, (h + 1) * GDN_DK)
        oh = o[:, cs]
        ms = jnp.mean(oh * oh, axis=-1, keepdims=True)
        y = (oh * lax.rsqrt(ms + RMS_EPS)) * og_ref[...]
        y_ref[:, cs] = (y * _silu(z[:, cs])).astype(BF16)
    o_ref[...] = x_ref[...] + gt_ref[...] * _dot(y_ref[...], w_ref[...])


def _gdn_out(o_f, o_b, qkvz, o_g, w_o, x, gt, tm):
    B, L, W = o_f.shape
    return pl.pallas_call(
        _gdn_out_kernel,
        out_shape=jax.ShapeDtypeStruct((B, L, D_MODEL), F32),
        grid=(B, L // tm),
        in_specs=[pl.BlockSpec((None, tm, W), lambda b, i: (b, i, 0)),
                  pl.BlockSpec((None, tm, W), lambda b, i: (b, i, 0)),
                  pl.BlockSpec((None, tm, W), lambda b, i: (b, i, 3)),
                  _vec_spec(GDN_DK),
                  pl.BlockSpec((W, D_MODEL), lambda b, i: (0, 0)),
                  pl.BlockSpec((None, tm, D_MODEL), lambda b, i: (b, i, 0)),
                  _mod_spec(gt)],
        out_specs=pl.BlockSpec((None, tm, D_MODEL), lambda b, i: (b, i, 0)),
        scratch_shapes=[pltpu.VMEM((tm, W), BF16)],
        compiler_params=_cparams("parallel", "parallel"),
        name="gdn_out",
    )(o_f, o_b, qkvz, o_g, w_o, x, gt)


def _ctx_attn_kernel(q_ref, k_ref, v_ref, o_ref, ko_ref, vo_ref):
    scale = NA_DH ** -0.5
    HS = range(NA_HEADS)
    cs = [slice(h * NA_DH, (h + 1) * NA_DH) for h in HS]
    k = [k_ref[:, cs[h]] for h in HS]
    v = [v_ref[:, cs[h]] for h in HS]
    s = [_dot_nt((q_ref[:, cs[h]] * scale).astype(BF16), k[h].astype(BF16)) for h in HS]
    p = [jnp.exp(s[h] - jnp.max(s[h], axis=-1, keepdims=True)) for h in HS]
    p = [p[h] / jnp.sum(p[h], axis=-1, keepdims=True) for h in HS]
    o = [_dot(p[h].astype(BF16), v[h].astype(BF16)) for h in HS]
    for h in range(0, NA_HEADS, 2):
        o_ref[:, h * NA_DH:(h + 2) * NA_DH] = jnp.concatenate([o[h], o[h + 1]], axis=-1)
    for h in HS:
        ko_ref[h] = k[h]
        vo_ref[h] = v[h]


def _ctx_attn(qkv):
    B, L, _ = qkv.shape
    kv_shape = jax.ShapeDtypeStruct((B, NA_HEADS, L, NA_DH), F32)
    return pl.pallas_call(
        _ctx_attn_kernel,
        out_shape=(jax.ShapeDtypeStruct((B, L, D_MODEL), F32), kv_shape, kv_shape),
        grid=(B,),
        in_specs=[pl.BlockSpec((None, L, D_MODEL), lambda b: (b, 0, 0)),
                  pl.BlockSpec((None, L, D_MODEL), lambda b: (b, 0, 1)),
                  pl.BlockSpec((None, L, D_MODEL), lambda b: (b, 0, 2))],
        out_specs=(pl.BlockSpec((None, L, D_MODEL), lambda b: (b, 0, 0)),
                   pl.BlockSpec((None, NA_HEADS, L, NA_DH), lambda b: (b, 0, 0, 0)),
                   pl.BlockSpec((None, NA_HEADS, L, NA_DH), lambda b: (b, 0, 0, 0))),
        compiler_params=_cparams("parallel"),
        name="ctx_attn",
    )(qkv, qkv, qkv)


def _na_geometry(rows):
    nblk = rows // NA_QROWS
    kr = min(NA_WIN_R, rows)
    variants, var_of_block, kstart = [], [], []
    for b in range(nblk):
        r0 = b * NA_QROWS
        ks = int(np.clip(r0 - NA_WIN_R // 2, 0, rows - NA_KROWS))
        qr = r0 + np.arange(NA_QROWS)
        rs = np.clip(qr - kr // 2, 0, rows - kr)
        key_row = ks + np.arange(NA_KROWS)
        ok = (key_row[None, :] >= rs[:, None]) & (key_row[None, :] < rs[:, None] + kr)
        dr = np.where(ok, key_row[None, :] - qr[:, None] + NA_WIN_R - 1, 0)
        geo = (ok.tobytes(), dr.tobytes())
        if geo not in [g for g, _, _ in variants]:
            variants.append((geo, ok, dr))
        var_of_block.append([g for g, _, _ in variants].index(geo))
        kstart.append(ks)
    return [(ok, dr) for _, ok, dr in variants], var_of_block, kstart


def _na_attn_kernel(q_ref, k_ref, v_ref, kc_ref, vc_ref, bias_ref, o_ref, *, rows):
    scale = NA_DH ** -0.5
    nblk = rows // NA_QROWS
    nq = NA_QROWS * GRID_W
    nk = NA_KROWS * GRID_W
    kc = [kc_ref[hh].astype(BF16) for hh in range(2)]
    vc = [vc_ref[hh].astype(BF16) for hh in range(2)]

    def body(blk, carry):
        ks = jnp.clip(blk * NA_QROWS - NA_WIN_R // 2, 0, rows - NA_KROWS)
        var = jnp.where(blk == 0, 0, jnp.where(blk == nblk - 1, 2, 1))
        q0 = pl.multiple_of(blk * nq, nq)
        k0 = pl.multiple_of(ks * GRID_W, GRID_W)
        HH = range(2)
        cs = [slice(hh * NA_DH, (hh + 1) * NA_DH) for hh in HH]
        q = [(q_ref[pl.ds(q0, nq), cs[hh]] * scale).astype(BF16) for hh in HH]
        kw = [k_ref[pl.ds(k0, nk), cs[hh]].astype(BF16) for hh in HH]
        vw = [v_ref[pl.ds(k0, nk), cs[hh]].astype(BF16) for hh in HH]
        s_loc = [_dot_nt(q[hh], kw[hh]) + bias_ref[hh, var] for hh in HH]
        s_ctx = [_dot_nt(q[hh], kc[hh]) for hh in HH]
        m = [jnp.maximum(jnp.max(s_loc[hh], axis=-1, keepdims=True),
                         jnp.max(s_ctx[hh], axis=-1, keepdims=True)) for hh in HH]
        p_loc = [jnp.exp(s_loc[hh] - m[hh]) for hh in HH]
        p_ctx = [jnp.exp(s_ctx[hh] - m[hh]) for hh in HH]
        den = [jnp.sum(p_loc[hh], axis=-1, keepdims=True) + jnp.sum(p_ctx[hh], axis=-1, keepdims=True)
               for hh in HH]
        o = [_dot(p_loc[hh].astype(BF16), vw[hh]) + _dot(p_ctx[hh].astype(BF16), vc[hh]) for hh in HH]
        o_ref[pl.ds(q0, nq), :] = jnp.concatenate([o[hh] / den[hh] for hh in HH], axis=-1)
        return carry

    lax.fori_loop(0, nblk, body, 0)


def _na_attn(qkv, k_ctx, v_ctx, bias):
    B, L, _ = qkv.shape
    P = k_ctx.shape[2]
    rows = L // GRID_W
    hp = NA_HEADS // 2
    kern = functools.partial(_na_attn_kernel, rows=rows)
    return pl.pallas_call(
        kern,
        out_shape=jax.ShapeDtypeStruct((B, L, D_MODEL), F32),
        grid=(hp, B),
        in_specs=[pl.BlockSpec((None, L, 2 * NA_DH), lambda p, b: (b, 0, p)),
                  pl.BlockSpec((None, L, 2 * NA_DH), lambda p, b: (b, 0, hp + p)),
                  pl.BlockSpec((None, L, 2 * NA_DH), lambda p, b: (b, 0, 2 * hp + p)),
                  pl.BlockSpec((None, 2, P, NA_DH), lambda p, b: (b, p, 0, 0)),
                  pl.BlockSpec((None, 2, P, NA_DH), lambda p, b: (b, p, 0, 0)),
                  pl.BlockSpec((2,) + bias.shape[1:], lambda p, b: (p, 0, 0, 0))],
        out_specs=pl.BlockSpec((None, L, 2 * NA_DH), lambda p, b: (b, 0, p)),
        compiler_params=_cparams("parallel", "parallel"),
        name="na_attn",
    )(qkv, qkv, qkv, k_ctx, v_ctx, bias)


def _na_bias_table(rpb, rows):
    variants, var_of_block, kstart = _na_geometry(rows)
    nblk = rows // NA_QROWS
    expect = [0] + [1] * (nblk - 2) + [2]
    assert rows % NA_QROWS == 0 and rows >= NA_KROWS and var_of_block == expect, (rows, var_of_block)
    assert all(kstart[b] == int(np.clip(b * NA_QROWS - NA_WIN_R // 2, 0, rows - NA_KROWS))
               for b in range(nblk))
    qcol = np.arange(GRID_W)
    kcol = np.arange(GRID_W)
    cstart = np.clip(qcol - NA_WIN_C // 2, 0, GRID_W - NA_WIN_C)
    col_ok = (kcol[None, :] >= cstart[:, None]) & (kcol[None, :] < cstart[:, None] + NA_WIN_C)
    dc = np.clip(kcol[None, :] - qcol[:, None], -(NA_WIN_C - 1), NA_WIN_C - 1) + NA_WIN_C - 1
    plane = jnp.where(col_ok[None, None], rpb[:, :, dc], NEG_BIG)
    masked = jnp.full((rpb.shape[0], GRID_W, GRID_W), NEG_BIG, rpb.dtype)
    tabs = []
    for row_ok, dr in variants:
        q_rows = [jnp.concatenate([plane[:, int(dr[a, c])] if row_ok[a, c] else masked
                                   for c in range(NA_KROWS)], axis=-1) for a in range(NA_QROWS)]
        tabs.append(jnp.concatenate(q_rows, axis=1))
    return jnp.stack(tabs, axis=1)


def _moe_router_kernel(x_ref, g_ref, sc_ref, sh_ref, whi_ref, wlo_ref, br_ref,
                       h_out_ref, gid_ref):
    h = _norm_mod(x_ref[...], g_ref[...], sc_ref[...], sh_ref[...])
    h_hi, h_lo = _split_bf16(h)
    lg = (_dot_nt(whi_ref[...], h_hi) + _dot_nt(wlo_ref[...], h_hi) + _dot_nt(whi_ref[...], h_lo)
          + br_ref[...])
    tm = lg.shape[1]

    def first_max(vals):
        mx = functools.reduce(jnp.maximum, vals)
        taken = jnp.zeros((1, tm), jnp.bool_)
        hot = []
        for v in vals:
            hit = jnp.logical_and(v == mx, jnp.logical_not(taken))
            taken = jnp.logical_or(taken, hit)
            hot.append(hit)
        return mx, hot

    gl = [lg[g:g + 1, :] for g in range(MOE_G)]
    gmax, g_hit = first_max(gl)
    gsum = functools.reduce(lambda a, b: a + b, [jnp.exp(v - gmax) for v in gl])
    g_w = 1.0 / gsum
    g_hot = [jnp.where(hit, 1.0, 0.0) for hit in g_hit]
    el = []
    for e in range(MOE_E):
        acc = jnp.zeros((1, tm), F32)
        for g in range(MOE_G):
            r = MOE_G + g * MOE_E + e
            acc = acc + g_hot[g] * lg[r:r + 1, :]
        el.append(acc)
    emax = functools.reduce(jnp.maximum, el)
    ex = [jnp.exp(v - emax) for v in el]
    esum = functools.reduce(lambda a, b: a + b, ex)
    pe = [v / esum for v in ex]
    m1, hot1 = first_max(pe)
    m2, hot2 = first_max([jnp.where(hot1[e], -1.0, pe[e]) for e in range(MOE_E)])
    den = m1 + m2
    w_e = [(jnp.where(hot1[e], m1, 0.0) + jnp.where(hot2[e], m2, 0.0)) / den * g_w
           for e in range(MOE_E)]
    h_out_ref[:, :D_MODEL] = h
    w_e.append(jnp.zeros((LANES - MOE_E, tm), F32))
    h_out_ref[:, D_MODEL:] = jnp.concatenate(w_e, axis=0).T
    gid_ref[...] = functools.reduce(lambda a, b: a + b,
                                    [jnp.where(g_hit[g], g, 0) for g in range(MOE_G)])


def _moe_router(x, g, sc, sh, w_rt, b_r, tm):
    B, L, _ = x.shape
    w_hi, w_lo = _split_bf16(w_rt)
    return pl.pallas_call(
        _moe_router_kernel,
        out_shape=(jax.ShapeDtypeStruct((B, L, D_MODEL + LANES), F32),
                   jax.ShapeDtypeStruct((B, 1, L), jnp.int32)),
        grid=(B, L // tm),
        in_specs=[pl.BlockSpec((None, tm, D_MODEL), lambda b, i: (b, i, 0)),
                  _vec_spec(D_MODEL), _mod_spec(sc), _mod_spec(sh),
                  pl.BlockSpec((ROUTER_ROWS, D_MODEL), lambda b, i: (0, 0)),
                  pl.BlockSpec((ROUTER_ROWS, D_MODEL), lambda b, i: (0, 0)),
                  pl.BlockSpec((ROUTER_ROWS, 1), lambda b, i: (0, 0))],
        out_specs=(pl.BlockSpec((None, tm, D_MODEL + LANES), lambda b, i: (b, i, 0)),
                   pl.BlockSpec((None, 1, tm), lambda b, i: (b, 0, i))),
        compiler_params=_cparams("parallel", "parallel"),
        name="moe_router",
    )(x, g, sc, sh, w_hi, w_lo, b_r)


def _moe_plan(gid, tm):
    T = gid.shape[0]
    n_tiles = T // tm + MOE_G
    tp = n_tiles * tm
    order = jnp.argsort(gid, stable=True).astype(jnp.int32)
    counts = jnp.sum((gid[:, None] == jnp.arange(MOE_G)[None, :]).astype(jnp.int32), axis=0)
    padded = ((counts + tm - 1) // tm) * tm
    seg_end = jnp.cumsum(padded)
    seg_start = seg_end - padded
    first = jnp.cumsum(counts) - counts

    def pick(g_idx, table):
        return functools.reduce(lambda a, b: a + b,
                                [jnp.where(g_idx == g, table[g], 0) for g in range(MOE_G)])

    p = jnp.arange(tp, dtype=jnp.int32)
    g_of_p = functools.reduce(lambda a, b: a + b,
                              [(p >= seg_end[g]).astype(jnp.int32) for g in range(MOE_G - 1)])
    valid = p - pick(g_of_p, seg_start) < pick(g_of_p, counts)
    order_ext = jnp.concatenate([jnp.zeros((tp,), jnp.int32), order, jnp.zeros((tp,), jnp.int32)])
    tok = jnp.zeros((tp,), jnp.int32)
    for g in range(MOE_G):
        win = lax.dynamic_slice(order_ext, (tp + first[g] - seg_start[g],), (tp,))
        tok = jnp.where(g_of_p == g, win, tok)
    src = jnp.where(valid, tok, 0)
    dst = jnp.where(valid, tok, T + p % (2 * tm))
    t0 = jnp.arange(n_tiles, dtype=jnp.int32) * tm
    tile_gid = functools.reduce(lambda a, b: a + b,
                                [(t0 >= seg_end[g]).astype(jnp.int32) for g in range(MOE_G - 1)])
    tile_cnt = jnp.clip(pick(tile_gid, counts) - (t0 - pick(tile_gid, seg_start)), 0, tm)
    return (src.reshape(n_tiles, 1, tm), dst.reshape(n_tiles, 1, tm), tile_gid, tile_cnt)


def _moe_sparse_kernel(tgid_ref, tcnt_ref, src_ref, srcn_ref, dstp_ref, dst_ref, h_hbm, w1_ref, w3_ref,
                       w2_ref, y_hbm, hbuf, ybuf, gsem, ssem, *, tm):
    i = pl.program_id(0)
    n = pl.num_programs(0)
    slot = lax.rem(i, 2)
    other = 1 - slot

    def start_gather(idx_ref, s):
        for r in range(tm):
            pltpu.make_async_copy(h_hbm.at[pl.ds(idx_ref[0, r], 1)], hbuf.at[s, pl.ds(r, 1)],
                                  gsem.at[s]).start(priority=r % 2)

    def start_scatter(idx_ref, s):
        for r in range(tm):
            pltpu.make_async_copy(ybuf.at[s, pl.ds(r, 1)], y_hbm.at[pl.ds(idx_ref[0, r], 1)],
                                  ssem.at[s]).start(priority=r % 2)

    def wait_gather(s):
        pltpu.make_async_copy(h_hbm.at[pl.ds(0, tm)], hbuf.at[s], gsem.at[s]).wait()

    def wait_scatter(s):
        pltpu.make_async_copy(ybuf.at[s], y_hbm.at[pl.ds(0, tm)], ssem.at[s]).wait()

    @pl.when(i == 0)
    def _():
        ybuf[...] = jnp.zeros_like(ybuf)
        n_tok = y_hbm.shape[0] - 2 * tm
        for s in range(2):
            cp = pltpu.make_async_copy(ybuf.at[s], y_hbm.at[pl.ds(n_tok + s * tm, tm)], ssem.at[s])
            cp.start()
            cp.wait()
        start_gather(src_ref, 0)

    wait_gather(slot)

    @pl.when(i >= 1)
    def _():
        wait_scatter(slot)

    start_gather(srcn_ref, other)
    start_scatter(dstp_ref, other)
    h = hbuf[slot, :, :D_MODEL].astype(BF16)
    row = lax.broadcasted_iota(jnp.int32, (tm, LANES), 0)
    cw = jnp.where(row < tcnt_ref[i], hbuf[slot, :, D_MODEL:], 0.0)
    acc = jnp.zeros((tm, D_MODEL), F32)
    for e in range(MOE_E):
        a = _dot(h, w1_ref[e])
        b = _dot(h, w3_ref[e])
        hid = _silu(a) * b * cw[:, e:e + 1]
        acc = acc + _dot(hid.astype(BF16), w2_ref[e])
    ybuf[slot] = acc

    @pl.when(i == n - 1)
    def _():
        start_scatter(dst_ref, slot)
        wait_gather(other)
        wait_scatter(other)
        wait_scatter(slot)


def _moe_sparse(h_ext, plan, w1, w3, w2, tm):
    T = h_ext.shape[0]
    src, dst, tile_gid, tile_cnt = plan
    n_tiles = src.shape[0]
    kern = functools.partial(_moe_sparse_kernel, tm=tm)
    smem_row = lambda f: pl.BlockSpec((None, 1, tm), f, memory_space=pltpu.SMEM)
    grid_spec = pltpu.PrefetchScalarGridSpec(
        num_scalar_prefetch=2,
        grid=(n_tiles,),
        in_specs=[smem_row(lambda i, tg, tc: (i, 0, 0)),
                  smem_row(lambda i, tg, tc: (jnp.minimum(i + 1, n_tiles - 1), 0, 0)),
                  smem_row(lambda i, tg, tc: (jnp.maximum(i - 1, 0), 0, 0)),
                  smem_row(lambda i, tg, tc: (i, 0, 0)),
                  pl.BlockSpec(memory_space=pl.ANY),
                  pl.BlockSpec((None, MOE_E, D_MODEL, MOE_F), lambda i, tg, tc: (tg[i], 0, 0, 0)),
                  pl.BlockSpec((None, MOE_E, D_MODEL, MOE_F), lambda i, tg, tc: (tg[i], 0, 0, 0)),
                  pl.BlockSpec((None, MOE_E, MOE_F, D_MODEL), lambda i, tg, tc: (tg[i], 0, 0, 0))],
        out_specs=pl.BlockSpec(memory_space=pl.ANY),
        scratch_shapes=[pltpu.VMEM((2, tm, D_MODEL + LANES), F32), pltpu.VMEM((2, tm, D_MODEL), F32),
                        pltpu.SemaphoreType.DMA((2,)), pltpu.SemaphoreType.DMA((2,))])
    return pl.pallas_call(
        kern,
        out_shape=jax.ShapeDtypeStruct((T + 2 * tm, D_MODEL), F32),
        grid_spec=grid_spec,
        compiler_params=_cparams("arbitrary"),
        name="moe_sparse",
    )(tile_gid, tile_cnt, src, src, dst, dst, h_ext, w1, w3, w2)


def _moe_combine_kernel(x_ref, y_ref, gt_ref, o_ref):
    o_ref[...] = x_ref[...] + gt_ref[...] * y_ref[...]


def _moe_combine_norm_kernel(x_ref, y_ref, gt_ref, g_ref, o_ref):
    x = x_ref[...] + gt_ref[...] * y_ref[...]
    ms = jnp.mean(x * x, axis=-1, keepdims=True)
    o_ref[...] = (x * lax.rsqrt(ms + RMS_EPS)) * g_ref[...]


def _moe_combine(x, y, gt, tm, final_g=None):
    B, L, _ = x.shape
    nb = L // tm
    in_specs = [pl.BlockSpec((None, tm, D_MODEL), lambda b, i: (b, i, 0)),
                pl.BlockSpec((tm, D_MODEL), lambda b, i: (b * nb + i, 0)),
                _mod_spec(gt)]
    args = (x, y, gt)
    if final_g is not None:
        in_specs.append(_vec_spec(D_MODEL))
        args += (final_g,)
    return pl.pallas_call(
        _moe_combine_kernel if final_g is None else _moe_combine_norm_kernel,
        out_shape=jax.ShapeDtypeStruct((B, L, D_MODEL), F32),
        grid=(B, nb),
        in_specs=in_specs,
        out_specs=pl.BlockSpec((None, tm, D_MODEL), lambda b, i: (b, i, 0)),
        compiler_params=_cparams("parallel", "parallel"),
        name="moe_combine",
    )(*args)


def _row(v):
    return v.reshape(1, -1)


def _prepare_weights(P):
    W = {}
    W["cv_pw1_w"] = P["cv_pw1_w"].astype(BF16)
    W["cv_pw2_w"] = P["cv_pw2_w"].astype(BF16)
    W["gdn_w_qkvz"] = P["gdn_w_qkvz"].astype(BF16)
    W["gdn_w_ba"] = jnp.pad(P["gdn_w_ba"], ((0, 0), (0, 0), (0, LANES - 4 * GDN_H)))
    a_neg = jnp.exp(P["gdn_a_log"].astype(F32))
    zeros = jnp.zeros_like(a_neg)
    a_cols = jnp.stack([zeros, a_neg], axis=2).reshape(a_neg.shape[0], 4 * GDN_H)
    dt_cols = jnp.stack([zeros, P["gdn_dt_bias"].astype(F32)], axis=2).reshape(a_neg.shape[0], 4 * GDN_H)
    W["gdn_a_row"] = jnp.pad(a_cols, ((0, 0), (0, LANES - 4 * GDN_H)))
    W["gdn_dt_row"] = jnp.pad(dt_cols, ((0, 0), (0, LANES - 4 * GDN_H)))
    W["gdn_w_o"] = P["gdn_w_o"].astype(BF16)
    W["na_w_qkv"] = P["na_w_qkv"].astype(BF16)
    W["na_w_o"] = P["na_w_o"].astype(BF16)
    W["na_rpb"] = P["na_rpb"]
    W["moe_w1"] = P["moe_w1"].astype(BF16)
    W["moe_w3"] = P["moe_w3"].astype(BF16)
    W["moe_w2"] = P["moe_w2"].astype(BF16)
    w_r = jnp.concatenate([P["moe_wg"], P["moe_we"]], axis=-1)
    n_r = w_r.shape[-1]
    W["moe_w_rt"] = jnp.pad(w_r.transpose(0, 2, 1), ((0, 0), (0, ROUTER_ROWS - n_r), (0, 0)))
    b_r = jnp.concatenate([P["moe_bg"], P["moe_be"]], axis=-1)
    W["moe_b_r"] = jnp.pad(b_r, ((0, 0), (0, ROUTER_ROWS - n_r)))[:, :, None]
    return W


def _trunk(x, mods, P, W, cache):
    B, L, _ = x.shape
    latent = cache is not None
    Bm = mods.shape[1]
    flat = Bm == 1
    tm_seq = min(L, 256)
    n_rows = B * L if flat else L
    tm_tok = min(512, n_rows)
    tm_big = min(1024, n_rows)
    gdn_states, na_ks, na_vs = [], [], []

    def tok(a):
        return a.reshape(1, B * L, a.shape[-1]) if flat else a

    def seq(a):
        return a.reshape(B, L, a.shape[-1])

    for i in range(DEPTH):
        m = mods[i].reshape(Bm, 1, ADA_CHUNKS, D_MODEL)
        sh1, sc1, gt1, sh2, sc2, gt2 = (m[:, :, c] for c in range(ADA_CHUNKS))
        g1 = _row(P["norm1_g"][i])
        kind, j = i % N_MIXERS, i // N_MIXERS
        if kind == 0:
            u = _nm_glu(tok(x), g1, sc1, sh1, W["cv_pw1_w"][j], _row(P["cv_pw1_b"][j]), tm_big, TN_GLU)
            x = _conv_tail(seq(u), P["cv_dw_w"][j], _row(P["cv_dw_b"][j]), _row(P["cv_ln_g"][j]),
                           _row(P["cv_ln_b"][j]), W["cv_pw2_w"][j], _row(P["cv_pw2_b"][j]),
                           seq(x), gt1, tm_seq)
        elif kind == 1:
            qkvz = _nm_linear(tok(x), g1, sc1, sh1, W["gdn_w_qkvz"][j], tm_big, TN_WIDE)
            ba = _nm_small(tok(x), g1, sc1, sh1, W["gdn_w_ba"][j], tm_tok)
            qkv = _gdn_prep(seq(qkvz), P["gdn_conv_w"][j], min(L, tm_tok))
            if latent:
                s0 = cache[0][:, j].astype(F32)
            else:
                s0 = jnp.zeros((B, 2, GDN_H, GDN_DK, GDN_DK), F32)
            a_row, dt_row = W["gdn_a_row"][j:j + 1], W["gdn_dt_row"][j:j + 1]
            o_f, s_f = _gdn_chunk(qkv, seq(ba), a_row, dt_row, s0[:, 0], rev=False)
            o_b, s_b = _gdn_chunk(qkv, seq(ba), a_row, dt_row, s0[:, 1], rev=True)
            gdn_states.append(jnp.stack([s_f, s_b], axis=1))
            x = _gdn_out(tok(o_f), tok(o_b), qkvz, _row(P["gdn_o_g"][j]), W["gdn_w_o"][j],
                         tok(x), gt1, tm_tok)
        else:
            qkv = _nm_linear(tok(x), g1, sc1, sh1, W["na_w_qkv"][j], tm_big, TN_WIDE)
            if latent:
                bias = _na_bias_table(W["na_rpb"][j], L // GRID_W)
                o = _na_attn(seq(qkv), cache[1][:, j], cache[2][:, j], bias)
            else:
                o, k_ctx, v_ctx = _ctx_attn(seq(qkv))
                na_ks.append(k_ctx)
                na_vs.append(v_ctx)
            x = _linear_res(tok(o), W["na_w_o"][j], tok(x), gt1, tm_tok)
        h2, gid = _moe_router(tok(x), _row(P["norm2_g"][i]), sc2, sh2, W["moe_w_rt"][i],
                              W["moe_b_r"][i], tm_tok)
        plan = _moe_plan(gid.reshape(B * L), MOE_TILE)
        y = _moe_sparse(h2.reshape(B * L, D_MODEL + LANES), plan, W["moe_w1"][i], W["moe_w3"][i],
                        W["moe_w2"][i], MOE_TILE)
        final_g = _row(P["final_norm_g"]) if i == DEPTH - 1 else None
        x = seq(_moe_combine(tok(x), y, gt2, tm_tok, final_g))
    return x, gdn_states, na_ks, na_vs


def kernel(x_prompt, x_sample, state_gdn, cache_na_k, cache_na_v, c, c_ctx,
           ada_w, ada_b, norm1_g, norm2_g,
           cv_pw1_w, cv_pw1_b, cv_dw_w, cv_dw_b, cv_ln_g, cv_ln_b, cv_pw2_w, cv_pw2_b,
           gdn_w_qkvz, gdn_conv_w, gdn_w_ba, gdn_a_log, gdn_dt_bias, gdn_o_g, gdn_w_o,
           na_w_qkv, na_rpb, na_w_o,
           moe_wg, moe_bg, moe_we, moe_be, moe_w1, moe_w3, moe_w2,
           final_norm_g):
    P = dict(norm1_g=norm1_g, norm2_g=norm2_g,
             cv_pw1_w=cv_pw1_w, cv_pw1_b=cv_pw1_b, cv_dw_w=cv_dw_w, cv_dw_b=cv_dw_b,
             cv_ln_g=cv_ln_g, cv_ln_b=cv_ln_b, cv_pw2_w=cv_pw2_w, cv_pw2_b=cv_pw2_b,
             gdn_w_qkvz=gdn_w_qkvz, gdn_conv_w=gdn_conv_w, gdn_w_ba=gdn_w_ba, gdn_a_log=gdn_a_log,
             gdn_dt_bias=gdn_dt_bias, gdn_o_g=gdn_o_g, gdn_w_o=gdn_w_o,
             na_w_qkv=na_w_qkv, na_rpb=na_rpb, na_w_o=na_w_o,
             moe_wg=moe_wg, moe_bg=moe_bg, moe_we=moe_we, moe_be=moe_be,
             moe_w1=moe_w1, moe_w3=moe_w3, moe_w2=moe_w2, final_norm_g=final_norm_g)
    W = _prepare_weights(P)
    nb = c.shape[0]
    rows = 8 * ((nb + 1 + 7) // 8)
    cond = jnp.concatenate([c, c_ctx[None, :], jnp.zeros((rows - nb - 1, D_MODEL), F32)], axis=0)
    mods = _ada_all(cond, ada_w, ada_b)
    y_prompt, gdn_states, na_ks, na_vs = _trunk(x_prompt, mods[:, nb:nb + 1], P, W, None)
    y_sample, _, _, _ = _trunk(x_sample, mods[:, :nb], P, W, (state_gdn, cache_na_k, cache_na_v))
    return (y_prompt, y_sample, jnp.stack(gdn_states, axis=1),
            jnp.stack(na_ks, axis=1), jnp.stack(na_vs, axis=1))
```

```python
import functools

import numpy as np
import jax
import jax.numpy as jnp
from jax import lax
from jax.experimental import pallas as pl
from jax.experimental.pallas import tpu as pltpu

F32 = jnp.float32
BF16 = jnp.bfloat16

D_MODEL = 1024
DEPTH = 4
N_MIXERS = 3
ADA_CHUNKS = 6
RMS_EPS = 1e-6
LN_EPS = 1e-5
CONV_WIDTH = 31
CONV_HALO = 16
GDN_H = 8
GDN_DK = 128
GDN_CONV = 5
GDN_HALO = 8
GDN_BLOCK = 128
GDN_SUB = 16
NA_HEADS = 16
NA_DH = 64
NA_WIN_R = 8
NA_WIN_C = 16
NA_QROWS = 4
NA_KROWS = 12
GRID_W = 64
MOE_G = 4
MOE_E = 4
MOE_F = 256
MOE_TILE = 512
ROUTER_ROWS = 32
LANES = 128
SUBLANES = 8
CONV_TAPS_PAD = -(-CONV_WIDTH // SUBLANES) * SUBLANES
CONV_ROW_CHUNK = 64
TN_WIDE = 1024
TN_GLU = 512
NEG_BIG = -1e30

VMEM_LIMIT_BYTES = 48 * 1024 * 1024


def _cparams(*sem):
    return pltpu.CompilerParams(dimension_semantics=sem, vmem_limit_bytes=VMEM_LIMIT_BYTES)


def _sigmoid(x):
    return 1.0 / (1.0 + jnp.exp(-x))


def _silu(x):
    return x * _sigmoid(x)


def _norm_mod(x, g, sc, sh):
    ms = jnp.mean(x * x, axis=-1, keepdims=True)
    y = x * lax.rsqrt(ms + RMS_EPS)
    return (y * g) * (1.0 + sc) + sh


def _dot(a, b):
    return jnp.dot(a, b, preferred_element_type=F32)


def _dot_nt(a, b):
    return lax.dot_general(a, b, (((1,), (1,)), ((), ())), preferred_element_type=F32)


def _dot_f32(a, b):
    return jnp.dot(a, b, preferred_element_type=F32, precision=lax.Precision.HIGHEST)


def _split_bf16(w):
    hi = w.astype(BF16)
    lo = (w - hi.astype(F32)).astype(BF16)
    return hi, lo


def _mod_spec(mod):
    if mod.shape[0] == 1:
        return pl.BlockSpec((None, 1, D_MODEL), lambda b, *_: (0, 0, 0))
    return pl.BlockSpec((None, 1, D_MODEL), lambda b, *_: (b, 0, 0))


def _vec_spec(n):
    return pl.BlockSpec((1, n), lambda *_: (0, 0))


def _ada_kernel(c_ref, w_ref, b_ref, o_ref):
    s = _silu(c_ref[...]).astype(BF16)
    o_ref[...] = _dot(s, w_ref[...].astype(BF16)) + b_ref[...]


def _ada_all(cond, ada_w, ada_b, tn=TN_WIDE):
    R = cond.shape[0]
    N = ada_w.shape[-1]
    return pl.pallas_call(
        _ada_kernel,
        out_shape=jax.ShapeDtypeStruct((DEPTH, R, N), F32),
        grid=(DEPTH, N // tn),
        in_specs=[pl.BlockSpec((R, D_MODEL), lambda l, j: (0, 0)),
                  pl.BlockSpec((None, D_MODEL, tn), lambda l, j: (l, 0, j)),
                  pl.BlockSpec((None, 1, tn), lambda l, j: (l, 0, j))],
        out_specs=pl.BlockSpec((None, R, tn), lambda l, j: (l, 0, j)),
        compiler_params=_cparams("parallel", "parallel"),
        name="ada",
    )(cond, ada_w, ada_b.reshape(DEPTH, 1, N))


def _nm_linear_kernel(x_ref, g_ref, sc_ref, sh_ref, w_ref, o_ref, h_ref):
    @pl.when(pl.program_id(2) == 0)
    def _():
        h_ref[...] = _norm_mod(x_ref[...], g_ref[...], sc_ref[...], sh_ref[...]).astype(BF16)

    o_ref[...] = _dot(h_ref[...], w_ref[...]).astype(o_ref.dtype)


def _nm_linear(x, g, sc, sh, w, tm, tn):
    B, L, _ = x.shape
    N = w.shape[1]
    return pl.pallas_call(
        _nm_linear_kernel,
        out_shape=jax.ShapeDtypeStruct((B, L, N), F32),
        grid=(B, L // tm, N // tn),
        in_specs=[pl.BlockSpec((None, tm, D_MODEL), lambda b, i, j: (b, i, 0)),
                  _vec_spec(D_MODEL), _mod_spec(sc), _mod_spec(sh),
                  pl.BlockSpec((D_MODEL, tn), lambda b, i, j: (0, j))],
        out_specs=pl.BlockSpec((None, tm, tn), lambda b, i, j: (b, i, j)),
        scratch_shapes=[pltpu.VMEM((tm, D_MODEL), BF16)],
        compiler_params=_cparams("parallel", "parallel", "arbitrary"),
        name="nm_linear",
    )(x, g, sc, sh, w)


def _nm_glu_kernel(x_ref, g_ref, sc_ref, sh_ref, wa_ref, wg_ref, ba_ref, bg_ref, o_ref, h_ref):
    @pl.when(pl.program_id(2) == 0)
    def _():
        h_ref[...] = _norm_mod(x_ref[...], g_ref[...], sc_ref[...], sh_ref[...]).astype(BF16)

    h = h_ref[...]
    a = _dot(h, wa_ref[...]) + ba_ref[...]
    gate = _dot(h, wg_ref[...]) + bg_ref[...]
    o_ref[...] = a * _sigmoid(gate)


def _nm_glu(x, g, sc, sh, w, b, tm, tn):
    B, L, _ = x.shape
    N = w.shape[1] // 2
    nj = N // tn
    return pl.pallas_call(
        _nm_glu_kernel,
        out_shape=jax.ShapeDtypeStruct((B, L, N), F32),
        grid=(B, L // tm, nj),
        in_specs=[pl.BlockSpec((None, tm, D_MODEL), lambda b_, i, j: (b_, i, 0)),
                  _vec_spec(D_MODEL), _mod_spec(sc), _mod_spec(sh),
                  pl.BlockSpec((D_MODEL, tn), lambda b_, i, j: (0, j)),
                  pl.BlockSpec((D_MODEL, tn), lambda b_, i, j: (0, j + nj)),
                  pl.BlockSpec((1, tn), lambda b_, i, j: (0, j)),
                  pl.BlockSpec((1, tn), lambda b_, i, j: (0, j + nj))],
        out_specs=pl.BlockSpec((None, tm, tn), lambda b_, i, j: (b_, i, j)),
        scratch_shapes=[pltpu.VMEM((tm, D_MODEL), BF16)],
        compiler_params=_cparams("parallel", "parallel", "arbitrary"),
        name="nm_glu",
    )(x, g, sc, sh, w, w, b, b)


def _nm_small_kernel(x_ref, g_ref, sc_ref, sh_ref, whi_ref, wlo_ref, o_ref):
    h = _norm_mod(x_ref[...], g_ref[...], sc_ref[...], sh_ref[...])
    h_hi, h_lo = _split_bf16(h)
    o_ref[...] = _dot(h_hi, whi_ref[...]) + _dot(h_hi, wlo_ref[...]) + _dot(h_lo, whi_ref[...])


def _nm_small(x, g, sc, sh, w, tm):
    B, L, _ = x.shape
    w_hi, w_lo = _split_bf16(w)
    return pl.pallas_call(
        _nm_small_kernel,
        out_shape=jax.ShapeDtypeStruct((B, L, LANES), F32),
        grid=(B, L // tm),
        in_specs=[pl.BlockSpec((None, tm, D_MODEL), lambda b, i: (b, i, 0)),
                  _vec_spec(D_MODEL), _mod_spec(sc), _mod_spec(sh),
                  pl.BlockSpec((D_MODEL, LANES), lambda b, i: (0, 0)),
                  pl.BlockSpec((D_MODEL, LANES), lambda b, i: (0, 0))],
        out_specs=pl.BlockSpec((None, tm, LANES), lambda b, i: (b, i, 0)),
        compiler_params=_cparams("parallel", "parallel"),
        name="nm_small",
    )(x, g, sc, sh, w_hi, w_lo)


def _linear_res_kernel(a_ref, w_ref, x_ref, gt_ref, o_ref):
    y = _dot(a_ref[...].astype(BF16), w_ref[...])
    o_ref[...] = x_ref[...] + gt_ref[...] * y


def _linear_res(a, w, x, gt, tm):
    B, L, K = a.shape
    return pl.pallas_call(
        _linear_res_kernel,
        out_shape=jax.ShapeDtypeStruct((B, L, D_MODEL), F32),
        grid=(B, L // tm),
        in_specs=[pl.BlockSpec((None, tm, K), lambda b, i: (b, i, 0)),
                  pl.BlockSpec((K, D_MODEL), lambda b, i: (0, 0)),
                  pl.BlockSpec((None, tm, D_MODEL), lambda b, i: (b, i, 0)),
                  _mod_spec(gt)],
        out_specs=pl.BlockSpec((None, tm, D_MODEL), lambda b, i: (b, i, 0)),
        compiler_params=_cparams("parallel", "parallel"),
        name="linear_res",
    )(a, w, x, gt)


def _conv_tail_kernel(u_ref, up_ref, un_ref, dww_ref, dwb_ref, lng_ref, lnb_ref, w2_ref, b2_ref,
                      x_ref, gt_ref, o_ref, buf_ref, sh_ref, cv_ref, *, tm, rc):
    i = pl.program_id(1)
    last = pl.num_programs(1) - 1
    buf_ref[0:CONV_HALO, :] = jnp.where(i > 0, up_ref[...], 0.0)
    buf_ref[CONV_HALO:CONV_HALO + tm, :] = u_ref[...]
    buf_ref[CONV_HALO + tm:, :] = jnp.where(i < last, un_ref[...], 0.0)
    off = CONV_HALO - CONV_WIDTH // 2
    nsh = sh_ref.shape[1]
    for b in range(SUBLANES):
        sh_ref[b] = buf_ref[b:b + nsh, :]
    for c in range(D_MODEL // LANES):
        cs = slice(c * LANES, (c + 1) * LANES)
        for r0 in range(0, tm, rc):
            acc = jnp.zeros((rc, LANES), F32)
            for k in range(CONV_WIDTH):
                a, b = divmod(off + k, SUBLANES)
                acc = acc + dww_ref[k:k + 1, cs] * sh_ref[b, r0 + a * SUBLANES:r0 + a * SUBLANES + rc, cs]
            cv_ref[r0:r0 + rc, cs] = acc + dwb_ref[:, cs]
    v = cv_ref[...]
    mu = jnp.mean(v, axis=-1, keepdims=True)
    vc = v - mu
    var = jnp.mean(vc * vc, axis=-1, keepdims=True)
    y = _silu(vc * lax.rsqrt(var + LN_EPS) * lng_ref[...] + lnb_ref[...])
    z = _dot(y.astype(BF16), w2_ref[...]) + b2_ref[...]
    o_ref[...] = x_ref[...] + gt_ref[...] * z


def _conv_tail(u, dw_w, dw_b, ln_g, ln_b, w2, b2, x, gt, tm):
    B, L, _ = u.shape
    hb = tm // CONV_HALO
    nh = L // CONV_HALO
    dww = jnp.pad(dw_w, ((0, CONV_TAPS_PAD - CONV_WIDTH), (0, 0)))
    last_tap = CONV_HALO - CONV_WIDTH // 2 + CONV_WIDTH - 1
    kern = functools.partial(_conv_tail_kernel, tm=tm, rc=CONV_ROW_CHUNK)
    return pl.pallas_call(
        kern,
        out_shape=jax.ShapeDtypeStruct((B, L, D_MODEL), F32),
        grid=(B, L // tm),
        in_specs=[pl.BlockSpec((None, tm, D_MODEL), lambda b, i: (b, i, 0)),
                  pl.BlockSpec((None, CONV_HALO, D_MODEL),
                               lambda b, i: (b, jnp.maximum(i * hb - 1, 0), 0)),
                  pl.BlockSpec((None, CONV_HALO, D_MODEL),
                               lambda b, i: (b, jnp.minimum((i + 1) * hb, nh - 1), 0)),
                  pl.BlockSpec((CONV_TAPS_PAD, D_MODEL), lambda b, i: (0, 0)),
                  _vec_spec(D_MODEL), _vec_spec(D_MODEL), _vec_spec(D_MODEL),
                  pl.BlockSpec((D_MODEL, D_MODEL), lambda b, i: (0, 0)),
                  _vec_spec(D_MODEL),
                  pl.BlockSpec((None, tm, D_MODEL), lambda b, i: (b, i, 0)),
                  _mod_spec(gt)],
        out_specs=pl.BlockSpec((None, tm, D_MODEL), lambda b, i: (b, i, 0)),
        scratch_shapes=[pltpu.VMEM((tm + 2 * CONV_HALO, D_MODEL), F32),
                        pltpu.VMEM((SUBLANES, tm + SUBLANES * (last_tap // SUBLANES), D_MODEL), F32),
                        pltpu.VMEM((tm, D_MODEL), F32)],
        compiler_params=_cparams("parallel", "parallel"),
        name="conv_tail",
    )(u, u, u, dww, dw_b, ln_g, ln_b, w2, b2, x, gt)


def _gdn_prep_kernel(u_ref, up_ref, un_ref, w_ref, o_ref, buf_ref, *, tm):
    i = pl.program_id(1)
    j = pl.program_id(2)
    last = pl.num_programs(1) - 1
    buf_ref[0:GDN_HALO, :] = jnp.where(i > 0, up_ref[...], 0.0)
    buf_ref[GDN_HALO:GDN_HALO + tm, :] = u_ref[...]
    buf_ref[GDN_HALO + tm:, :] = jnp.where(i < last, un_ref[...], 0.0)
    off = GDN_HALO - GDN_CONV // 2
    for h in range(GDN_H):
        cs = slice(h * GDN_DK, (h + 1) * GDN_DK)
        acc = jnp.zeros((tm, GDN_DK), F32)
        for k in range(GDN_CONV):
            acc = acc + w_ref[k:k + 1, cs] * buf_ref[off + k:off + k + tm, cs]
        y = _silu(acc)
        inv = lax.rsqrt(jnp.sum(y * y, axis=-1, keepdims=True) + 1e-6)
        f = jnp.where(j < 2, inv, 1.0) * jnp.where(j == 0, GDN_DK ** -0.5, 1.0)
        o_ref[:, cs] = y * f


def _gdn_prep(qkvz, conv_w, tm):
    B, L, _ = qkvz.shape
    W = GDN_H * GDN_DK
    hb = tm // GDN_HALO
    nh = L // GDN_HALO
    cw = jnp.pad(conv_w, ((0, SUBLANES - GDN_CONV), (0, 0)))
    kern = functools.partial(_gdn_prep_kernel, tm=tm)
    return pl.pallas_call(
        kern,
        out_shape=jax.ShapeDtypeStruct((B, L, 3 * W), F32),
        grid=(B, L // tm, 3),
        in_specs=[pl.BlockSpec((None, tm, W), lambda b, i, j: (b, i, j)),
                  pl.BlockSpec((None, GDN_HALO, W), lambda b, i, j: (b, jnp.maximum(i * hb - 1, 0), j)),
                  pl.BlockSpec((None, GDN_HALO, W),
                               lambda b, i, j: (b, jnp.minimum((i + 1) * hb, nh - 1), j)),
                  pl.BlockSpec((SUBLANES, W), lambda b, i, j: (0, j))],
        out_specs=pl.BlockSpec((None, tm, W), lambda b, i, j: (b, i, j)),
        scratch_shapes=[pltpu.VMEM((tm + 2 * GDN_HALO, W), F32)],
        compiler_params=_cparams("parallel", "parallel", "arbitrary"),
        name="gdn_prep",
    )(qkvz, qkvz, qkvz, cw)


def _split3(x):
    hi = x.astype(BF16)
    r1 = x - hi.astype(F32)
    mid = r1.astype(BF16)
    lo = (r1 - mid.astype(F32)).astype(BF16)
    return hi, mid, lo


def _mm3(a, b):
    lhs = jnp.concatenate([a[0], a[1]], axis=1)
    rhs = jnp.concatenate([b[0], b[0]], axis=0)
    return _dot(lhs, rhs) + _dot(a[0], b[1])


def _gdn_chunk_kernel(q_ref, k_ref, v_ref, ba_ref, a_ref, dtb_ref, s0_ref, o_ref, sfin_ref, s_ref,
                      *, rev, col0):
    n = pl.program_id(1)
    C = GDN_BLOCK

    @pl.when(n == 0)
    def _():
        s_ref[...] = s0_ref[...]

    ri = lax.broadcasted_iota(jnp.int32, (C, C), 0)
    ci = lax.broadcasted_iota(jnp.int32, (C, C), 1)
    if rev:
        incl, strict = ri <= ci, ri < ci
    else:
        incl, strict = ri >= ci, ri > ci
    blk = (ri // GDN_SUB) == (ci // GDN_SUB)
    eye_f = jnp.where(ri == ci, 1.0, 0.0)

    ba = ba_ref[...]
    beta_all = _sigmoid(ba)
    xa = ba + dtb_ref[...]
    softplus = jnp.maximum(xa, 0.0) + jnp.log(1.0 + jnp.exp(-jnp.abs(xa)))
    g_all = -a_ref[...] * softplus
    tri = jnp.where(incl, 1.0, 0.0).astype(BF16)
    g_hi, g_mid, g_lo = _split3(g_all)
    cum_all = _dot(tri, g_hi) + _dot(tri, g_mid) + _dot(tri, g_lo)
    cum_t = cum_all.T
    tot_row = cum_all[0:1, :] if rev else cum_all[C - 1:C, :]
    ecum_all = jnp.exp(cum_all)
    etail_all = jnp.exp(tot_row - cum_all)
    etot_row = jnp.exp(tot_row)

    HS = range(GDN_H)
    cs = [slice(h * GDN_DK, (h + 1) * GDN_DK) for h in HS]
    cg = [col0 + GDN_H + h for h in HS]
    k = [k_ref[:, cs[h]] for h in HS]
    kbf = [k[h].astype(BF16) for h in HS]
    beta = [beta_all[:, col0 + h:col0 + h + 1] for h in HS]
    kb = [k[h] * beta[h] for h in HS]
    decay = [jnp.where(incl, jnp.exp(jnp.where(
        incl, cum_all[:, cg[h]:cg[h] + 1] - cum_t[cg[h]:cg[h] + 1, :], 0.0)), 0.0) for h in HS]
    nm = [jnp.where(strict, _dot_nt(kb[h].astype(BF16), kbf[h]) * decay[h], 0.0) for h in HS]
    nd = [jnp.where(blk, nm[h], 0.0) for h in HS]
    nd_s = [_split_bf16(nd[h]) for h in HS]
    nd2_s = [_split_bf16(_mm3(nd_s[h], nd_s[h])) for h in HS]
    nd4_s = [_split_bf16(_mm3(nd2_s[h], nd2_s[h])) for h in HS]
    nd8_s = [_split_bf16(_mm3(nd4_s[h], nd4_s[h])) for h in HS]
    t = [eye_f - nd[h] for h in HS]
    t = [t[h] + _mm3(_split_bf16(t[h]), nd2_s[h]) for h in HS]
    t = [t[h] + _mm3(_split_bf16(t[h]), nd4_s[h]) for h in HS]
    t = [t[h] + _mm3(_split_bf16(t[h]), nd8_s[h]) for h in HS]
    t_s = [_split_bf16(t[h]) for h in HS]
    m_s = [_split_bf16(_mm3(t_s[h], _split_bf16(nm[h] - nd[h]))) for h in HS]
    m2_s = [_split_bf16(_mm3(m_s[h], m_s[h])) for h in HS]
    m4_s = [_split_bf16(_mm3(m2_s[h], m2_s[h])) for h in HS]
    a_inv = [t[h] + _mm3(m4_s[h], t_s[h]) for h in HS]
    a_inv = [a_inv[h] + _mm3(m2_s[h], _split_bf16(a_inv[h])) for h in HS]
    a_inv = [a_inv[h] - _mm3(m_s[h], _split_bf16(a_inv[h])) for h in HS]
    e_cum = [ecum_all[:, cg[h]:cg[h] + 1] for h in HS]
    rhs = [jnp.concatenate([v_ref[:, cs[h]] * beta[h], kb[h] * e_cum[h]], axis=-1) for h in HS]
    sol = [_mm3(_split_bf16(a_inv[h]), _split_bf16(rhs[h])) for h in HS]
    q = [q_ref[:, cs[h]] for h in HS]
    qk = [jnp.where(incl, _dot_nt(q[h].astype(BF16), kbf[h]) * decay[h], 0.0).astype(BF16)
          for h in HS]
    q_g = [(q[h] * e_cum[h]).astype(BF16) for h in HS]
    k_tail_t = [(k[h] * etail_all[:, cg[h]:cg[h] + 1]).T.astype(BF16) for h in HS]
    s = [s_ref[h] for h in HS]
    sb = [s[h].astype(BF16) for h in HS]
    vb = [(sol[h][:, :GDN_DK] - _dot(sol[h][:, GDN_DK:].astype(BF16), sb[h])).astype(BF16)
          for h in HS]
    for h in HS:
        o_ref[:, cs[h]] = _dot(q_g[h], sb[h]) + _dot(qk[h], vb[h])
    for h in HS:
        s_ref[h] = s[h] * etot_row[:, cg[h]:cg[h] + 1] + _dot(k_tail_t[h], vb[h])

    @pl.when(n == pl.num_programs(1) - 1)
    def _():
        sfin_ref[...] = s_ref[...]


def _gdn_chunk(qkv, ba, a_row, dtb_row, s0, rev):
    B, L, _ = qkv.shape
    W = GDN_H * GDN_DK
    nc = L // GDN_BLOCK
    cidx = (lambda n: nc - 1 - n) if rev else (lambda n: n)
    kern = functools.partial(_gdn_chunk_kernel, rev=rev, col0=2 * GDN_H if rev else 0)
    return pl.pallas_call(
        kern,
        out_shape=(jax.ShapeDtypeStruct((B, L, W), F32),
                   jax.ShapeDtypeStruct((B, GDN_H, GDN_DK, GDN_DK), F32)),
        grid=(B, nc),
        in_specs=[pl.BlockSpec((None, GDN_BLOCK, W), lambda b, n: (b, cidx(n), 0)),
                  pl.BlockSpec((None, GDN_BLOCK, W), lambda b, n: (b, cidx(n), 1)),
                  pl.BlockSpec((None, GDN_BLOCK, W), lambda b, n: (b, cidx(n), 2)),
                  pl.BlockSpec((None, GDN_BLOCK, LANES), lambda b, n: (b, cidx(n), 0)),
                  _vec_spec(LANES), _vec_spec(LANES),
                  pl.BlockSpec((None, GDN_H, GDN_DK, GDN_DK), lambda b, n: (b, 0, 0, 0))],
        out_specs=(pl.BlockSpec((None, GDN_BLOCK, W), lambda b, n: (b, cidx(n), 0)),
                   pl.BlockSpec((None, GDN_H, GDN_DK, GDN_DK), lambda b, n: (b, 0, 0, 0))),
        scratch_shapes=[pltpu.VMEM((GDN_H, GDN_DK, GDN_DK), F32)],
        compiler_params=_cparams("parallel", "arbitrary"),
        name="gdn_chunk_rev" if rev else "gdn_chunk_fwd",
    )(qkv, qkv, qkv, ba, a_row, dtb_row, s0)


def _gdn_out_kernel(of_ref, ob_ref, z_ref, og_ref, w_ref, x_ref, gt_ref, o_ref, y_ref):
    o = of_ref[...] + ob_ref[...]
    z = z_ref[...]
    for h in range(GDN_H):
        cs = slice(h * GDN_DK, (h + 1) * GDN_DK)
        oh = o[:, cs]
        ms = jnp.mean(oh * oh, axis=-1, keepdims=True)
        y = (oh * lax.rsqrt(ms + RMS_EPS)) * og_ref[...]
        y_ref[:, cs] = (y * _silu(z[:, cs])).astype(BF16)
    o_ref[...] = x_ref[...] + gt_ref[...] * _dot(y_ref[...], w_ref[...])


def _gdn_out(o_f, o_b, qkvz, o_g, w_o, x, gt, tm):
    B, L, W = o_f.shape
    return pl.pallas_call(
        _gdn_out_kernel,
        out_shape=jax.ShapeDtypeStruct((B, L, D_MODEL), F32),
        grid=(B, L // tm),
        in_specs=[pl.BlockSpec((None, tm, W), lambda b, i: (b, i, 0)),
                  pl.BlockSpec((None, tm, W), lambda b, i: (b, i, 0)),
                  pl.BlockSpec((None, tm, W), lambda b, i: (b, i, 3)),
                  _vec_spec(GDN_DK),
                  pl.BlockSpec((W, D_MODEL), lambda b, i: (0, 0)),
                  pl.BlockSpec((None, tm, D_MODEL), lambda b, i: (b, i, 0)),
                  _mod_spec(gt)],
        out_specs=pl.BlockSpec((None, tm, D_MODEL), lambda b, i: (b, i, 0)),
        scratch_shapes=[pltpu.VMEM((tm, W), BF16)],
        compiler_params=_cparams("parallel", "parallel"),
        name="gdn_out",
    )(o_f, o_b, qkvz, o_g, w_o, x, gt)


def _ctx_attn_kernel(q_ref, k_ref, v_ref, o_ref, ko_ref, vo_ref):
    scale = NA_DH ** -0.5
    HS = range(NA_HEADS)
    cs = [slice(h * NA_DH, (h + 1) * NA_DH) for h in HS]
    k = [k_ref[:, cs[h]] for h in HS]
    v = [v_ref[:, cs[h]] for h in HS]
    s = [_dot_nt((q_ref[:, cs[h]] * scale).astype(BF16), k[h].astype(BF16)) for h in HS]
    p = [jnp.exp(s[h] - jnp.max(s[h], axis=-1, keepdims=True)) for h in HS]
    p = [p[h] / jnp.sum(p[h], axis=-1, keepdims=True) for h in HS]
    o = [_dot(p[h].astype(BF16), v[h].astype(BF16)) for h in HS]
    for h in range(0, NA_HEADS, 2):
        o_ref[:, h * NA_DH:(h + 2) * NA_DH] = jnp.concatenate([o[h], o[h + 1]], axis=-1)
    for h in HS:
        ko_ref[h] = k[h]
        vo_ref[h] = v[h]


def _ctx_attn(qkv):
    B, L, _ = qkv.shape
    kv_shape = jax.ShapeDtypeStruct((B, NA_HEADS, L, NA_DH), F32)
    return pl.pallas_call(
        _ctx_attn_kernel,
        out_shape=(jax.ShapeDtypeStruct((B, L, D_MODEL), F32), kv_shape, kv_shape),
        grid=(B,),
        in_specs=[pl.BlockSpec((None, L, D_MODEL), lambda b: (b, 0, 0)),
                  pl.BlockSpec((None, L, D_MODEL), lambda b: (b, 0, 1)),
                  pl.BlockSpec((None, L, D_MODEL), lambda b: (b, 0, 2))],
        out_specs=(pl.BlockSpec((None, L, D_MODEL), lambda b: (b, 0, 0)),
                   pl.BlockSpec((None, NA_HEADS, L, NA_DH), lambda b: (b, 0, 0, 0)),
                   pl.BlockSpec((None, NA_HEADS, L, NA_DH), lambda b: (b, 0, 0, 0))),
        compiler_params=_cparams("parallel"),
        name="ctx_attn",
    )(qkv, qkv, qkv)


def _na_geometry(rows):
    nblk = rows // NA_QROWS
    kr = min(NA_WIN_R, rows)
    variants, var_of_block, kstart = [], [], []
    for b in range(nblk):
        r0 = b * NA_QROWS
        ks = int(np.clip(r0 - NA_WIN_R // 2, 0, rows - NA_KROWS))
        qr = r0 + np.arange(NA_QROWS)
        rs = np.clip(qr - kr // 2, 0, rows - kr)
        key_row = ks + np.arange(NA_KROWS)
        ok = (key_row[None, :] >= rs[:, None]) & (key_row[None, :] < rs[:, None] + kr)
        dr = np.where(ok, key_row[None, :] - qr[:, None] + NA_WIN_R - 1, 0)
        geo = (ok.tobytes(), dr.tobytes())
        if geo not in [g for g, _, _ in variants]:
            variants.append((geo, ok, dr))
        var_of_block.append([g for g, _, _ in variants].index(geo))
        kstart.append(ks)
    return [(ok, dr) for _, ok, dr in variants], var_of_block, kstart


def _na_attn_kernel(q_ref, k_ref, v_ref, kc_ref, vc_ref, bias_ref, o_ref, *, rows):
    scale = NA_DH ** -0.5
    nblk = rows // NA_QROWS
    nq = NA_QROWS * GRID_W
    nk = NA_KROWS * GRID_W
    kc = [kc_ref[hh].astype(BF16) for hh in range(2)]
    vc = [vc_ref[hh].astype(BF16) for hh in range(2)]

    def body(blk, carry):
        ks = jnp.clip(blk * NA_QROWS - NA_WIN_R // 2, 0, rows - NA_KROWS)
        var = jnp.where(blk == 0, 0, jnp.where(blk == nblk - 1, 2, 1))
        q0 = pl.multiple_of(blk * nq, nq)
        k0 = pl.multiple_of(ks * GRID_W, GRID_W)
        HH = range(2)
        cs = [slice(hh * NA_DH, (hh + 1) * NA_DH) for hh in HH]
        q = [(q_ref[pl.ds(q0, nq), cs[hh]] * scale).astype(BF16) for hh in HH]
        kw = [k_ref[pl.ds(k0, nk), cs[hh]].astype(BF16) for hh in HH]
        vw = [v_ref[pl.ds(k0, nk), cs[hh]].astype(BF16) for hh in HH]
        s_loc = [_dot_nt(q[hh], kw[hh]) + bias_ref[hh, var] for hh in HH]
        s_ctx = [_dot_nt(q[hh], kc[hh]) for hh in HH]
        m = [jnp.maximum(jnp.max(s_loc[hh], axis=-1, keepdims=True),
                         jnp.max(s_ctx[hh], axis=-1, keepdims=True)) for hh in HH]
        p_loc = [jnp.exp(s_loc[hh] - m[hh]) for hh in HH]
        p_ctx = [jnp.exp(s_ctx[hh] - m[hh]) for hh in HH]
        den = [jnp.sum(p_loc[hh], axis=-1, keepdims=True) + jnp.sum(p_ctx[hh], axis=-1, keepdims=True)
               for hh in HH]
        o = [_dot(p_loc[hh].astype(BF16), vw[hh]) + _dot(p_ctx[hh].astype(BF16), vc[hh]) for hh in HH]
        o_ref[pl.ds(q0, nq), :] = jnp.concatenate([o[hh] / den[hh] for hh in HH], axis=-1)
        return carry

    lax.fori_loop(0, nblk, body, 0)


def _na_attn(qkv, k_ctx, v_ctx, bias):
    B, L, _ = qkv.shape
    P = k_ctx.shape[2]
    rows = L // GRID_W
    hp = NA_HEADS // 2
    kern = functools.partial(_na_attn_kernel, rows=rows)
    return pl.pallas_call(
        kern,
        out_shape=jax.ShapeDtypeStruct((B, L, D_MODEL), F32),
        grid=(hp, B),
        in_specs=[pl.BlockSpec((None, L, 2 * NA_DH), lambda p, b: (b, 0, p)),
                  pl.BlockSpec((None, L, 2 * NA_DH), lambda p, b: (b, 0, hp + p)),
                  pl.BlockSpec((None, L, 2 * NA_DH), lambda p, b: (b, 0, 2 * hp + p)),
                  pl.BlockSpec((None, 2, P, NA_DH), lambda p, b: (b, p, 0, 0)),
                  pl.BlockSpec((None, 2, P, NA_DH), lambda p, b: (b, p, 0, 0)),
                  pl.BlockSpec((2,) + bias.shape[1:], lambda p, b: (p, 0, 0, 0))],
        out_specs=pl.BlockSpec((None, L, 2 * NA_DH), lambda p, b: (b, 0, p)),
        compiler_params=_cparams("parallel", "parallel"),
        name="na_attn",
    )(qkv, qkv, qkv, k_ctx, v_ctx, bias)


def _na_bias_table(rpb, rows):
    variants, var_of_block, kstart = _na_geometry(rows)
    nblk = rows // NA_QROWS
    expect = [0] + [1] * (nblk - 2) + [2]
    assert rows % NA_QROWS == 0 and rows >= NA_KROWS and var_of_block == expect, (rows, var_of_block)
    assert all(kstart[b] == int(np.clip(b * NA_QROWS - NA_WIN_R // 2, 0, rows - NA_KROWS))
               for b in range(nblk))
    qcol = np.arange(GRID_W)
    kcol = np.arange(GRID_W)
    cstart = np.clip(qcol - NA_WIN_C // 2, 0, GRID_W - NA_WIN_C)
    col_ok = (kcol[None, :] >= cstart[:, None]) & (kcol[None, :] < cstart[:, None] + NA_WIN_C)
    dc = np.clip(kcol[None, :] - qcol[:, None], -(NA_WIN_C - 1), NA_WIN_C - 1) + NA_WIN_C - 1
    plane = jnp.where(col_ok[None, None], rpb[:, :, dc], NEG_BIG)
    masked = jnp.full((rpb.shape[0], GRID_W, GRID_W), NEG_BIG, rpb.dtype)
    tabs = []
    for row_ok, dr in variants:
        q_rows = [jnp.concatenate([plane[:, int(dr[a, c])] if row_ok[a, c] else masked
                                   for c in range(NA_KROWS)], axis=-1) for a in range(NA_QROWS)]
        tabs.append(jnp.concatenate(q_rows, axis=1))
    return jnp.stack(tabs, axis=1)


def _moe_router_kernel(x_ref, g_ref, sc_ref, sh_ref, whi_ref, wlo_ref, br_ref,
                       h_out_ref, gid_ref):
    h = _norm_mod(x_ref[...], g_ref[...], sc_ref[...], sh_ref[...])
    h_hi, h_lo = _split_bf16(h)
    lg = (_dot_nt(whi_ref[...], h_hi) + _dot_nt(wlo_ref[...], h_hi) + _dot_nt(whi_ref[...], h_lo)
          + br_ref[...])
    tm = lg.shape[1]

    def first_max(vals):
        mx = functools.reduce(jnp.maximum, vals)
        taken = jnp.zeros((1, tm), jnp.bool_)
        hot = []
        for v in vals:
            hit = jnp.logical_and(v == mx, jnp.logical_not(taken))
            taken = jnp.logical_or(taken, hit)
            hot.append(hit)
        return mx, hot

    gl = [lg[g:g + 1, :] for g in range(MOE_G)]
    gmax, g_hit = first_max(gl)
    gsum = functools.reduce(lambda a, b: a + b, [jnp.exp(v - gmax) for v in gl])
    g_w = 1.0 / gsum
    g_hot = [jnp.where(hit, 1.0, 0.0) for hit in g_hit]
    el = []
    for e in range(MOE_E):
        acc = jnp.zeros((1, tm), F32)
        for g in range(MOE_G):
            r = MOE_G + g * MOE_E + e
            acc = acc + g_hot[g] * lg[r:r + 1, :]
        el.append(acc)
    emax = functools.reduce(jnp.maximum, el)
    ex = [jnp.exp(v - emax) for v in el]
    esum = functools.reduce(lambda a, b: a + b, ex)
    pe = [v / esum for v in ex]
    m1, hot1 = first_max(pe)
    m2, hot2 = first_max([jnp.where(hot1[e], -1.0, pe[e]) for e in range(MOE_E)])
    den = m1 + m2
    w_e = [(jnp.where(hot1[e], m1, 0.0) + jnp.where(hot2[e], m2, 0.0)) / den * g_w
           for e in range(MOE_E)]
    h_out_ref[:, :D_MODEL] = h
    w_e.append(jnp.zeros((LANES - MOE_E, tm), F32))
    h_out_ref[:, D_MODEL:] = jnp.concatenate(w_e, axis=0).T
    gid_ref[...] = functools.reduce(lambda a, b: a + b,
                                    [jnp.where(g_hit[g], g, 0) for g in range(MOE_G)])


def _moe_router(x, g, sc, sh, w_rt, b_r, tm):
    B, L, _ = x.shape
    w_hi, w_lo = _split_bf16(w_rt)
    return pl.pallas_call(
        _moe_router_kernel,
        out_shape=(jax.ShapeDtypeStruct((B, L, D_MODEL + LANES), F32),
                   jax.ShapeDtypeStruct((B, 1, L), jnp.int32)),
        grid=(B, L // tm),
        in_specs=[pl.BlockSpec((None, tm, D_MODEL), lambda b, i: (b, i, 0)),
                  _vec_spec(D_MODEL), _mod_spec(sc), _mod_spec(sh),
                  pl.BlockSpec((ROUTER_ROWS, D_MODEL), lambda b, i: (0, 0)),
                  pl.BlockSpec((ROUTER_ROWS, D_MODEL), lambda b, i: (0, 0)),
                  pl.BlockSpec((ROUTER_ROWS, 1), lambda b, i: (0, 0))],
        out_specs=(pl.BlockSpec((None, tm, D_MODEL + LANES), lambda b, i: (b, i, 0)),
                   pl.BlockSpec((None, 1, tm), lambda b, i: (b, 0, i))),
        compiler_params=_cparams("parallel", "parallel"),
        name="moe_router",
    )(x, g, sc, sh, w_hi, w_lo, b_r)


def _moe_plan(gid, tm):
    T = gid.shape[0]
    n_tiles = T // tm + MOE_G
    tp = n_tiles * tm
    order = jnp.argsort(gid, stable=True).astype(jnp.int32)
    counts = jnp.sum((gid[:, None] == jnp.arange(MOE_G)[None, :]).astype(jnp.int32), axis=0)
    padded = ((counts + tm - 1) // tm) * tm
    seg_end = jnp.cumsum(padded)
    seg_start = seg_end - padded
    first = jnp.cumsum(counts) - counts

    def pick(g_idx, table):
        return functools.reduce(lambda a, b: a + b,
                                [jnp.where(g_idx == g, table[g], 0) for g in range(MOE_G)])

    p = jnp.arange(tp, dtype=jnp.int32)
    g_of_p = functools.reduce(lambda a, b: a + b,
                              [(p >= seg_end[g]).astype(jnp.int32) for g in range(MOE_G - 1)])
    valid = p - pick(g_of_p, seg_start) < pick(g_of_p, counts)
    order_ext = jnp.concatenate([jnp.zeros((tp,), jnp.int32), order, jnp.zeros((tp,), jnp.int32)])
    tok = jnp.zeros((tp,), jnp.int32)
    for g in range(MOE_G):
        win = lax.dynamic_slice(order_ext, (tp + first[g] - seg_start[g],), (tp,))
        tok = jnp.where(g_of_p == g, win, tok)
    src = jnp.where(valid, tok, 0)
    dst = jnp.where(valid, tok, T + p % (2 * tm))
    t0 = jnp.arange(n_tiles, dtype=jnp.int32) * tm
    tile_gid = functools.reduce(lambda a, b: a + b,
                                [(t0 >= seg_end[g]).astype(jnp.int32) for g in range(MOE_G - 1)])
    tile_cnt = jnp.clip(pick(tile_gid, counts) - (t0 - pick(tile_gid, seg_start)), 0, tm)
    return (src.reshape(n_tiles, 1, tm), dst.reshape(n_tiles, 1, tm), tile_gid, tile_cnt)


def _moe_sparse_kernel(tgid_ref, tcnt_ref, src_ref, srcn_ref, dstp_ref, dst_ref, h_hbm, w1_ref, w3_ref,
                       w2_ref, y_hbm, hbuf, ybuf, gsem, ssem, *, tm):
    i = pl.program_id(0)
    n = pl.num_programs(0)
    slot = lax.rem(i, 2)
    other = 1 - slot

    def start_gather(idx_ref, s):
        for r in range(tm):
            pltpu.make_async_copy(h_hbm.at[pl.ds(idx_ref[0, r], 1)], hbuf.at[s, pl.ds(r, 1)],
                                  gsem.at[s]).start(priority=r % 2)

    def start_scatter(idx_ref, s):
        for r in range(tm):
            pltpu.make_async_copy(ybuf.at[s, pl.ds(r, 1)], y_hbm.at[pl.ds(idx_ref[0, r], 1)],
                                  ssem.at[s]).start(priority=r % 2)

    def wait_gather(s):
        pltpu.make_async_copy(h_hbm.at[pl.ds(0, tm)], hbuf.at[s], gsem.at[s]).wait()

    def wait_scatter(s):
        pltpu.make_async_copy(ybuf.at[s], y_hbm.at[pl.ds(0, tm)], ssem.at[s]).wait()

    @pl.when(i == 0)
    def _():
        ybuf[...] = jnp.zeros_like(ybuf)
        n_tok = y_hbm.shape[0] - 2 * tm
        for s in range(2):
            cp = pltpu.make_async_copy(ybuf.at[s], y_hbm.at[pl.ds(n_tok + s * tm, tm)], ssem.at[s])
            cp.start()
            cp.wait()
        start_gather(src_ref, 0)

    wait_gather(slot)

    @pl.when(i >= 1)
    def _():
        wait_scatter(slot)

    start_gather(srcn_ref, other)
    start_scatter(dstp_ref, other)
    h = hbuf[slot, :, :D_MODEL].astype(BF16)
    row = lax.broadcasted_iota(jnp.int32, (tm, LANES), 0)
    cw = jnp.where(row < tcnt_ref[i], hbuf[slot, :, D_MODEL:], 0.0)
    acc = jnp.zeros((tm, D_MODEL), F32)
    for e in range(MOE_E):
        a = _dot(h, w1_ref[e])
        b = _dot(h, w3_ref[e])
        hid = _silu(a) * b * cw[:, e:e + 1]
        acc = acc + _dot(hid.astype(BF16), w2_ref[e])
    ybuf[slot] = acc

    @pl.when(i == n - 1)
    def _():
        start_scatter(dst_ref, slot)
        wait_gather(other)
        wait_scatter(other)
        wait_scatter(slot)


def _moe_sparse(h_ext, plan, w1, w3, w2, tm):
    T = h_ext.shape[0]
    src, dst, tile_gid, tile_cnt = plan
    n_tiles = src.shape[0]
    kern = functools.partial(_moe_sparse_kernel, tm=tm)
    smem_row = lambda f: pl.BlockSpec((None, 1, tm), f, memory_space=pltpu.SMEM)
    grid_spec = pltpu.PrefetchScalarGridSpec(
        num_scalar_prefetch=2,
        grid=(n_tiles,),
        in_specs=[smem_row(lambda i, tg, tc: (i, 0, 0)),
                  smem_row(lambda i, tg, tc: (jnp.minimum(i + 1, n_tiles - 1), 0, 0)),
                  smem_row(lambda i, tg, tc: (jnp.maximum(i - 1, 0), 0, 0)),
                  smem_row(lambda i, tg, tc: (i, 0, 0)),
                  pl.BlockSpec(memory_space=pl.ANY),
                  pl.BlockSpec((None, MOE_E, D_MODEL, MOE_F), lambda i, tg, tc: (tg[i], 0, 0, 0)),
                  pl.BlockSpec((None, MOE_E, D_MODEL, MOE_F), lambda i, tg, tc: (tg[i], 0, 0, 0)),
                  pl.BlockSpec((None, MOE_E, MOE_F, D_MODEL), lambda i, tg, tc: (tg[i], 0, 0, 0))],
        out_specs=pl.BlockSpec(memory_space=pl.ANY),
        scratch_shapes=[pltpu.VMEM((2, tm, D_MODEL + LANES), F32), pltpu.VMEM((2, tm, D_MODEL), F32),
                        pltpu.SemaphoreType.DMA((2,)), pltpu.SemaphoreType.DMA((2,))])
    return pl.pallas_call(
        kern,
        out_shape=jax.ShapeDtypeStruct((T + 2 * tm, D_MODEL), F32),
        grid_spec=grid_spec,
        compiler_params=_cparams("arbitrary"),
        name="moe_sparse",
    )(tile_gid, tile_cnt, src, src, dst, dst, h_ext, w1, w3, w2)


def _moe_combine_kernel(x_ref, y_ref, gt_ref, o_ref):
    o_ref[...] = x_ref[...] + gt_ref[...] * y_ref[...]


def _moe_combine_norm_kernel(x_ref, y_ref, gt_ref, g_ref, o_ref):
    x = x_ref[...] + gt_ref[...] * y_ref[...]
    ms = jnp.mean(x * x, axis=-1, keepdims=True)
    o_ref[...] = (x * lax.rsqrt(ms + RMS_EPS)) * g_ref[...]


def _moe_combine(x, y, gt, tm, final_g=None):
    B, L, _ = x.shape
    nb = L // tm
    in_specs = [pl.BlockSpec((None, tm, D_MODEL), lambda b, i: (b, i, 0)),
                pl.BlockSpec((tm, D_MODEL), lambda b, i: (b * nb + i, 0)),
                _mod_spec(gt)]
    args = (x, y, gt)
    if final_g is not None:
        in_specs.append(_vec_spec(D_MODEL))
        args += (final_g,)
    return pl.pallas_call(
        _moe_combine_kernel if final_g is None else _moe_combine_norm_kernel,
        out_shape=jax.ShapeDtypeStruct((B, L, D_MODEL), F32),
        grid=(B, nb),
        in_specs=in_specs,
        out_specs=pl.BlockSpec((None, tm, D_MODEL), lambda b, i: (b, i, 0)),
        compiler_params=_cparams("parallel", "parallel"),
        name="moe_combine",
    )(*args)


def _row(v):
    return v.reshape(1, -1)


def _prepare_weights(P):
    W = {}
    W["cv_pw1_w"] = P["cv_pw1_w"].astype(BF16)
    W["cv_pw2_w"] = P["cv_pw2_w"].astype(BF16)
    W["gdn_w_qkvz"] = P["gdn_w_qkvz"].astype(BF16)
    W["gdn_w_ba"] = jnp.pad(P["gdn_w_ba"], ((0, 0), (0, 0), (0, LANES - 4 * GDN_H)))
    a_neg = jnp.exp(P["gdn_a_log"].astype(F32))
    zeros = jnp.zeros_like(a_neg)
    a_cols = jnp.stack([zeros, a_neg], axis=2).reshape(a_neg.shape[0], 4 * GDN_H)
    dt_cols = jnp.stack([zeros, P["gdn_dt_bias"].astype(F32)], axis=2).reshape(a_neg.shape[0], 4 * GDN_H)
    W["gdn_a_row"] = jnp.pad(a_cols, ((0, 0), (0, LANES - 4 * GDN_H)))
    W["gdn_dt_row"] = jnp.pad(dt_cols, ((0, 0), (0, LANES - 4 * GDN_H)))
    W["gdn_w_o"] = P["gdn_w_o"].astype(BF16)
    W["na_w_qkv"] = P["na_w_qkv"].astype(BF16)
    W["na_w_o"] = P["na_w_o"].astype(BF16)
    W["na_rpb"] = P["na_rpb"]
    W["moe_w1"] = P["moe_w1"].astype(BF16)
    W["moe_w3"] = P["moe_w3"].astype(BF16)
    W["moe_w2"] = P["moe_w2"].astype(BF16)
    w_r = jnp.concatenate([P["moe_wg"], P["moe_we"]], axis=-1)
    n_r = w_r.shape[-1]
    W["moe_w_rt"] = jnp.pad(w_r.transpose(0, 2, 1), ((0, 0), (0, ROUTER_ROWS - n_r), (0, 0)))
    b_r = jnp.concatenate([P["moe_bg"], P["moe_be"]], axis=-1)
    W["moe_b_r"] = jnp.pad(b_r, ((0, 0), (0, ROUTER_ROWS - n_r)))[:, :, None]
    return W


def _trunk(x, mods, P, W, cache):
    B, L, _ = x.shape
    latent = cache is not None
    Bm = mods.shape[1]
    flat = Bm == 1
    tm_seq = min(L, 512)
    n_rows = B * L if flat else L
    tm_tok = min(512, n_rows)
    tm_big = min(1024, n_rows)
    gdn_states, na_ks, na_vs = [], [], []

    def tok(a):
        return a.reshape(1, B * L, a.shape[-1]) if flat else a

    def seq(a):
        return a.reshape(B, L, a.shape[-1])

    for i in range(DEPTH):
        m = mods[i].reshape(Bm, 1, ADA_CHUNKS, D_MODEL)
        sh1, sc1, gt1, sh2, sc2, gt2 = (m[:, :, c] for c in range(ADA_CHUNKS))
        g1 = _row(P["norm1_g"][i])
        kind, j = i % N_MIXERS, i // N_MIXERS
        if kind == 0:
            u = _nm_glu(tok(x), g1, sc1, sh1, W["cv_pw1_w"][j], _row(P["cv_pw1_b"][j]), tm_big, TN_GLU)
            x = _conv_tail(seq(u), P["cv_dw_w"][j], _row(P["cv_dw_b"][j]), _row(P["cv_ln_g"][j]),
                           _row(P["cv_ln_b"][j]), W["cv_pw2_w"][j], _row(P["cv_pw2_b"][j]),
                           seq(x), gt1, tm_seq)
        elif kind == 1:
            qkvz = _nm_linear(tok(x), g1, sc1, sh1, W["gdn_w_qkvz"][j], tm_big, TN_WIDE)
            ba = _nm_small(tok(x), g1, sc1, sh1, W["gdn_w_ba"][j], tm_tok)
            qkv = _gdn_prep(seq(qkvz), P["gdn_conv_w"][j], min(L, tm_tok))
            if latent:
                s0 = cache[0][:, j].astype(F32)
            else:
                s0 = jnp.zeros((B, 2, GDN_H, GDN_DK, GDN_DK), F32)
            a_row, dt_row = W["gdn_a_row"][j:j + 1], W["gdn_dt_row"][j:j + 1]
            o_f, s_f = _gdn_chunk(qkv, seq(ba), a_row, dt_row, s0[:, 0], rev=False)
            o_b, s_b = _gdn_chunk(qkv, seq(ba), a_row, dt_row, s0[:, 1], rev=True)
            gdn_states.append(jnp.stack([s_f, s_b], axis=1))
            x = _gdn_out(tok(o_f), tok(o_b), qkvz, _row(P["gdn_o_g"][j]), W["gdn_w_o"][j],
                         tok(x), gt1, tm_tok)
        else:
            qkv = _nm_linear(tok(x), g1, sc1, sh1, W["na_w_qkv"][j], tm_big, TN_WIDE)
            if latent:
                bias = _na_bias_table(W["na_rpb"][j], L // GRID_W)
                o = _na_attn(seq(qkv), cache[1][:, j], cache[2][:, j], bias)
            else:
                o, k_ctx, v_ctx = _ctx_attn(seq(qkv))
                na_ks.append(k_ctx)
                na_vs.append(v_ctx)
            x = _linear_res(tok(o), W["na_w_o"][j], tok(x), gt1, tm_tok)
        h2, gid = _moe_router(tok(x), _row(P["norm2_g"][i]), sc2, sh2, W["moe_w_rt"][i],
                              W["moe_b_r"][i], tm_big)
        plan = _moe_plan(gid.reshape(B * L), MOE_TILE)
        y = _moe_sparse(h2.reshape(B * L, D_MODEL + LANES), plan, W["moe_w1"][i], W["moe_w3"][i],
                        W["moe_w2"][i], MOE_TILE)
        final_g = _row(P["final_norm_g"]) if i == DEPTH - 1 else None
        x = seq(_moe_combine(tok(x), y, gt2, tm_big, final_g))
    return x, gdn_states, na_ks, na_vs


def kernel(x_prompt, x_sample, state_gdn, cache_na_k, cache_na_v, c, c_ctx,
           ada_w, ada_b, norm1_g, norm2_g,
           cv_pw1_w, cv_pw1_b, cv_dw_w, cv_dw_b, cv_ln_g, cv_ln_b, cv_pw2_w, cv_pw2_b,
           gdn_w_qkvz, gdn_conv_w, gdn_w_ba, gdn_a_log, gdn_dt_bias, gdn_o_g, gdn_w_o,
           na_w_qkv, na_rpb, na_w_o,
           moe_wg, moe_bg, moe_we, moe_be, moe_w1, moe_w3, moe_w2,
           final_norm_g):
    P = dict(norm1_g=norm1_g, norm2_g=norm2_g,
             cv_pw1_w=cv_pw1_w, cv_pw1_b=cv_pw1_b, cv_dw_w=cv_dw_w, cv_dw_b=cv_dw_b,
             cv_ln_g=cv_ln_g, cv_ln_b=cv_ln_b, cv_pw2_w=cv_pw2_w, cv_pw2_b=cv_pw2_b,
             gdn_w_qkvz=gdn_w_qkvz, gdn_conv_w=gdn_conv_w, gdn_w_ba=gdn_w_ba, gdn_a_log=gdn_a_log,
             gdn_dt_bias=gdn_dt_bias, gdn_o_g=gdn_o_g, gdn_w_o=gdn_w_o,
             na_w_qkv=na_w_qkv, na_rpb=na_rpb, na_w_o=na_w_o,
             moe_wg=moe_wg, moe_bg=moe_bg, moe_we=moe_we, moe_be=moe_be,
             moe_w1=moe_w1, moe_w3=moe_w3, moe_w2=moe_w2, final_norm_g=final_norm_g)
    W = _prepare_weights(P)
    nb = c.shape[0]
    rows = 8 * ((nb + 1 + 7) // 8)
    cond = jnp.concatenate([c, c_ctx[None, :], jnp.zeros((rows - nb - 1, D_MODEL), F32)], axis=0)
    mods = _ada_all(cond, ada_w, ada_b)
    y_prompt, gdn_states, na_ks, na_vs = _trunk(x_prompt, mods[:, nb:nb + 1], P, W, None)
    y_sample, _, _, _ = _trunk(x_sample, mods[:, :nb], P, W, (state_gdn, cache_na_k, cache_na_v))
    return (y_prompt, y_sample, jnp.stack(gdn_states, axis=1),
            jnp.stack(na_ks, axis=1), jnp.stack(na_vs, axis=1))
```
